```python
import math
import jax, jax.numpy as jnp
from jax import lax
import numpy as np

D_MODEL = 1024
BATCH = 2
SEQ = 16384
DEPTH = 4

HEAD_DIM = 64
BLK = 128
GRID_W = 64
WIN_HEADS = 4
WIN_KV_HEADS = 2
WINDOW = 128
MLA_HEADS = 4
MLA_Q_RANK = 256
MLA_KV_RANK = 128
MLA_NOPE = 64
MLA_ROPE = 32
MLA_V = 64
AX_HEADS = 4
AX_KV_HEADS = 2
DIFF_HEADS = 4
DIFF_QK = 32
DIFF_V = 64
A_COLS = (WIN_HEADS + 2 * WIN_KV_HEADS) * HEAD_DIM
B_COLS = MLA_Q_RANK + MLA_KV_RANK + MLA_ROPE
C_COLS = (AX_HEADS + 2 * AX_KV_HEADS) * HEAD_DIM
D_COLS = DIFF_HEADS * (4 * DIFF_QK + DIFF_V)
N_IN = A_COLS + B_COLS + C_COLS + D_COLS
D_MIX = WIN_HEADS * HEAD_DIM + MLA_HEADS * MLA_V + AX_HEADS * HEAD_DIM + DIFF_HEADS * DIFF_V
D_FF = 2816
ROPE_BASE = 10000.0
NORM_EPS = 1e-5
NEG_INF = -1e30

kernel_name = "hybrid_parallel_heads_deepnorm_encoder"


def _layer_norm(x, g, b):
    xf = x.astype(jnp.float32)
    mu = jnp.mean(xf, -1, keepdims=True)
    xc = xf - mu
    var = jnp.mean(xc * xc, -1, keepdims=True)
    y = xc * lax.rsqrt(var + NORM_EPS) * g.astype(jnp.float32) + b.astype(jnp.float32)
    return y.astype(x.dtype)


def _rms_norm(x, g):
    xf = x.astype(jnp.float32)
    y = xf * lax.rsqrt(jnp.mean(xf * xf, -1, keepdims=True) + NORM_EPS)
    return (y * g.astype(jnp.float32)).astype(x.dtype)


def _swiglu(h, w_gu, w_down):
    gate, up = jnp.split(h @ w_gu, 2, axis=-1)
    return (jax.nn.silu(gate) * up) @ w_down


def _alibi_slopes(n):
    return 2.0 ** (-8.0 * (jnp.arange(n, dtype=jnp.float32) + 1.0) / n)


def _rope_angles(pos, dim):
    inv = ROPE_BASE ** (-jnp.arange(0, dim, 2, dtype=jnp.float32) / dim)
    ang = pos.astype(jnp.float32)[:, None] * inv[None, :]
    return jnp.cos(ang), jnp.sin(ang)


def _apply_rope(x, cos, sin):
    c = cos[:, None, :]
    s = sin[:, None, :]
    x1, x2 = jnp.split(x.astype(jnp.float32), 2, axis=-1)
    return jnp.concatenate([x1 * c - x2 * s, x1 * s + x2 * c], -1).astype(x.dtype)


def _dense_attention(q, k, v, scale):
    B, S, Hq, Dk = q.shape
    Hkv = k.shape[2]
    G = Hq // Hkv
    nb = S // BLK
    qb = q.reshape(B, nb, BLK, Hkv, G, Dk).transpose(1, 0, 2, 3, 4, 5)

    def block(qi):
        s = jnp.einsum('bqhgd,bkhd->bhgqk', qi, k).astype(jnp.float32) * scale
        p = jax.nn.softmax(s, axis=-1)
        return jnp.einsum('bhgqk,bkhd->bqhgd', p.astype(v.dtype), v)

    o = lax.map(block, qb)
    return o.transpose(1, 0, 2, 3, 4, 5).reshape(B, S, Hq * v.shape[-1])


def _mixer_window(a, sink, slopes):
    B, S, _ = a.shape
    nb = S // BLK
    G = WIN_HEADS // WIN_KV_HEADS
    q, k, v = jnp.split(a, [WIN_HEADS * HEAD_DIM, (WIN_HEADS + WIN_KV_HEADS) * HEAD_DIM], axis=-1)
    qb = q.reshape(B, nb, BLK, WIN_KV_HEADS, G, HEAD_DIM)

    def band(t):
        tp = jnp.pad(t.reshape(B, S, WIN_KV_HEADS, HEAD_DIM), ((0, 0), (BLK, BLK), (0, 0), (0, 0)))
        tp = tp.reshape(B, nb + 2, BLK, WIN_KV_HEADS, HEAD_DIM)
        return jnp.concatenate([tp[:, :-2], tp[:, 1:-1], tp[:, 2:]], axis=2)

    kw = band(k)
    vw = band(v)
    s = jnp.einsum('bnqhgd,bnkhd->bnhgqk', qb, kw).astype(jnp.float32) * (HEAD_DIM ** -0.5)
    dist = jnp.abs(jnp.arange(BLK)[:, None] - jnp.arange(3 * BLK)[None, :] + BLK)
    kpos = jnp.arange(nb)[:, None] * BLK - BLK + jnp.arange(3 * BLK)[None, :]
    allowed = (dist <= WINDOW)[None] & ((kpos >= 0) & (kpos < S))[:, None, :]
    bias = -slopes.reshape(WIN_KV_HEADS, G, 1, 1) * dist.astype(jnp.float32)
    s = jnp.where(allowed[None, :, None, None], s + bias, NEG_INF)
    sink_logit = jnp.broadcast_to(sink.astype(jnp.float32).reshape(1, 1, WIN_KV_HEADS, G, 1, 1),
                                  s.shape[:-1] + (1,))
    p = jax.nn.softmax(jnp.concatenate([s, sink_logit], axis=-1), axis=-1)[..., :-1]
    o = jnp.einsum('bnhgqk,bnkhd->bnqhgd', p.astype(vw.dtype), vw)
    return o.reshape(B, S, WIN_HEADS * HEAD_DIM)


def _mixer_mla(b, q_norm_g, w_uq, kv_norm_g, w_ukv, pos):
    B, S, _ = b.shape
    c_q, c_kv, k_rope = jnp.split(b, [MLA_Q_RANK, MLA_Q_RANK + MLA_KV_RANK], axis=-1)
    q = (_rms_norm(c_q, q_norm_g) @ w_uq).reshape(B, S, MLA_HEADS, MLA_NOPE + MLA_ROPE)
    kv = (_rms_norm(c_kv, kv_norm_g) @ w_ukv).reshape(B, S, MLA_HEADS, MLA_NOPE + MLA_V)
    cos, sin = _rope_angles(pos, MLA_ROPE)
    q = jnp.concatenate([q[..., :MLA_NOPE], _apply_rope(q[..., MLA_NOPE:], cos, sin)], axis=-1)
    k_r = _apply_rope(k_rope[:, :, None, :], cos, sin)
    k = jnp.concatenate([kv[..., :MLA_NOPE], jnp.broadcast_to(k_r, (B, S, MLA_HEADS, MLA_ROPE))], axis=-1)
    v = kv[..., MLA_NOPE:]
    return _dense_attention(q, k, v, (MLA_NOPE + MLA_ROPE) ** -0.5)


def _mixer_axial(c, q_g, k_g):
    B, S, _ = c.shape
    q, k, v = jnp.split(c, [AX_HEADS * HEAD_DIM, (AX_HEADS + AX_KV_HEADS) * HEAD_DIM], axis=-1)
    q = _rms_norm(q.reshape(B, S, AX_HEADS, HEAD_DIM), q_g)
    k = _rms_norm(k.reshape(B, S, AX_KV_HEADS, HEAD_DIM), k_g)
    v = v.reshape(B, S, AX_KV_HEADS, HEAD_DIM)
    rows = S // GRID_W
    row = jnp.repeat(jnp.arange(rows, dtype=jnp.int32), GRID_W)
    col = jnp.tile(jnp.arange(GRID_W, dtype=jnp.int32), rows)
    half = HEAD_DIM // 2
    cr, sr = _rope_angles(row, half)
    cc, sc = _rope_angles(col, half)

    def axial(t):
        return jnp.concatenate([_apply_rope(t[..., :half], cr, sr), _apply_rope(t[..., half:], cc, sc)], axis=-1)

    return _dense_attention(axial(q), axial(k), v, HEAD_DIM ** -0.5)


def _mixer_diff(d, lam_params, subln_g, slopes, layer_idx):
    B, S, _ = d.shape
    H = DIFF_HEADS
    q, k, v = jnp.split(d, [H * 2 * DIFF_QK, H * 4 * DIFF_QK], axis=-1)
    q = q.reshape(B, S, H, 2, DIFF_QK)
    k = k.reshape(B, S, H, 2, DIFF_QK)
    v = v.reshape(B, S, H, DIFF_V)
    lam_init = 0.8 - 0.6 * math.exp(-0.3 * layer_idx)
    lp = lam_params.astype(jnp.float32)
    lam = jnp.exp(jnp.sum(lp[0] * lp[1])) - jnp.exp(jnp.sum(lp[2] * lp[3])) + lam_init
    nb = S // BLK
    qb = q.reshape(B, nb, BLK, H, 2, DIFF_QK).transpose(1, 0, 2, 3, 4, 5)
    kpos = jnp.arange(S)
    scale = DIFF_QK ** -0.5

    def block(args):
        qi, i = args
        tq = i * BLK + jnp.arange(BLK)
        dist = jnp.abs(tq[:, None] - kpos[None, :]).astype(jnp.float32)
        bias = -slopes[:, None, None] * dist
        s = jnp.einsum('bqhmd,bkhmd->bhmqk', qi, k).astype(jnp.float32) * scale + bias[None, :, None]
        p = jax.nn.softmax(s, axis=-1)
        w = p[:, :, 0] - lam * p[:, :, 1]
        return jnp.einsum('bhqk,bkhd->bqhd', w.astype(v.dtype), v)

    o = lax.map(block, (qb, jnp.arange(nb)))
    o = o.transpose(1, 0, 2, 3, 4).reshape(B, S, H, DIFF_V)
    o = _rms_norm(o, subln_g) * (1.0 - lam_init)
    return o.reshape(B, S, H * DIFF_V)


def _token_mixers(h, w_in, sink, mla_q_norm, mla_w_uq, mla_kv_norm, mla_w_ukv,
                  ax_q_norm, ax_k_norm, diff_lambda, diff_subln, w_out, layer_idx):
    S = h.shape[1]
    proj = h @ w_in
    a, b, c, d = jnp.split(proj, [A_COLS, A_COLS + B_COLS, A_COLS + B_COLS + C_COLS], axis=-1)
    pos = jnp.arange(S, dtype=jnp.int32)
    slopes = _alibi_slopes(WIN_HEADS + DIFF_HEADS)
    o_a = _mixer_window(a, sink, slopes[:WIN_HEADS])
    o_b = _mixer_mla(b, mla_q_norm, mla_w_uq, mla_kv_norm, mla_w_ukv, pos)
    o_c = _mixer_axial(c, ax_q_norm, ax_k_norm)
    o_d = _mixer_diff(d, diff_lambda, diff_subln, slopes[WIN_HEADS:], layer_idx)
    return jnp.concatenate([o_a, o_b, o_c, o_d], axis=-1) @ w_out


def setup_inputs(seed: int = 0) -> dict:
    key = jax.random.key(seed)
    ks = jax.random.split(key, 20)
    f32 = jnp.float32
    beta = (8 * DEPTH) ** -0.25

    def nrm(k, shape, scale):
        return jax.random.normal(k, shape, f32) * scale

    return {
        "x": nrm(ks[0], (BATCH, SEQ, D_MODEL), 1.0),
        "w_in": nrm(ks[1], (DEPTH, D_MODEL, N_IN), D_MODEL ** -0.5),
        "win_sink": nrm(ks[2], (DEPTH, WIN_HEADS), 0.5),
        "mla_q_norm": 1.0 + nrm(ks[3], (DEPTH, MLA_Q_RANK), 0.02),
        "mla_w_uq": nrm(ks[4], (DEPTH, MLA_Q_RANK, MLA_HEADS * (MLA_NOPE + MLA_ROPE)), MLA_Q_RANK ** -0.5),
        "mla_kv_norm": 1.0 + nrm(ks[5], (DEPTH, MLA_KV_RANK), 0.02),
        "mla_w_ukv": nrm(ks[6], (DEPTH, MLA_KV_RANK, MLA_HEADS * (MLA_NOPE + MLA_V)), MLA_KV_RANK ** -0.5),
        "ax_q_norm": 1.0 + nrm(ks[7], (DEPTH, HEAD_DIM), 0.02),
        "ax_k_norm": 1.0 + nrm(ks[8], (DEPTH, HEAD_DIM), 0.02),
        "diff_lambda": nrm(ks[9], (DEPTH, 4, DIFF_QK), 0.1),
        "diff_subln": 1.0 + nrm(ks[10], (DEPTH, DIFF_V), 0.02),
        "w_out": nrm(ks[11], (DEPTH, D_MIX, D_MODEL), beta * D_MIX ** -0.5),
        "ffn_w_gu": nrm(ks[12], (DEPTH, 2, D_MODEL, 2 * D_FF), D_MODEL ** -0.5),
        "ffn_w_down": nrm(ks[13], (DEPTH, 2, D_FF, D_MODEL), beta * D_FF ** -0.5),
        "ln_g": 1.0 + nrm(ks[14], (DEPTH, 3, D_MODEL), 0.02),
        "ln_b": nrm(ks[15], (DEPTH, 3, D_MODEL), 0.02),
    }


def reference(x, w_in, win_sink, mla_q_norm, mla_w_uq, mla_kv_norm, mla_w_ukv,
              ax_q_norm, ax_k_norm, diff_lambda, diff_subln, w_out,
              ffn_w_gu, ffn_w_down, ln_g, ln_b):
    alpha = (2 * DEPTH) ** 0.25
    for l in range(DEPTH):
        x = _layer_norm(alpha * x + 0.5 * _swiglu(x, ffn_w_gu[l, 0], ffn_w_down[l, 0]), ln_g[l, 0], ln_b[l, 0])
        mix = _token_mixers(x, w_in[l], win_sink[l], mla_q_norm[l], mla_w_uq[l], mla_kv_norm[l], mla_w_ukv[l],
                            ax_q_norm[l], ax_k_norm[l], diff_lambda[l], diff_subln[l], w_out[l], l)
        x = _layer_norm(alpha * x + mix, ln_g[l, 1], ln_b[l, 1])
        x = _layer_norm(alpha * x + 0.5 * _swiglu(x, ffn_w_gu[l, 1], ffn_w_down[l, 1]), ln_g[l, 2], ln_b[l, 2])
    return x
```

```python
import functools
import math

import numpy as np
import jax
import jax.numpy as jnp
from jax import lax
from jax.experimental import pallas as pl
from jax.experimental.pallas import tpu as pltpu

F32 = jnp.float32
BF16 = jnp.bfloat16

HEAD_DIM = 64
HEAD_SHIFT = 6
GRID_W = 64
WIN_HEADS, WIN_KV_HEADS, WINDOW = 4, 2, 128
MLA_HEADS, MLA_Q_RANK, MLA_KV_RANK, MLA_NOPE, MLA_ROPE, MLA_V = 4, 256, 128, 64, 32, 64
AX_HEADS, AX_KV_HEADS = 4, 2
DIFF_HEADS, DIFF_QK, DIFF_V = 4, 32, 64
A_COLS = (WIN_HEADS + 2 * WIN_KV_HEADS) * HEAD_DIM
B_COLS = MLA_Q_RANK + MLA_KV_RANK + MLA_ROPE
C_COLS = (AX_HEADS + 2 * AX_KV_HEADS) * HEAD_DIM
D_COLS = DIFF_HEADS * (4 * DIFF_QK + DIFF_V)
ROPE_BASE = 10000.0
NORM_EPS = 1e-5
NEG_INF = -1e30

LANES = 128
BF16_SUBLANES = 16
V_ROWS = HEAD_DIM + BF16_SUBLANES
VMEM_LIMIT = 56 * 1024 * 1024

TOK_TILE = 512
KV_CHUNK = 512
Q_TILE = 256
FF_CHUNK = 256


def _cparams(sem):
    return pltpu.CompilerParams(dimension_semantics=sem, vmem_limit_bytes=VMEM_LIMIT)


def _resident(shape):
    nd = len(shape)
    return pl.BlockSpec(shape, lambda *_: (0,) * nd, pipeline_mode=pl.Buffered(1))


def _layer_norm_rows(y, g, b):
    mu = jnp.mean(y, axis=-1, keepdims=True)
    yc = y - mu
    var = jnp.mean(yc * yc, axis=-1, keepdims=True)
    return yc * lax.rsqrt(var + NORM_EPS) * g + b


def _ffn_kernel(x_ref, wgu_ref, wd_ref, g_ref, b_ref, o_ref, acc_ref, *, alpha, n_chunks):
    x = x_ref[...]
    xb = x.astype(BF16)
    acc_ref[...] = jnp.zeros_like(acc_ref)

    def body(c, carry):
        gu = jnp.dot(xb, wgu_ref[c], preferred_element_type=F32)
        gate = gu[:, :FF_CHUNK]
        up = gu[:, FF_CHUNK:]
        h = (gate * jax.nn.sigmoid(gate) * up).astype(BF16)
        acc_ref[...] += jnp.dot(h, wd_ref[c], preferred_element_type=F32)
        return carry

    lax.fori_loop(0, n_chunks, body, 0)
    y = alpha * x + 0.5 * acc_ref[...]
    o_ref[...] = _layer_norm_rows(y, g_ref[...], b_ref[...])


def _ffn(x2d, wgu, wd, g, b, alpha):
    n_tok, d = x2d.shape
    n_chunks = wgu.shape[0]
    tm = min(TOK_TILE, n_tok)
    return pl.pallas_call(
        functools.partial(_ffn_kernel, alpha=alpha, n_chunks=n_chunks),
        grid=(n_tok // tm,),
        in_specs=[
            pl.BlockSpec((tm, d), lambda i: (i, 0)),
            _resident(wgu.shape),
            _resident(wd.shape),
            _resident(g.shape),
            _resident(b.shape),
        ],
        out_specs=pl.BlockSpec((tm, d), lambda i: (i, 0)),
        out_shape=jax.ShapeDtypeStruct((n_tok, d), F32),
        scratch_shapes=[pltpu.VMEM((tm, d), F32)],
        compiler_params=_cparams(("parallel",)),
    )(x2d, wgu, wd, g, b)


_C_AQ, _C_AK, _C_CQ, _C_CK, _C_CQR, _C_CKR = 0, 256, 384, 640, 768, 1024
_C_DQ, _C_DK, _C_BQ, _C_BKV, _C_BKR, _C_BKRR, _C_END = 1152, 1664, 2176, 2432, 2560, 2688, 2816
_R_AV, _R_CV, _R_DV, _R_END = 0, 128, 256, 512


def _split_hi_lo(v):
    hi = v.astype(BF16)
    lo = (v - hi.astype(F32)).astype(BF16)
    return hi, lo


def _prep_kernel(x_ref, wtok_ref, wvt_ref, wuq_ref, wuqr_ref, wukvk_ref, wukvvt_ref, gavg_ref,
                 gq_ref, gkv_ref, axq_ref, axqr_ref, axk_ref, axkr_ref,
                 cosb_ref, sinb_ref, cosc_ref, sinc_ref, augk_ref,
                 qa_ref, ka_ref, vta_ref, qb_ref, kb_ref, vtb_ref,
                 qc_ref, kc_ref, vtc_ref, qd_ref, kd_ref, vtd_ref):
    xb = x_ref[0].astype(BF16)
    tm = xb.shape[0]
    pm = jnp.dot(xb, wtok_ref[...], preferred_element_type=F32)
    nt = (((1,), (1,)), ((), ()))
    pvt = lax.dot_general(wvt_ref[...], xb, nt, preferred_element_type=F32)

    row = lax.broadcasted_iota(jnp.int32, (BF16_SUBLANES, tm), 0)
    ones_tile = jnp.where(row == 0, 1.0, 0.0).astype(BF16)

    def put_vt(ref, head, vt):
        ref[0, head, 0, 0:HEAD_DIM, :] = vt.astype(BF16)
        ref[0, head, 0, HEAD_DIM:V_ROWS, :] = ones_tile

    qa_ref[0] = (pm[:, _C_AQ:_C_AQ + 256] * (HEAD_DIM ** -0.5)).astype(BF16)
    ka_ref[0] = pm[:, _C_AK:_C_AK + 128].astype(BF16)
    for h in range(WIN_KV_HEADS):
        put_vt(vta_ref, h, pvt[_R_AV + 64 * h:_R_AV + 64 * (h + 1)])

    cosc = cosc_ref[...]
    sinc = sinc_ref[...]
    gavg = gavg_ref[...]

    def head_rms_scale(v):
        hi, lo = _split_hi_lo(v * v)
        ms = (jnp.dot(hi, gavg, preferred_element_type=F32)
              + jnp.dot(lo, gavg, preferred_element_type=F32))
        return lax.rsqrt(ms + NORM_EPS)

    def axial(off, off_rot, g_ref_, gr_ref_, scale):
        v = pm[:, off:off + LANES]
        r = head_rms_scale(v)
        vn = v * r * g_ref_[...]
        vrn = pm[:, off_rot:off_rot + LANES] * r * gr_ref_[...]
        return ((vn * cosc + vrn * sinc) * scale).astype(BF16)

    for grp in range(2):
        qc_ref[0, :, grp * LANES:(grp + 1) * LANES] = axial(
            _C_CQ + grp * LANES, _C_CQR + grp * LANES, axq_ref, axqr_ref, HEAD_DIM ** -0.5)
    kc_ref[0] = axial(_C_CK, _C_CKR, axk_ref, axkr_ref, 1.0)
    for h in range(AX_KV_HEADS):
        put_vt(vtc_ref, h, pvt[_R_CV + 64 * h:_R_CV + 64 * (h + 1)])

    qd_ref[0] = (pm[:, _C_DQ:_C_DQ + 512] * (DIFF_QK ** -0.5)).astype(BF16)
    augk = augk_ref[...]
    for h in range(DIFF_HEADS):
        kd_ref[0, :, h * LANES:(h + 1) * LANES] = (
            pm[:, _C_DK + h * LANES:_C_DK + (h + 1) * LANES] + augk).astype(BF16)
        put_vt(vtd_ref, h, pvt[_R_DV + 64 * h:_R_DV + 64 * (h + 1)])

    cosb = cosb_ref[...]
    sinb = sinb_ref[...]
    cq = pm[:, _C_BQ:_C_BQ + MLA_Q_RANK]
    cqn = (cq * lax.rsqrt(jnp.mean(cq * cq, axis=-1, keepdims=True) + NORM_EPS) * gq_ref[...]).astype(BF16)
    qw = jnp.dot(cqn, wuq_ref[...], preferred_element_type=F32)
    qwr = jnp.dot(cqn, wuqr_ref[...], preferred_element_type=F32)
    ckv = pm[:, _C_BKV:_C_BKV + MLA_KV_RANK]
    ckvn = (ckv * lax.rsqrt(jnp.mean(ckv * ckv, axis=-1, keepdims=True) + NORM_EPS) * gkv_ref[...]).astype(BF16)
    kw = jnp.dot(ckvn, wukvk_ref[...], preferred_element_type=F32)
    vbt = lax.dot_general(wukvvt_ref[...], ckvn, nt, preferred_element_type=F32)
    k_rope = pm[:, _C_BKR:_C_BKR + LANES] * cosb + pm[:, _C_BKRR:_C_BKRR + LANES] * sinb
    q_scale = (MLA_NOPE + MLA_ROPE) ** -0.5
    for h in range(MLA_HEADS):
        sl = slice(h * LANES, (h + 1) * LANES)
        qb_ref[0, :, sl] = ((qw[:, sl] * cosb + qwr[:, sl] * sinb) * q_scale).astype(BF16)
        kb_ref[0, :, sl] = (kw[:, sl] + k_rope).astype(BF16)
        put_vt(vtb_ref, h, vbt[64 * h:64 * (h + 1)])


def _prep(x, p, tabs):
    bsz, seq, d = x.shape
    tm = KV_CHUNK
    nch = seq // tm
    tok = lambda w: pl.BlockSpec((1, tm, w), lambda b, i: (b, i, 0))
    tab = pl.BlockSpec((tm, LANES), lambda b, i: (i, 0))
    vt = lambda nh: pl.BlockSpec((1, nh, 1, V_ROWS, tm), lambda b, i: (b, 0, i, 0, 0))
    vt_shape = lambda nh: jax.ShapeDtypeStruct((bsz, nh, nch, V_ROWS, tm), BF16)
    tok_shape = lambda w: jax.ShapeDtypeStruct((bsz, seq, w), BF16)
    weights = [p["w_tok"], p["w_vt"], p["w_uq"], p["w_uq_rot"], p["w_ukv_k"], p["w_ukv_vt"], p["g_avg"],
               p["g_q"], p["g_kv"], p["ax_q"], p["ax_q_rot"], p["ax_k"], p["ax_k_rot"]]
    return pl.pallas_call(
        _prep_kernel,
        grid=(bsz, nch),
        in_specs=[tok(d)] + [_resident(w.shape) for w in weights] + [tab] * 5,
        out_specs=[tok(256), tok(128), vt(2), tok(512), tok(512), vt(4),
                   tok(256), tok(128), vt(2), tok(512), tok(512), vt(4)],
        out_shape=[tok_shape(256), tok_shape(128), vt_shape(2), tok_shape(512), tok_shape(512), vt_shape(4),
                   tok_shape(256), tok_shape(128), vt_shape(2), tok_shape(512), tok_shape(512), vt_shape(4)],
        compiler_params=_cparams(("parallel", "parallel")),
    )(x, *weights, tabs["cos_b"], tabs["sin_b"], tabs["cos_c"], tabs["sin_c"], tabs["aug_k"])


_NT = (((1,), (1,)), ((), ()))


def _softmax_step(s, vt, m, acc):
    m_new = jnp.maximum(m, jnp.max(s, axis=0, keepdims=True))
    alpha = jnp.exp(m - m_new)
    p = jnp.exp(s - m_new).astype(BF16)
    acc = acc * alpha + jnp.dot(vt, p, preferred_element_type=F32)
    return m_new, acc


def _dense_kernel(q_ref, k_ref, vt_ref, o_ref, *, n_chunks, kv_group):
    q = q_ref[0]
    tq = q.shape[0]
    if kv_group:
        kv_head = pl.program_id(1) // kv_group
        lane = lax.broadcasted_iota(jnp.int32, (tq, LANES), 1)
        q = jnp.where((lane >> HEAD_SHIFT) == kv_head, q.astype(F32), 0.0).astype(BF16)

    def body(j, carry):
        m, acc = carry
        kc = k_ref[0, pl.ds(pl.multiple_of(j * KV_CHUNK, KV_CHUNK), KV_CHUNK), :]
        s = lax.dot_general(kc, q, _NT, preferred_element_type=F32)
        return _softmax_step(s, vt_ref[0, 0, j], m, acc)

    m0 = jnp.full((1, tq), NEG_INF, F32)
    acc0 = jnp.zeros((V_ROWS, tq), F32)
    m, acc = lax.fori_loop(0, n_chunks, body, (m0, acc0))
    o_ref[0] = (acc[:HEAD_DIM] / acc[HEAD_DIM:HEAD_DIM + 1]).astype(BF16)


def _dense_attention(q, k, vt, *, n_heads, kv_group):
    bsz, seq, _ = q.shape
    nch = vt.shape[2]
    tq = min(Q_TILE, seq)
    if kv_group:
        q_map = lambda b, h, i: (b, i, h % kv_group)
        k_map = lambda b, h, i: (b, 0, 0)
        vt_map = lambda b, h, i: (b, h // kv_group, 0, 0, 0)
    else:
        q_map = lambda b, h, i: (b, i, h)
        k_map = lambda b, h, i: (b, 0, h)
        vt_map = lambda b, h, i: (b, h, 0, 0, 0)
    return pl.pallas_call(
        functools.partial(_dense_kernel, n_chunks=nch, kv_group=kv_group),
        grid=(bsz, n_heads, seq // tq),
        in_specs=[
            pl.BlockSpec((1, tq, LANES), q_map),
            pl.BlockSpec((1, seq, LANES), k_map),
            pl.BlockSpec((1, 1, nch, V_ROWS, KV_CHUNK), vt_map),
        ],
        out_specs=pl.BlockSpec((1, HEAD_DIM, tq), lambda b, h, i: (b, h, i)),
        out_shape=jax.ShapeDtypeStruct((bsz, n_heads * HEAD_DIM, seq), BF16),
        compiler_params=_cparams(("parallel", "parallel", "arbitrary")),
    )(q, k, vt)


_AUG = 2 * DIFF_QK


def _diff_kernel(lam_ref, g_ref, q_ref, k_ref, vt_ref, o_ref, *, n_chunks, lam_init, slope_exp0):
    h = pl.program_id(1)
    i = pl.program_id(2)
    tq = q_ref.shape[1]
    qf = q_ref[0].astype(F32)
    lane = lax.broadcasted_iota(jnp.int32, (tq, LANES), 1)
    qpos = i * tq + lax.broadcasted_iota(jnp.int32, (tq, LANES), 0)
    slope_bits = jnp.full((1, LANES), (127 - slope_exp0), jnp.int32) - h
    slope = lax.bitcast_convert_type(slope_bits << 23, F32)
    q_hi = ((qpos >> 7) << 7).astype(F32)
    q_lo = (qpos & 127).astype(F32)
    aug = jnp.where(lane == _AUG, -slope * q_hi,
                    jnp.where(lane == _AUG + 1, -slope * q_lo,
                              jnp.where((lane == _AUG + 2) | (lane == _AUG + 3), slope, 0.0)))
    q1 = jnp.where(lane < DIFF_QK, qf, 0.0)
    q2 = jnp.where((lane >= DIFF_QK) & (lane < 2 * DIFF_QK), qf, 0.0)

    def both(extra):
        return jnp.concatenate([q1 + extra, q2 + extra], axis=0).astype(BF16)

    q_before = both(aug)
    q_after = both(-aug)
    q_plain = both(jnp.zeros_like(aug))

    def chunk(j):
        return k_ref[0, pl.ds(pl.multiple_of(j * KV_CHUNK, KV_CHUNK), KV_CHUNK), :]

    def make_body(qm):
        def body(j, carry):
            s = lax.dot_general(chunk(j), qm, _NT, preferred_element_type=F32)
            return _softmax_step(s, vt_ref[0, 0, j], *carry)
        return body

    carry = (jnp.full((1, 2 * tq), NEG_INF, F32), jnp.zeros((V_ROWS, 2 * tq), F32))
    jd = (i * tq) // KV_CHUNK
    carry = lax.fori_loop(0, jd, make_body(q_before), carry)
    kpos = jd * KV_CHUNK + lax.broadcasted_iota(jnp.int32, (KV_CHUNK, tq), 0)
    tpos = i * tq + lax.broadcasted_iota(jnp.int32, (KV_CHUNK, tq), 1)
    bias = -slope[:, :1] * jnp.abs(tpos - kpos).astype(F32)
    s = lax.dot_general(chunk(jd), q_plain, _NT, preferred_element_type=F32)
    s = s + jnp.concatenate([bias, bias], axis=1)
    carry = _softmax_step(s, vt_ref[0, 0, jd], *carry)
    m, acc = lax.fori_loop(jd + 1, n_chunks, make_body(q_after), carry)

    lp = lam_ref[...]
    lam = (jnp.exp(jnp.sum(lp[0:1] * lp[1:2], axis=-1, keepdims=True))
           - jnp.exp(jnp.sum(lp[2:3] * lp[3:4], axis=-1, keepdims=True)) + lam_init)
    o1 = acc[:HEAD_DIM, :tq] / acc[HEAD_DIM:HEAD_DIM + 1, :tq]
    o2 = acc[:HEAD_DIM, tq:] / acc[HEAD_DIM:HEAD_DIM + 1, tq:]
    o = o1 - lam * o2
    o = o * lax.rsqrt(jnp.mean(o * o, axis=0, keepdims=True) + NORM_EPS)
    o_ref[0] = (o * g_ref[...] * (1.0 - lam_init)).astype(BF16)


def _diff_attention(q, k, vt, lam_params, subln_col, *, layer_idx, slope_exp0):
    bsz, seq, _ = q.shape
    nch = vt.shape[2]
    tq = min(Q_TILE, seq)
    assert KV_CHUNK % tq == 0 and seq <= 256 * LANES
    lam_init = 0.8 - 0.6 * math.exp(-0.3 * layer_idx)
    return pl.pallas_call(
        functools.partial(_diff_kernel, n_chunks=nch, lam_init=lam_init, slope_exp0=slope_exp0),
        grid=(bsz, DIFF_HEADS, seq // tq),
        in_specs=[
            _resident(lam_params.shape),
            _resident(subln_col.shape),
            pl.BlockSpec((1, tq, LANES), lambda b, h, i: (b, i, h)),
            pl.BlockSpec((1, seq, LANES), lambda b, h, i: (b, 0, h)),
            pl.BlockSpec((1, 1, nch, V_ROWS, KV_CHUNK), lambda b, h, i: (b, h, 0, 0, 0)),
        ],
        out_specs=pl.BlockSpec((1, HEAD_DIM, tq), lambda b, h, i: (b, h, i)),
        out_shape=jax.ShapeDtypeStruct((bsz, DIFF_HEADS * DIFF_V, seq), BF16),
        compiler_params=_cparams(("parallel", "parallel", "arbitrary")),
    )(lam_params, subln_col, q, k, vt)


def _window_kernel(sink_ref, q_ref, k_ref, vt_ref, o_ref, *, seq):
    i = pl.program_id(1)
    tq = q_ref.shape[1]
    q0 = i * tq
    j_lo = jnp.maximum(q0 - WINDOW, 0) // KV_CHUNK
    j_hi = jnp.minimum(q0 + tq - 1 + WINDOW, seq - 1) // KV_CHUNK
    lane = lax.broadcasted_iota(jnp.int32, (tq, LANES), 1)
    acc_row = lax.broadcasted_iota(jnp.int32, (V_ROWS, tq), 0)
    tpos = q0 + lax.broadcasted_iota(jnp.int32, (KV_CHUNK, tq), 1)
    krow = lax.broadcasted_iota(jnp.int32, (KV_CHUNK, tq), 0)
    group = WIN_HEADS // WIN_KV_HEADS
    for head in range(WIN_HEADS):
        kv_head, slab = head // group, head % group
        slope = 2.0 ** -(head + 1)
        qf = q_ref[0, :, slab * LANES:(slab + 1) * LANES].astype(F32)
        q = jnp.where((lane >> HEAD_SHIFT) == kv_head, qf, 0.0).astype(BF16)

        def body(j, carry, q=q, slope=slope, kv_head=kv_head):
            kc = k_ref[0, pl.ds(pl.multiple_of(j * KV_CHUNK, KV_CHUNK), KV_CHUNK), :]
            s = lax.dot_general(kc, q, _NT, preferred_element_type=F32)
            dist = jnp.abs(tpos - (j * KV_CHUNK + krow))
            s = jnp.where(dist <= WINDOW, s - slope * dist.astype(F32), NEG_INF)
            return _softmax_step(s, vt_ref[0, kv_head, j], *carry)

        m0 = jnp.full((1, tq), sink_ref[head], F32)
        acc0 = jnp.where(acc_row == HEAD_DIM, 1.0, 0.0)
        m, acc = lax.fori_loop(j_lo, j_hi + 1, body, (m0, acc0))
        o_ref[0, head * HEAD_DIM:(head + 1) * HEAD_DIM, :] = (
            acc[:HEAD_DIM] / acc[HEAD_DIM:HEAD_DIM + 1]).astype(BF16)


def _window_attention(q, k, vt, sink):
    bsz, seq, _ = q.shape
    nch = vt.shape[2]
    tq = min(Q_TILE, seq)
    return pl.pallas_call(
        functools.partial(_window_kernel, seq=seq),
        grid=(bsz, seq // tq),
        in_specs=[
            pl.BlockSpec(memory_space=pltpu.SMEM),
            pl.BlockSpec((1, tq, 2 * LANES), lambda b, i: (b, i, 0)),
            pl.BlockSpec((1, seq, LANES), lambda b, i: (b, 0, 0)),
            pl.BlockSpec((1, WIN_KV_HEADS, nch, V_ROWS, KV_CHUNK), lambda b, i: (b, 0, 0, 0, 0)),
        ],
        out_specs=pl.BlockSpec((1, WIN_HEADS * HEAD_DIM, tq), lambda b, i: (b, 0, i)),
        out_shape=jax.ShapeDtypeStruct((bsz, WIN_HEADS * HEAD_DIM, seq), BF16),
        compiler_params=_cparams(("parallel", "arbitrary")),
    )(sink, q, k, vt)


def _outproj_kernel(x_ref, oa_ref, ob_ref, oc_ref, od_ref, wt_ref, g_ref, b_ref, o_ref, *, alpha):
    mix_t = None
    for m, ref in enumerate((oa_ref, ob_ref, oc_ref, od_ref)):
        part = jnp.dot(wt_ref[:, m * 256:(m + 1) * 256], ref[0], preferred_element_type=F32)
        mix_t = part if mix_t is None else mix_t + part
    y = alpha * x_ref[0] + mix_t.T
    o_ref[0] = _layer_norm_rows(y, g_ref[...], b_ref[...])


def _outproj(x, o_a, o_b, o_c, o_d, w_out_t, g, b, alpha):
    bsz, seq, d = x.shape
    tm = min(TOK_TILE, seq)
    ot = pl.BlockSpec((1, 256, tm), lambda bb, i: (bb, 0, i))
    tok = pl.BlockSpec((1, tm, d), lambda bb, i: (bb, i, 0))
    return pl.pallas_call(
        functools.partial(_outproj_kernel, alpha=alpha),
        grid=(bsz, seq // tm),
        in_specs=[tok, ot, ot, ot, ot, _resident(w_out_t.shape), _resident(g.shape), _resident(b.shape)],
        out_specs=tok,
        out_shape=jax.ShapeDtypeStruct((bsz, seq, d), F32),
        compiler_params=_cparams(("parallel", "parallel")),
    )(x, o_a, o_b, o_c, o_d, w_out_t, g, b)


def _rot_half_cols(start, width):
    half = width // 2
    src = np.concatenate([np.arange(start + half, start + width), np.arange(start, start + half)])
    sgn = np.concatenate([-np.ones(half), np.ones(half)])
    return src, sgn


def _gqa_slab_order(n_heads, n_kv):
    group = n_heads // n_kv
    return [kv * group + s for s in range(group) for kv in range(n_kv)]


def _layer_params(w_in, mla_q_norm, mla_w_uq, mla_kv_norm, mla_w_ukv, ax_q_norm, ax_k_norm):
    d = w_in.shape[0]
    zeros = lambda n: jnp.zeros((d, n), w_in.dtype)
    a0, b0, c0, d0 = 0, A_COLS, A_COLS + B_COLS, A_COLS + B_COLS + C_COLS
    head_cols = lambda base, h: np.arange(base + h * HEAD_DIM, base + (h + 1) * HEAD_DIM)

    def axial_rot(cols):
        src, sgn = [], []
        for blk in range(0, HEAD_DIM, HEAD_DIM // 2):
            s_, g_ = _rot_half_cols(blk, HEAD_DIM // 2)
            src.append(cols[s_])
            sgn.append(g_)
        return np.concatenate(src), np.concatenate(sgn)

    order = _gqa_slab_order(AX_HEADS, AX_KV_HEADS)
    aq = np.concatenate([head_cols(a0, h) for h in order])
    ak = np.arange(a0 + 256, a0 + 384)
    cq = np.concatenate([head_cols(c0, h) for h in order])
    ck = np.arange(c0 + 256, c0 + 384)
    cq_rot = [axial_rot(head_cols(c0, h)) for h in order]
    ck_rot = [axial_rot(head_cols(c0 + 256, h)) for h in range(AX_KV_HEADS)]

    def gather(src, sgn=None):
        w = w_in[:, np.asarray(src)]
        return w if sgn is None else w * jnp.asarray(sgn, w.dtype)[None, :]

    def widen(cols):
        return jnp.concatenate([gather(cols), zeros(LANES - len(cols))], axis=1)

    kr = np.arange(b0 + MLA_Q_RANK + MLA_KV_RANK, b0 + B_COLS)
    kr_src, kr_sgn = _rot_half_cols(kr[0], MLA_ROPE)
    place_rope = lambda w: jnp.concatenate([zeros(MLA_NOPE), w, zeros(LANES - MLA_NOPE - MLA_ROPE)], axis=1)

    w_tok = jnp.concatenate(
        [gather(aq), gather(ak), gather(cq), gather(ck)]
        + [gather(s, g) for s, g in cq_rot] + [gather(s, g) for s, g in ck_rot]
        + [widen(head_cols(d0, h)) for h in range(DIFF_HEADS)]
        + [widen(head_cols(d0 + 256, h)) for h in range(DIFF_HEADS)]
        + [gather(np.arange(b0, b0 + MLA_Q_RANK + MLA_KV_RANK)),
           place_rope(gather(kr)), place_rope(gather(kr_src, kr_sgn))], axis=1)
    assert w_tok.shape[1] == _C_END
    v_cols = np.concatenate([np.arange(a0 + 384, a0 + 512), np.arange(c0 + 384, c0 + 512),
                             np.arange(d0 + 512, d0 + 768)])
    w_vt = w_in[:, v_cols].T

    qd = MLA_NOPE + MLA_ROPE
    zq = lambda n: jnp.zeros((MLA_Q_RANK, n), mla_w_uq.dtype)
    uq, uq_rot = [], []
    for h in range(MLA_HEADS):
        blk = mla_w_uq[:, h * qd:(h + 1) * qd]
        src, sgn = _rot_half_cols(MLA_NOPE, MLA_ROPE)
        rot = blk[:, src] * jnp.asarray(sgn, blk.dtype)[None, :]
        uq += [blk, zq(LANES - qd)]
        uq_rot += [zq(MLA_NOPE), rot, zq(LANES - qd)]
    ukv = mla_w_ukv.reshape(MLA_KV_RANK, MLA_HEADS, MLA_NOPE + MLA_V)
    ukv_k = jnp.concatenate([ukv[:, :, :MLA_NOPE], jnp.zeros((MLA_KV_RANK, MLA_HEADS, LANES - MLA_NOPE), ukv.dtype)],
                            axis=2).reshape(MLA_KV_RANK, MLA_HEADS * LANES)
    ukv_vt = ukv[:, :, MLA_NOPE:].reshape(MLA_KV_RANK, MLA_HEADS * MLA_V).T

    def ax_gain(g):
        src, sgn = axial_rot(np.arange(HEAD_DIM))
        return jnp.tile(g, 2)[None, :].astype(F32), jnp.tile(g[src], 2)[None, :].astype(F32)

    ax_q, ax_q_rot = ax_gain(ax_q_norm)
    ax_k, ax_k_rot = ax_gain(ax_k_norm)
    lane = np.arange(LANES)
    g_avg = (lane[:, None] // HEAD_DIM == lane[None, :] // HEAD_DIM).astype(np.float32) / HEAD_DIM
    return {
        "w_tok": w_tok.astype(BF16), "w_vt": w_vt.astype(BF16),
        "w_uq": jnp.concatenate(uq, axis=1).astype(BF16), "w_uq_rot": jnp.concatenate(uq_rot, axis=1).astype(BF16),
        "w_ukv_k": ukv_k.astype(BF16), "w_ukv_vt": ukv_vt.astype(BF16),
        "g_avg": jnp.asarray(g_avg, BF16),
        "g_q": mla_q_norm[None, :].astype(F32), "g_kv": mla_kv_norm[None, :].astype(F32),
        "ax_q": ax_q, "ax_q_rot": ax_q_rot, "ax_k": ax_k, "ax_k_rot": ax_k_rot,
    }


def _position_tables(seq):
    pos = jnp.arange(seq, dtype=jnp.int32)
    half = HEAD_DIM // 2

    def angles(p, dim):
        inv = ROPE_BASE ** (-jnp.arange(0, dim, 2, dtype=F32) / dim)
        ang = p.astype(F32)[:, None] * inv[None, :]
        return jnp.cos(ang), jnp.sin(ang)

    cb, sb = angles(pos, MLA_ROPE)
    pad = LANES - MLA_NOPE - MLA_ROPE
    cos_b = jnp.concatenate([jnp.ones((seq, MLA_NOPE), F32), cb, cb, jnp.zeros((seq, pad), F32)], axis=1)
    sin_b = jnp.concatenate([jnp.zeros((seq, MLA_NOPE), F32), sb, sb, jnp.zeros((seq, pad), F32)], axis=1)
    cr, sr = angles(pos // GRID_W, half)
    cc, sc = angles(pos % GRID_W, half)
    cos_c = jnp.tile(jnp.concatenate([cr, cr, cc, cc], axis=1), (1, 2))
    sin_c = jnp.tile(jnp.concatenate([sr, sr, sc, sc], axis=1), (1, 2))
    aug = jnp.zeros((seq, LANES), F32)
    aug = aug.at[:, _AUG].set(1.0).at[:, _AUG + 1].set(1.0)
    aug = aug.at[:, _AUG + 2].set(((pos >> 7) << 7).astype(F32)).at[:, _AUG + 3].set((pos & 127).astype(F32))
    return {"cos_b": cos_b, "sin_b": sin_b, "cos_c": cos_c, "sin_c": sin_c, "aug_k": aug}


def _ffn_weights(w_gu, w_down):
    d, two_ff = w_gu.shape
    d_ff = two_ff // 2
    n_chunks = d_ff // FF_CHUNK
    wgu = w_gu.astype(BF16).reshape(d, 2, n_chunks, FF_CHUNK).transpose(2, 0, 1, 3).reshape(n_chunks, d, 2 * FF_CHUNK)
    wd = w_down.astype(BF16).reshape(n_chunks, FF_CHUNK, d)
    return wgu, wd


def kernel(x, w_in, win_sink, mla_q_norm, mla_w_uq, mla_kv_norm, mla_w_ukv, ax_q_norm, ax_k_norm,
           diff_lambda, diff_subln, w_out, ffn_w_gu, ffn_w_down, ln_g, ln_b):
    depth = w_in.shape[0]
    bsz, seq, d = x.shape
    alpha = (2 * depth) ** 0.25
    tabs = _position_tables(seq)
    row = lambda v: v[None, :].astype(F32)
    for l in range(depth):
        wgu, wd = _ffn_weights(ffn_w_gu[l, 0], ffn_w_down[l, 0])
        x = _ffn(x.reshape(bsz * seq, d), wgu, wd, row(ln_g[l, 0]), row(ln_b[l, 0]), alpha).reshape(bsz, seq, d)

        p = _layer_params(w_in[l], mla_q_norm[l], mla_w_uq[l], mla_kv_norm[l], mla_w_ukv[l],
                          ax_q_norm[l], ax_k_norm[l])
        qa, ka, vta, qb, kb, vtb, qc, kc, vtc, qd, kd, vtd = _prep(x, p, tabs)
        o_a = _window_attention(qa, ka, vta, win_sink[l].astype(F32))
        o_b = _dense_attention(qb, kb, vtb, n_heads=MLA_HEADS, kv_group=0)
        o_c = _dense_attention(qc, kc, vtc, n_heads=AX_HEADS, kv_group=AX_HEADS // AX_KV_HEADS)
        o_d = _diff_attention(qd, kd, vtd, diff_lambda[l].astype(F32), diff_subln[l][:, None].astype(F32),
                              layer_idx=l, slope_exp0=WIN_HEADS + 1)
        x = _outproj(x, o_a, o_b, o_c, o_d, w_out[l].T.astype(BF16), row(ln_g[l, 1]), row(ln_b[l, 1]), alpha)

        wgu, wd = _ffn_weights(ffn_w_gu[l, 1], ffn_w_down[l, 1])
        x = _ffn(x.reshape(bsz * seq, d), wgu, wd, row(ln_g[l, 2]), row(ln_b[l, 2]), alpha).reshape(bsz, seq, d)
    return x
```

```python
import functools
import math

import numpy as np
import jax
import jax.numpy as jnp
from jax import lax
from jax.experimental import pallas as pl
from jax.experimental.pallas import tpu as pltpu

F32 = jnp.float32
BF16 = jnp.bfloat16

HEAD_DIM = 64
HEAD_SHIFT = 6
GRID_W = 64
WIN_HEADS, WIN_KV_HEADS, WINDOW = 4, 2, 128
MLA_HEADS, MLA_Q_RANK, MLA_KV_RANK, MLA_NOPE, MLA_ROPE, MLA_V = 4, 256, 128, 64, 32, 64
AX_HEADS, AX_KV_HEADS = 4, 2
DIFF_HEADS, DIFF_QK, DIFF_V = 4, 32, 64
A_COLS = (WIN_HEADS + 2 * WIN_KV_HEADS) * HEAD_DIM
B_COLS = MLA_Q_RANK + MLA_KV_RANK + MLA_ROPE
C_COLS = (AX_HEADS + 2 * AX_KV_HEADS) * HEAD_DIM
D_COLS = DIFF_HEADS * (4 * DIFF_QK + DIFF_V)
ROPE_BASE = 10000.0
NORM_EPS = 1e-5
NEG_INF = -1e30

LANES = 128
BF16_SUBLANES = 16
V_ROWS = HEAD_DIM + BF16_SUBLANES
VMEM_LIMIT = 56 * 1024 * 1024

TOK_TILE = 512
KV_CHUNK = 512
Q_TILE = 512
FF_CHUNK = 256
HEADS_PER_STEP = 4
CHUNKS_PER_STEP = 2


def _cparams(sem):
    return pltpu.CompilerParams(dimension_semantics=sem, vmem_limit_bytes=VMEM_LIMIT)


def _resident(shape):
    nd = len(shape)
    return pl.BlockSpec(shape, lambda *_: (0,) * nd, pipeline_mode=pl.Buffered(1))


def _layer_norm_rows(y, g, b):
    mu = jnp.mean(y, axis=-1, keepdims=True)
    yc = y - mu
    var = jnp.mean(yc * yc, axis=-1, keepdims=True)
    return yc * lax.rsqrt(var + NORM_EPS) * g + b


def _ffn_kernel(x_ref, wgu_ref, wd_ref, g_ref, b_ref, o_ref, acc_ref, *, alpha, n_chunks):
    x = x_ref[...]
    xb = x.astype(BF16)
    acc_ref[...] = jnp.zeros_like(acc_ref)

    def body(c, carry):
        gu = jnp.dot(xb, wgu_ref[c], preferred_element_type=F32)
        gate = gu[:, :FF_CHUNK]
        up = gu[:, FF_CHUNK:]
        h = (gate * jax.nn.sigmoid(gate) * up).astype(BF16)
        acc_ref[...] += jnp.dot(h, wd_ref[c], preferred_element_type=F32)
        return carry

    lax.fori_loop(0, n_chunks, body, 0)
    y = alpha * x + 0.5 * acc_ref[...]
    o_ref[...] = _layer_norm_rows(y, g_ref[...], b_ref[...])


def _ffn(x2d, wgu, wd, g, b, alpha):
    n_tok, d = x2d.shape
    n_chunks = wgu.shape[0]
    tm = min(TOK_TILE, n_tok)
    return pl.pallas_call(
        functools.partial(_ffn_kernel, alpha=alpha, n_chunks=n_chunks),
        grid=(n_tok // tm,),
        in_specs=[
            pl.BlockSpec((tm, d), lambda i: (i, 0)),
            _resident(wgu.shape),
            _resident(wd.shape),
            _resident(g.shape),
            _resident(b.shape),
        ],
        out_specs=pl.BlockSpec((tm, d), lambda i: (i, 0)),
        out_shape=jax.ShapeDtypeStruct((n_tok, d), F32),
        scratch_shapes=[pltpu.VMEM((tm, d), F32)],
        compiler_params=_cparams(("parallel",)),
    )(x2d, wgu, wd, g, b)


_C_AQ, _C_AK, _C_CQ, _C_CK, _C_CQR, _C_CKR = 0, 256, 384, 640, 768, 1024
_C_DQ, _C_DK, _C_BQ, _C_BKV, _C_BKR, _C_BKRR, _C_END = 1152, 1664, 2176, 2432, 2560, 2688, 2816
_R_AV, _R_CV, _R_DV, _R_END = 0, 128, 256, 512


def _split_hi_lo(v):
    hi = v.astype(BF16)
    lo = (v - hi.astype(F32)).astype(BF16)
    return hi, lo


def _prep_kernel(x_ref, wtok_ref, wvt_ref, wuq_ref, wuqr_ref, wukvk_ref, wukvvt_ref, gavg_ref,
                 gq_ref, gkv_ref, axq_ref, axqr_ref, axk_ref, axkr_ref,
                 cosb_ref, sinb_ref, cosc_ref, sinc_ref, augk_ref,
                 qa_ref, ka_ref, vta_ref, qb_ref, kb_ref, vtb_ref,
                 qc_ref, kc_ref, vtc_ref, qd_ref, kd_ref, vtd_ref):
    xb = x_ref[0].astype(BF16)
    tm = xb.shape[0]
    pm = jnp.dot(xb, wtok_ref[...], preferred_element_type=F32)
    nt = (((1,), (1,)), ((), ()))
    pvt = lax.dot_general(wvt_ref[...], xb, nt, preferred_element_type=F32)

    row = lax.broadcasted_iota(jnp.int32, (BF16_SUBLANES, tm), 0)
    ones_tile = jnp.where(row == 0, 1.0, 0.0).astype(BF16)

    def put_vt(ref, head, vt):
        ref[0, head, 0, 0:HEAD_DIM, :] = vt.astype(BF16)
        ref[0, head, 0, HEAD_DIM:V_ROWS, :] = ones_tile

    qa_ref[0] = (pm[:, _C_AQ:_C_AQ + 256] * (HEAD_DIM ** -0.5)).astype(BF16)
    ka_ref[0] = pm[:, _C_AK:_C_AK + 128].astype(BF16)
    for h in range(WIN_KV_HEADS):
        put_vt(vta_ref, h, pvt[_R_AV + 64 * h:_R_AV + 64 * (h + 1)])

    cosc = cosc_ref[...]
    sinc = sinc_ref[...]
    gavg = gavg_ref[...]

    def head_rms_scale(v):
        hi, lo = _split_hi_lo(v * v)
        ms = (jnp.dot(hi, gavg, preferred_element_type=F32)
              + jnp.dot(lo, gavg, preferred_element_type=F32))
        return lax.rsqrt(ms + NORM_EPS)

    def axial(off, off_rot, g_ref_, gr_ref_, scale):
        v = pm[:, off:off + LANES]
        r = head_rms_scale(v)
        vn = v * r * g_ref_[...]
        vrn = pm[:, off_rot:off_rot + LANES] * r * gr_ref_[...]
        return ((vn * cosc + vrn * sinc) * scale).astype(BF16)

    for grp in range(2):
        qc_ref[0, :, grp * LANES:(grp + 1) * LANES] = axial(
            _C_CQ + grp * LANES, _C_CQR + grp * LANES, axq_ref, axqr_ref, HEAD_DIM ** -0.5 * LOG2E)
    kc_ref[0] = axial(_C_CK, _C_CKR, axk_ref, axkr_ref, 1.0)
    for h in range(AX_KV_HEADS):
        put_vt(vtc_ref, h, pvt[_R_CV + 64 * h:_R_CV + 64 * (h + 1)])

    qd_ref[0] = (pm[:, _C_DQ:_C_DQ + 512] * (DIFF_QK ** -0.5)).astype(BF16)
    augk = augk_ref[...]
    for h in range(DIFF_HEADS):
        kd_ref[0, :, h * LANES:(h + 1) * LANES] = (
            pm[:, _C_DK + h * LANES:_C_DK + (h + 1) * LANES] + augk).astype(BF16)
        put_vt(vtd_ref, h, pvt[_R_DV + 64 * h:_R_DV + 64 * (h + 1)])

    cosb = cosb_ref[...]
    sinb = sinb_ref[...]
    cq = pm[:, _C_BQ:_C_BQ + MLA_Q_RANK]
    cqn = (cq * lax.rsqrt(jnp.mean(cq * cq, axis=-1, keepdims=True) + NORM_EPS) * gq_ref[...]).astype(BF16)
    qw = jnp.dot(cqn, wuq_ref[...], preferred_element_type=F32)
    qwr = jnp.dot(cqn, wuqr_ref[...], preferred_element_type=F32)
    ckv = pm[:, _C_BKV:_C_BKV + MLA_KV_RANK]
    ckvn = (ckv * lax.rsqrt(jnp.mean(ckv * ckv, axis=-1, keepdims=True) + NORM_EPS) * gkv_ref[...]).astype(BF16)
    kw = jnp.dot(ckvn, wukvk_ref[...], preferred_element_type=F32)
    vbt = lax.dot_general(wukvvt_ref[...], ckvn, nt, preferred_element_type=F32)
    k_rope = pm[:, _C_BKR:_C_BKR + LANES] * cosb + pm[:, _C_BKRR:_C_BKRR + LANES] * sinb
    q_scale = (MLA_NOPE + MLA_ROPE) ** -0.5 * LOG2E
    for h in range(MLA_HEADS):
        sl = slice(h * LANES, (h + 1) * LANES)
        qb_ref[0, :, sl] = ((qw[:, sl] * cosb + qwr[:, sl] * sinb) * q_scale).astype(BF16)
        kb_ref[0, :, sl] = (kw[:, sl] + k_rope).astype(BF16)
        put_vt(vtb_ref, h, vbt[64 * h:64 * (h + 1)])


def _prep(x, p, tabs):
    bsz, seq, d = x.shape
    tm = KV_CHUNK
    nch = seq // tm
    tok = lambda w: pl.BlockSpec((1, tm, w), lambda b, i: (b, i, 0))
    tab = pl.BlockSpec((tm, LANES), lambda b, i: (i, 0))
    vt = lambda nh: pl.BlockSpec((1, nh, 1, V_ROWS, tm), lambda b, i: (b, 0, i, 0, 0))
    vt_shape = lambda nh: jax.ShapeDtypeStruct((bsz, nh, nch, V_ROWS, tm), BF16)
    tok_shape = lambda w: jax.ShapeDtypeStruct((bsz, seq, w), BF16)
    weights = [p["w_tok"], p["w_vt"], p["w_uq"], p["w_uq_rot"], p["w_ukv_k"], p["w_ukv_vt"], p["g_avg"],
               p["g_q"], p["g_kv"], p["ax_q"], p["ax_q_rot"], p["ax_k"], p["ax_k_rot"]]
    return pl.pallas_call(
        _prep_kernel,
        grid=(bsz, nch),
        in_specs=[tok(d)] + [_resident(w.shape) for w in weights] + [tab] * 5,
        out_specs=[tok(256), tok(128), vt(2), tok(512), tok(512), vt(4),
                   tok(256), tok(128), vt(2), tok(512), tok(512), vt(4)],
        out_shape=[tok_shape(256), tok_shape(128), vt_shape(2), tok_shape(512), tok_shape(512), vt_shape(4),
                   tok_shape(256), tok_shape(128), vt_shape(2), tok_shape(512), tok_shape(512), vt_shape(4)],
        compiler_params=_cparams(("parallel", "parallel")),
    )(x, *weights, tabs["cos_b"], tabs["sin_b"], tabs["cos_c"], tabs["sin_c"], tabs["aug_k"])


_NT = (((1,), (1,)), ((), ()))
LOG2E = math.log2(math.e)


def _key_rows(k_ref, j, slab, cps):
    rows = pl.ds(pl.multiple_of(j * (cps * KV_CHUNK), cps * KV_CHUNK), cps * KV_CHUNK)
    return k_ref[0, rows, slab * LANES:(slab + 1) * LANES]


def _chain_body(k_ref, vt_ref, chains, exp_fn, transform=None, cps=1):
    def body(j, carry):
        scores = []
        for n, (q, k_slab, _) in enumerate(chains):
            s = lax.dot_general(_key_rows(k_ref, j, k_slab, cps), q, _NT, preferred_element_type=F32)
            scores.append(s if transform is None else transform(j, s, n))
        probs = []
        for s, (m, _) in zip(scores, carry):
            m_new = jnp.maximum(m, jnp.max(s, axis=0, keepdims=True))
            probs.append((m_new, exp_fn(m - m_new), exp_fn(s - m_new).astype(BF16)))
        out = []
        for (_, _, v_head), (m_new, alpha, p), (_, acc) in zip(chains, probs, carry):
            pv = None
            for c in range(cps):
                part = jnp.dot(vt_ref[0, v_head, j * cps + c], p[c * KV_CHUNK:(c + 1) * KV_CHUNK],
                               preferred_element_type=F32)
                pv = part if pv is None else pv + part
            out.append((m_new, acc * alpha + pv))
        return tuple(out)
    return body


def _normalised(acc):
    return acc[:HEAD_DIM] / acc[HEAD_DIM:HEAD_DIM + 1]


def _single_buffered(block, index_map):
    return pl.BlockSpec(block, index_map, pipeline_mode=pl.Buffered(1))


def _dense_kernel(q_ref, k_ref, vt_ref, o_ref, *, n_chunks, heads):
    tq = q_ref.shape[1]
    lane = lax.broadcasted_iota(jnp.int32, (tq, LANES), 1)
    chains = []
    for q_slab, k_slab, v_head, lane_half in heads:
        q = q_ref[0, :, q_slab * LANES:(q_slab + 1) * LANES]
        if lane_half is not None:
            q = jnp.where((lane >> HEAD_SHIFT) == lane_half, q.astype(F32), 0.0).astype(BF16)
        chains.append((q, k_slab, v_head))
    init = tuple((jnp.full((1, tq), NEG_INF, F32), jnp.zeros((V_ROWS, tq), F32)) for _ in heads)
    cps = math.gcd(CHUNKS_PER_STEP, n_chunks)
    final = lax.fori_loop(0, n_chunks // cps, _chain_body(k_ref, vt_ref, chains, jnp.exp2, cps=cps), init)
    for n, (_, acc) in enumerate(final):
        o_ref[0, n * HEAD_DIM:(n + 1) * HEAD_DIM, :] = _normalised(acc).astype(BF16)


def _dense_attention(q, k, vt, *, n_heads, kv_group, heads_per_step):
    bsz, seq, _ = q.shape
    nch = vt.shape[2]
    tq = min(Q_TILE, seq)
    hp = heads_per_step
    if kv_group:
        assert hp == n_heads
        n_kv = n_heads // kv_group
        heads = tuple((h % kv_group, 0, h // kv_group, h // kv_group) for h in range(n_heads))
        q_spec = pl.BlockSpec((1, tq, kv_group * LANES), lambda b, g, i: (b, i, 0))
        k_spec = _single_buffered((1, seq, LANES), lambda b, g, i: (b, 0, 0))
        vt_spec = _single_buffered((1, n_kv, nch, V_ROWS, KV_CHUNK), lambda b, g, i: (b, 0, 0, 0, 0))
    else:
        heads = tuple((n, n, n, None) for n in range(hp))
        q_spec = pl.BlockSpec((1, tq, hp * LANES), lambda b, g, i: (b, i, g))
        k_spec = _single_buffered((1, seq, hp * LANES), lambda b, g, i: (b, 0, g))
        vt_spec = _single_buffered((1, hp, nch, V_ROWS, KV_CHUNK), lambda b, g, i: (b, g, 0, 0, 0))
    return pl.pallas_call(
        functools.partial(_dense_kernel, n_chunks=nch, heads=heads),
        grid=(bsz, n_heads // hp, seq // tq),
        in_specs=[q_spec, k_spec, vt_spec],
        out_specs=pl.BlockSpec((1, hp * HEAD_DIM, tq), lambda b, g, i: (b, g, i)),
        out_shape=jax.ShapeDtypeStruct((bsz, n_heads * HEAD_DIM, seq), BF16),
        compiler_params=_cparams(("parallel", "parallel", "arbitrary")),
    )(q, k, vt)


_AUG = 2 * DIFF_QK


def _diff_kernel(lam_ref, g_ref, q_ref, k_ref, vt_ref, o_ref, *, n_chunks, hp, lam_init, slope_exp0):
    i = pl.program_id(2)
    tq = q_ref.shape[1]
    lane = lax.broadcasted_iota(jnp.int32, (tq, LANES), 1)
    qpos = i * tq + lax.broadcasted_iota(jnp.int32, (tq, LANES), 0)
    q_hi = ((qpos >> 7) << 7).astype(F32)
    q_lo = (qpos & 127).astype(F32)
    zero = jnp.zeros((tq, LANES), F32)

    slopes, before, after, plain = [], [], [], []
    for n in range(hp):
        head = pl.program_id(1) * hp + n
        slope_bits = jnp.full((1, LANES), 127 - slope_exp0, jnp.int32) - head
        slope = lax.bitcast_convert_type(slope_bits << 23, F32)
        qf = q_ref[0, :, n * LANES:(n + 1) * LANES].astype(F32)
        aug = jnp.where(lane == _AUG, -slope * q_hi,
                        jnp.where(lane == _AUG + 1, -slope * q_lo,
                                  jnp.where((lane == _AUG + 2) | (lane == _AUG + 3), slope, 0.0)))
        q1 = jnp.where(lane < DIFF_QK, qf, 0.0)
        q2 = jnp.where((lane >= DIFF_QK) & (lane < 2 * DIFF_QK), qf, 0.0)
        both = lambda extra: jnp.concatenate([q1 + extra, q2 + extra], axis=0).astype(BF16)
        slopes.append(slope[:, :1])
        before.append((both(aug), n, n))
        after.append((both(-aug), n, n))
        plain.append((both(zero), n, n))

    carry = tuple((jnp.full((1, 2 * tq), NEG_INF, F32), jnp.zeros((V_ROWS, 2 * tq), F32)) for _ in range(hp))
    jd = (i * tq) // KV_CHUNK
    carry = lax.fori_loop(0, jd, _chain_body(k_ref, vt_ref, before, jnp.exp), carry)
    kpos = jd * KV_CHUNK + lax.broadcasted_iota(jnp.int32, (KV_CHUNK, tq), 0)
    tpos = i * tq + lax.broadcasted_iota(jnp.int32, (KV_CHUNK, tq), 1)
    dist = jnp.abs(tpos - kpos).astype(F32)

    def diagonal_bias(j, s, n):
        bias = -slopes[n] * dist
        return s + jnp.concatenate([bias, bias], axis=1)

    carry = _chain_body(k_ref, vt_ref, plain, jnp.exp, diagonal_bias)(jd, carry)
    final = lax.fori_loop(jd + 1, n_chunks, _chain_body(k_ref, vt_ref, after, jnp.exp), carry)

    lp = lam_ref[...]
    lam = (jnp.exp(jnp.sum(lp[0:1] * lp[1:2], axis=-1, keepdims=True))
           - jnp.exp(jnp.sum(lp[2:3] * lp[3:4], axis=-1, keepdims=True)) + lam_init)
    for n, (_, acc) in enumerate(final):
        o = _normalised(acc[:, :tq]) - lam * _normalised(acc[:, tq:])
        o = o * lax.rsqrt(jnp.mean(o * o, axis=0, keepdims=True) + NORM_EPS)
        o_ref[0, n * DIFF_V:(n + 1) * DIFF_V, :] = (o * g_ref[...] * (1.0 - lam_init)).astype(BF16)


def _diff_attention(q, k, vt, lam_params, subln_col, *, layer_idx, slope_exp0, heads_per_step):
    bsz, seq, _ = q.shape
    nch = vt.shape[2]
    tq = min(Q_TILE, seq)
    hp = heads_per_step
    assert KV_CHUNK % tq == 0 and seq <= 256 * LANES and DIFF_HEADS % hp == 0
    lam_init = 0.8 - 0.6 * math.exp(-0.3 * layer_idx)
    return pl.pallas_call(
        functools.partial(_diff_kernel, n_chunks=nch, hp=hp, lam_init=lam_init, slope_exp0=slope_exp0),
        grid=(bsz, DIFF_HEADS // hp, seq // tq),
        in_specs=[
            _resident(lam_params.shape),
            _resident(subln_col.shape),
            pl.BlockSpec((1, tq, hp * LANES), lambda b, g, i: (b, i, g)),
            _single_buffered((1, seq, hp * LANES), lambda b, g, i: (b, 0, g)),
            _single_buffered((1, hp, nch, V_ROWS, KV_CHUNK), lambda b, g, i: (b, g, 0, 0, 0)),
        ],
        out_specs=pl.BlockSpec((1, hp * DIFF_V, tq), lambda b, g, i: (b, g, i)),
        out_shape=jax.ShapeDtypeStruct((bsz, DIFF_HEADS * DIFF_V, seq), BF16),
        compiler_params=_cparams(("parallel", "parallel", "arbitrary")),
    )(lam_params, subln_col, q, k, vt)


def _window_kernel(sink_ref, q_ref, k_ref, vt_ref, o_ref, *, seq):
    i = pl.program_id(1)
    tq = q_ref.shape[1]
    q0 = i * tq
    j_lo = jnp.maximum(q0 - WINDOW, 0) // KV_CHUNK
    j_hi = jnp.minimum(q0 + tq - 1 + WINDOW, seq - 1) // KV_CHUNK
    lane = lax.broadcasted_iota(jnp.int32, (tq, LANES), 1)
    acc_row = lax.broadcasted_iota(jnp.int32, (V_ROWS, tq), 0)
    tpos = q0 + lax.broadcasted_iota(jnp.int32, (KV_CHUNK, tq), 1)
    krow = lax.broadcasted_iota(jnp.int32, (KV_CHUNK, tq), 0)
    group = WIN_HEADS // WIN_KV_HEADS
    chains, init = [], []
    for head in range(WIN_HEADS):
        kv_head, slab = head // group, head % group
        qf = q_ref[0, :, slab * LANES:(slab + 1) * LANES].astype(F32)
        chains.append((jnp.where((lane >> HEAD_SHIFT) == kv_head, qf, 0.0).astype(BF16), 0, kv_head))
        init.append((jnp.full((1, tq), sink_ref[head], F32), jnp.where(acc_row == HEAD_DIM, 1.0, 0.0)))

    def band(j, s, n):
        dist = jnp.abs(tpos - (j * KV_CHUNK + krow))
        return jnp.where(dist <= WINDOW, s - (2.0 ** -(n + 1)) * dist.astype(F32), NEG_INF)

    final = lax.fori_loop(j_lo, j_hi + 1, _chain_body(k_ref, vt_ref, chains, jnp.exp, band), tuple(init))
    for n, (_, acc) in enumerate(final):
        o_ref[0, n * HEAD_DIM:(n + 1) * HEAD_DIM, :] = _normalised(acc).astype(BF16)


def _window_attention(q, k, vt, sink):
    bsz, seq, _ = q.shape
    nch = vt.shape[2]
    tq = min(Q_TILE, seq)
    return pl.pallas_call(
        functools.partial(_window_kernel, seq=seq),
        grid=(bsz, seq // tq),
        in_specs=[
            pl.BlockSpec(memory_space=pltpu.SMEM),
            pl.BlockSpec((1, tq, 2 * LANES), lambda b, i: (b, i, 0)),
            _single_buffered((1, seq, LANES), lambda b, i: (b, 0, 0)),
            _single_buffered((1, WIN_KV_HEADS, nch, V_ROWS, KV_CHUNK), lambda b, i: (b, 0, 0, 0, 0)),
        ],
        out_specs=pl.BlockSpec((1, WIN_HEADS * HEAD_DIM, tq), lambda b, i: (b, 0, i)),
        out_shape=jax.ShapeDtypeStruct((bsz, WIN_HEADS * HEAD_DIM, seq), BF16),
        compiler_params=_cparams(("parallel", "arbitrary")),
    )(sink, q, k, vt)


def _outproj_kernel(x_ref, oa_ref, ob_ref, oc_ref, od_ref, wt_ref, g_ref, b_ref, o_ref, *, alpha):
    mix_t = None
    for m, ref in enumerate((oa_ref, ob_ref, oc_ref, od_ref)):
        part = jnp.dot(wt_ref[:, m * 256:(m + 1) * 256], ref[0], preferred_element_type=F32)
        mix_t = part if mix_t is None else mix_t + part
    y = alpha * x_ref[0] + mix_t.T
    o_ref[0] = _layer_norm_rows(y, g_ref[...], b_ref[...])


def _outproj(x, o_a, o_b, o_c, o_d, w_out_t, g, b, alpha):
    bsz, seq, d = x.shape
    tm = min(TOK_TILE, seq)
    ot = pl.BlockSpec((1, 256, tm), lambda bb, i: (bb, 0, i))
    tok = pl.BlockSpec((1, tm, d), lambda bb, i: (bb, i, 0))
    return pl.pallas_call(
        functools.partial(_outproj_kernel, alpha=alpha),
        grid=(bsz, seq // tm),
        in_specs=[tok, ot, ot, ot, ot, _resident(w_out_t.shape), _resident(g.shape), _resident(b.shape)],
        out_specs=tok,
        out_shape=jax.ShapeDtypeStruct((bsz, seq, d), F32),
        compiler_params=_cparams(("parallel", "parallel")),
    )(x, o_a, o_b, o_c, o_d, w_out_t, g, b)


def _rot_half_cols(start, width):
    half = width // 2
    src = np.concatenate([np.arange(start + half, start + width), np.arange(start, start + half)])
    sgn = np.concatenate([-np.ones(half), np.ones(half)])
    return src, sgn


def _gqa_slab_order(n_heads, n_kv):
    group = n_heads // n_kv
    return [kv * group + s for s in range(group) for kv in range(n_kv)]


def _layer_params(w_in, mla_q_norm, mla_w_uq, mla_kv_norm, mla_w_ukv, ax_q_norm, ax_k_norm):
    d = w_in.shape[0]
    zeros = lambda n: jnp.zeros((d, n), w_in.dtype)
    a0, b0, c0, d0 = 0, A_COLS, A_COLS + B_COLS, A_COLS + B_COLS + C_COLS
    head_cols = lambda base, h: np.arange(base + h * HEAD_DIM, base + (h + 1) * HEAD_DIM)

    def axial_rot(cols):
        src, sgn = [], []
        for blk in range(0, HEAD_DIM, HEAD_DIM // 2):
            s_, g_ = _rot_half_cols(blk, HEAD_DIM // 2)
            src.append(cols[s_])
            sgn.append(g_)
        return np.concatenate(src), np.concatenate(sgn)

    order = _gqa_slab_order(AX_HEADS, AX_KV_HEADS)
    aq = np.concatenate([head_cols(a0, h) for h in order])
    ak = np.arange(a0 + 256, a0 + 384)
    cq = np.concatenate([head_cols(c0, h) for h in order])
    ck = np.arange(c0 + 256, c0 + 384)
    cq_rot = [axial_rot(head_cols(c0, h)) for h in order]
    ck_rot = [axial_rot(head_cols(c0 + 256, h)) for h in range(AX_KV_HEADS)]

    def gather(src, sgn=None):
        w = w_in[:, np.asarray(src)]
        return w if sgn is None else w * jnp.asarray(sgn, w.dtype)[None, :]

    def widen(cols):
        return jnp.concatenate([gather(cols), zeros(LANES - len(cols))], axis=1)

    kr = np.arange(b0 + MLA_Q_RANK + MLA_KV_RANK, b0 + B_COLS)
    kr_src, kr_sgn = _rot_half_cols(kr[0], MLA_ROPE)
    place_rope = lambda w: jnp.concatenate([zeros(MLA_NOPE), w, zeros(LANES - MLA_NOPE - MLA_ROPE)], axis=1)

    w_tok = jnp.concatenate(
        [gather(aq), gather(ak), gather(cq), gather(ck)]
        + [gather(s, g) for s, g in cq_rot] + [gather(s, g) for s, g in ck_rot]
        + [widen(head_cols(d0, h)) for h in range(DIFF_HEADS)]
        + [widen(head_cols(d0 + 256, h)) for h in range(DIFF_HEADS)]
        + [gather(np.arange(b0, b0 + MLA_Q_RANK + MLA_KV_RANK)),
           place_rope(gather(kr)), place_rope(gather(kr_src, kr_sgn))], axis=1)
    assert w_tok.shape[1] == _C_END
    v_cols = np.concatenate([np.arange(a0 + 384, a0 + 512), np.arange(c0 + 384, c0 + 512),
                             np.arange(d0 + 512, d0 + 768)])
    w_vt = w_in[:, v_cols].T

    qd = MLA_NOPE + MLA_ROPE
    zq = lambda n: jnp.zeros((MLA_Q_RANK, n), mla_w_uq.dtype)
    uq, uq_rot = [], []
    for h in range(MLA_HEADS):
        blk = mla_w_uq[:, h * qd:(h + 1) * qd]
        src, sgn = _rot_half_cols(MLA_NOPE, MLA_ROPE)
        rot = blk[:, src] * jnp.asarray(sgn, blk.dtype)[None, :]
        uq += [blk, zq(LANES - qd)]
        uq_rot += [zq(MLA_NOPE), rot, zq(LANES - qd)]
    ukv = mla_w_ukv.reshape(MLA_KV_RANK, MLA_HEADS, MLA_NOPE + MLA_V)
    ukv_k = jnp.concatenate([ukv[:, :, :MLA_NOPE], jnp.zeros((MLA_KV_RANK, MLA_HEADS, LANES - MLA_NOPE), ukv.dtype)],
                            axis=2).reshape(MLA_KV_RANK, MLA_HEADS * LANES)
    ukv_vt = ukv[:, :, MLA_NOPE:].reshape(MLA_KV_RANK, MLA_HEADS * MLA_V).T

    def ax_gain(g):
        src, sgn = axial_rot(np.arange(HEAD_DIM))
        return jnp.tile(g, 2)[None, :].astype(F32), jnp.tile(g[src], 2)[None, :].astype(F32)

    ax_q, ax_q_rot = ax_gain(ax_q_norm)
    ax_k, ax_k_rot = ax_gain(ax_k_norm)
    lane = np.arange(LANES)
    g_avg = (lane[:, None] // HEAD_DIM == lane[None, :] // HEAD_DIM).astype(np.float32) / HEAD_DIM
    return {
        "w_tok": w_tok.astype(BF16), "w_vt": w_vt.astype(BF16),
        "w_uq": jnp.concatenate(uq, axis=1).astype(BF16), "w_uq_rot": jnp.concatenate(uq_rot, axis=1).astype(BF16),
        "w_ukv_k": ukv_k.astype(BF16), "w_ukv_vt": ukv_vt.astype(BF16),
        "g_avg": jnp.asarray(g_avg, BF16),
        "g_q": mla_q_norm[None, :].astype(F32), "g_kv": mla_kv_norm[None, :].astype(F32),
        "ax_q": ax_q, "ax_q_rot": ax_q_rot, "ax_k": ax_k, "ax_k_rot": ax_k_rot,
    }


def _position_tables(seq):
    pos = jnp.arange(seq, dtype=jnp.int32)
    half = HEAD_DIM // 2

    def angles(p, dim):
        inv = ROPE_BASE ** (-jnp.arange(0, dim, 2, dtype=F32) / dim)
        ang = p.astype(F32)[:, None] * inv[None, :]
        return jnp.cos(ang), jnp.sin(ang)

    cb, sb = angles(pos, MLA_ROPE)
    pad = LANES - MLA_NOPE - MLA_ROPE
    cos_b = jnp.concatenate([jnp.ones((seq, MLA_NOPE), F32), cb, cb, jnp.zeros((seq, pad), F32)], axis=1)
    sin_b = jnp.concatenate([jnp.zeros((seq, MLA_NOPE), F32), sb, sb, jnp.zeros((seq, pad), F32)], axis=1)
    cr, sr = angles(pos // GRID_W, half)
    cc, sc = angles(pos % GRID_W, half)
    cos_c = jnp.tile(jnp.concatenate([cr, cr, cc, cc], axis=1), (1, 2))
    sin_c = jnp.tile(jnp.concatenate([sr, sr, sc, sc], axis=1), (1, 2))
    aug = jnp.zeros((seq, LANES), F32)
    aug = aug.at[:, _AUG].set(1.0).at[:, _AUG + 1].set(1.0)
    aug = aug.at[:, _AUG + 2].set(((pos >> 7) << 7).astype(F32)).at[:, _AUG + 3].set((pos & 127).astype(F32))
    return {"cos_b": cos_b, "sin_b": sin_b, "cos_c": cos_c, "sin_c": sin_c, "aug_k": aug}


def _ffn_weights(w_gu, w_down):
    d, two_ff = w_gu.shape
    d_ff = two_ff // 2
    n_chunks = d_ff // FF_CHUNK
    wgu = w_gu.astype(BF16).reshape(d, 2, n_chunks, FF_CHUNK).transpose(2, 0, 1, 3).reshape(n_chunks, d, 2 * FF_CHUNK)
    wd = w_down.astype(BF16).reshape(n_chunks, FF_CHUNK, d)
    return wgu, wd


def kernel(x, w_in, win_sink, mla_q_norm, mla_w_uq, mla_kv_norm, mla_w_ukv, ax_q_norm, ax_k_norm,
           diff_lambda, diff_subln, w_out, ffn_w_gu, ffn_w_down, ln_g, ln_b):
    depth = w_in.shape[0]
    bsz, seq, d = x.shape
    alpha = (2 * depth) ** 0.25
    tabs = _position_tables(seq)
    row = lambda v: v[None, :].astype(F32)
    for l in range(depth):
        wgu, wd = _ffn_weights(ffn_w_gu[l, 0], ffn_w_down[l, 0])
        x = _ffn(x.reshape(bsz * seq, d), wgu, wd, row(ln_g[l, 0]), row(ln_b[l, 0]), alpha).reshape(bsz, seq, d)

        p = _layer_params(w_in[l], mla_q_norm[l], mla_w_uq[l], mla_kv_norm[l], mla_w_ukv[l],
                          ax_q_norm[l], ax_k_norm[l])
        qa, ka, vta, qb, kb, vtb, qc, kc, vtc, qd, kd, vtd = _prep(x, p, tabs)
        o_a = _window_attention(qa, ka, vta, win_sink[l].astype(F32))
        o_b = _dense_attention(qb, kb, vtb, n_heads=MLA_HEADS, kv_group=0, heads_per_step=HEADS_PER_STEP)
        o_c = _dense_attention(qc, kc, vtc, n_heads=AX_HEADS, kv_group=AX_HEADS // AX_KV_HEADS,
                               heads_per_step=AX_HEADS)
        o_d = _diff_attention(qd, kd, vtd, diff_lambda[l].astype(F32), diff_subln[l][:, None].astype(F32),
                              layer_idx=l, slope_exp0=WIN_HEADS + 1, heads_per_step=HEADS_PER_STEP)
        x = _outproj(x, o_a, o_b, o_c, o_d, w_out[l].T.astype(BF16), row(ln_g[l, 1]), row(ln_b[l, 1]), alpha)

        wgu, wd = _ffn_weights(ffn_w_gu[l, 1], ffn_w_down[l, 1])
        x = _ffn(x.reshape(bsz * seq, d), wgu, wd, row(ln_g[l, 2]), row(ln_b[l, 2]), alpha).reshape(bsz, seq, d)
    return x
```

```python
import functools
import math

import numpy as np
import jax
import jax.numpy as jnp
from jax import lax
from jax.experimental import pallas as pl
from jax.experimental.pallas import tpu as pltpu

F32 = jnp.float32
BF16 = jnp.bfloat16

HEAD_DIM = 64
HEAD_SHIFT = 6
GRID_W = 64
WIN_HEADS, WIN_KV_HEADS, WINDOW = 4, 2, 128
MLA_HEADS, MLA_Q_RANK, MLA_KV_RANK, MLA_NOPE, MLA_ROPE, MLA_V = 4, 256, 128, 64, 32, 64
AX_HEADS, AX_KV_HEADS = 4, 2
DIFF_HEADS, DIFF_QK, DIFF_V = 4, 32, 64
A_COLS = (WIN_HEADS + 2 * WIN_KV_HEADS) * HEAD_DIM
B_COLS = MLA_Q_RANK + MLA_KV_RANK + MLA_ROPE
C_COLS = (AX_HEADS + 2 * AX_KV_HEADS) * HEAD_DIM
D_COLS = DIFF_HEADS * (4 * DIFF_QK + DIFF_V)
ROPE_BASE = 10000.0
NORM_EPS = 1e-5
NEG_INF = -1e30

LANES = 128
F32_SUBLANES = 8
BF16_SUBLANES = 16
V_ROWS = HEAD_DIM + BF16_SUBLANES
VMEM_LIMIT = 56 * 1024 * 1024

TOK_TILE = 512
KV_CHUNK = 512
Q_TILE = 512
FF_CHUNK = 256
HEADS_PER_STEP = 4
DENSE_HEADS_PER_STEP = 2
CHUNKS_PER_STEP = 4
ROW_BLOCK = 256


def _cparams(sem, flags=None):
    return pltpu.CompilerParams(dimension_semantics=sem, vmem_limit_bytes=VMEM_LIMIT, flags=flags)


ATTENTION_FLAGS = None


def _resident(shape):
    nd = len(shape)
    return pl.BlockSpec(shape, lambda *_: (0,) * nd, pipeline_mode=pl.Buffered(1))


def _layer_norm_rows(y, g, b):
    mu = jnp.mean(y, axis=-1, keepdims=True)
    yc = y - mu
    var = jnp.mean(yc * yc, axis=-1, keepdims=True)
    return yc * lax.rsqrt(var + NORM_EPS) * g + b


def _ffn_kernel(x_ref, wgu_ref, wd_ref, g_ref, b_ref, o_ref, acc_ref, *, alpha, n_chunks):
    x = x_ref[...]
    xb = x.astype(BF16)
    acc_ref[...] = jnp.zeros_like(acc_ref)

    def body(c, carry):
        gu = jnp.dot(xb, wgu_ref[c], preferred_element_type=F32)
        gate = gu[:, :FF_CHUNK]
        up = gu[:, FF_CHUNK:]
        h = (gate * jax.nn.sigmoid(gate) * up).astype(BF16)
        acc_ref[...] += jnp.dot(h, wd_ref[c], preferred_element_type=F32)
        return carry

    lax.fori_loop(0, n_chunks, body, 0)
    y = alpha * x + 0.5 * acc_ref[...]
    o_ref[...] = _layer_norm_rows(y, g_ref[...], b_ref[...])


def _ffn(x2d, wgu, wd, g, b, alpha):
    n_tok, d = x2d.shape
    n_chunks = wgu.shape[0]
    tm = min(TOK_TILE, n_tok)
    return pl.pallas_call(
        functools.partial(_ffn_kernel, alpha=alpha, n_chunks=n_chunks),
        grid=(n_tok // tm,),
        in_specs=[
            pl.BlockSpec((tm, d), lambda i: (i, 0)),
            _resident(wgu.shape),
            _resident(wd.shape),
            _resident(g.shape),
            _resident(b.shape),
        ],
        out_specs=pl.BlockSpec((tm, d), lambda i: (i, 0)),
        out_shape=jax.ShapeDtypeStruct((n_tok, d), F32),
        scratch_shapes=[pltpu.VMEM((tm, d), F32)],
        compiler_params=_cparams(("parallel",)),
    )(x2d, wgu, wd, g, b)


_C_AQ, _C_AK, _C_CQ, _C_CK, _C_CQR, _C_CKR = 0, 256, 384, 640, 768, 1024
_C_DQ, _C_DK, _C_BQ, _C_BKV, _C_BKR, _C_BKRR, _C_END = 1152, 1664, 2176, 2432, 2560, 2688, 2816
_R_AV, _R_CV, _R_DV, _R_END = 0, 128, 256, 512


def _split_hi_lo(v):
    hi = v.astype(BF16)
    lo = (v - hi.astype(F32)).astype(BF16)
    return hi, lo


def _prep_kernel(x_ref, wtok_ref, wvt_ref, wuq_ref, wuqr_ref, wukvk_ref, wukvvt_ref, gavg_ref,
                 gq_ref, gkv_ref, axq_ref, axqr_ref, axk_ref, axkr_ref,
                 cosb_ref, sinb_ref, cosc_ref, sinc_ref, augk_ref,
                 qa_ref, ka_ref, vta_ref, qb_ref, kb_ref, vtb_ref,
                 qc_ref, kc_ref, vtc_ref, qd_ref, kd_ref, vtd_ref):
    xb = x_ref[0].astype(BF16)
    tm = xb.shape[0]
    pm = jnp.dot(xb, wtok_ref[...], preferred_element_type=F32)
    nt = (((1,), (1,)), ((), ()))
    pvt = lax.dot_general(wvt_ref[...], xb, nt, preferred_element_type=F32)

    row = lax.broadcasted_iota(jnp.int32, (BF16_SUBLANES, tm), 0)
    ones_tile = jnp.where(row == 0, 1.0, 0.0).astype(BF16)

    def put_vt(ref, head, vt):
        ref[0, head, 0, 0:HEAD_DIM, :] = vt.astype(BF16)
        ref[0, head, 0, HEAD_DIM:V_ROWS, :] = ones_tile

    qa_ref[0] = (pm[:, _C_AQ:_C_AQ + 256] * (HEAD_DIM ** -0.5)).astype(BF16)
    ka_ref[0] = pm[:, _C_AK:_C_AK + 128].astype(BF16)
    for h in range(WIN_KV_HEADS):
        put_vt(vta_ref, h, pvt[_R_AV + 64 * h:_R_AV + 64 * (h + 1)])

    cosc = cosc_ref[...]
    sinc = sinc_ref[...]
    gavg = gavg_ref[...]

    def head_rms_scale(v):
        hi, lo = _split_hi_lo(v * v)
        ms = (jnp.dot(hi, gavg, preferred_element_type=F32)
              + jnp.dot(lo, gavg, preferred_element_type=F32))
        return lax.rsqrt(ms + NORM_EPS)

    def axial(off, off_rot, g_ref_, gr_ref_, scale):
        v = pm[:, off:off + LANES]
        r = head_rms_scale(v)
        vn = v * r * g_ref_[...]
        vrn = pm[:, off_rot:off_rot + LANES] * r * gr_ref_[...]
        return ((vn * cosc + vrn * sinc) * scale).astype(BF16)

    for grp in range(2):
        qc_ref[0, :, grp * LANES:(grp + 1) * LANES] = axial(
            _C_CQ + grp * LANES, _C_CQR + grp * LANES, axq_ref, axqr_ref, HEAD_DIM ** -0.5 * LOG2E)
    kc_ref[0] = axial(_C_CK, _C_CKR, axk_ref, axkr_ref, 1.0)
    for h in range(AX_KV_HEADS):
        put_vt(vtc_ref, h, pvt[_R_CV + 64 * h:_R_CV + 64 * (h + 1)])

    qd_ref[0] = (pm[:, _C_DQ:_C_DQ + 512] * (DIFF_QK ** -0.5)).astype(BF16)
    augk = augk_ref[...]
    for h in range(DIFF_HEADS):
        kd_ref[0, :, h * LANES:(h + 1) * LANES] = (
            pm[:, _C_DK + h * LANES:_C_DK + (h + 1) * LANES] + augk).astype(BF16)
        put_vt(vtd_ref, h, pvt[_R_DV + 64 * h:_R_DV + 64 * (h + 1)])

    cosb = cosb_ref[...]
    sinb = sinb_ref[...]
    cq = pm[:, _C_BQ:_C_BQ + MLA_Q_RANK]
    cqn = (cq * lax.rsqrt(jnp.mean(cq * cq, axis=-1, keepdims=True) + NORM_EPS) * gq_ref[...]).astype(BF16)
    qw = jnp.dot(cqn, wuq_ref[...], preferred_element_type=F32)
    qwr = jnp.dot(cqn, wuqr_ref[...], preferred_element_type=F32)
    ckv = pm[:, _C_BKV:_C_BKV + MLA_KV_RANK]
    ckvn = (ckv * lax.rsqrt(jnp.mean(ckv * ckv, axis=-1, keepdims=True) + NORM_EPS) * gkv_ref[...]).astype(BF16)
    kw = jnp.dot(ckvn, wukvk_ref[...], preferred_element_type=F32)
    vbt = lax.dot_general(wukvvt_ref[...], ckvn, nt, preferred_element_type=F32)
    k_rope = pm[:, _C_BKR:_C_BKR + LANES] * cosb + pm[:, _C_BKRR:_C_BKRR + LANES] * sinb
    q_scale = (MLA_NOPE + MLA_ROPE) ** -0.5 * LOG2E
    for h in range(MLA_HEADS):
        sl = slice(h * LANES, (h + 1) * LANES)
        qb_ref[0, :, sl] = ((qw[:, sl] * cosb + qwr[:, sl] * sinb) * q_scale).astype(BF16)
        kb_ref[0, :, sl] = (kw[:, sl] + k_rope).astype(BF16)
        put_vt(vtb_ref, h, vbt[64 * h:64 * (h + 1)])


def _prep(x, p, tabs):
    bsz, seq, d = x.shape
    tm = KV_CHUNK
    nch = seq // tm
    tok = lambda w: pl.BlockSpec((1, tm, w), lambda b, i: (b, i, 0))
    tab = pl.BlockSpec((tm, LANES), lambda b, i: (i, 0))
    vt = lambda nh: pl.BlockSpec((1, nh, 1, V_ROWS, tm), lambda b, i: (b, 0, i, 0, 0))
    vt_shape = lambda nh: jax.ShapeDtypeStruct((bsz, nh, nch, V_ROWS, tm), BF16)
    tok_shape = lambda w: jax.ShapeDtypeStruct((bsz, seq, w), BF16)
    weights = [p["w_tok"], p["w_vt"], p["w_uq"], p["w_uq_rot"], p["w_ukv_k"], p["w_ukv_vt"], p["g_avg"],
               p["g_q"], p["g_kv"], p["ax_q"], p["ax_q_rot"], p["ax_k"], p["ax_k_rot"]]
    return pl.pallas_call(
        _prep_kernel,
        grid=(bsz, nch),
        in_specs=[tok(d)] + [_resident(w.shape) for w in weights] + [tab] * 5,
        out_specs=[tok(256), tok(128), vt(2), tok(512), tok(512), vt(4),
                   tok(256), tok(128), vt(2), tok(512), tok(512), vt(4)],
        out_shape=[tok_shape(256), tok_shape(128), vt_shape(2), tok_shape(512), tok_shape(512), vt_shape(4),
                   tok_shape(256), tok_shape(128), vt_shape(2), tok_shape(512), tok_shape(512), vt_shape(4)],
        compiler_params=_cparams(("parallel", "parallel")),
    )(x, *weights, tabs["cos_b"], tabs["sin_b"], tabs["cos_c"], tabs["sin_c"], tabs["aug_k"])


_NT = (((1,), (1,)), ((), ()))
LOG2E = math.log2(math.e)


def _key_rows(k_ref, j, slab, cps):
    rows = pl.ds(pl.multiple_of(j * (cps * KV_CHUNK), cps * KV_CHUNK), cps * KV_CHUNK)
    return k_ref[0, rows, slab * LANES:(slab + 1) * LANES]


def _chain_body(k_ref, vt_ref, chains, exp_fn, transform=None, cps=1):
    def body(j, carry):
        scores = []
        for n, (q, k_slab, _) in enumerate(chains):
            s = lax.dot_general(_key_rows(k_ref, j, k_slab, cps), q, _NT, preferred_element_type=F32)
            scores.append(s if transform is None else transform(j, s, n))
        probs = []
        for s, (m, _) in zip(scores, carry):
            m_new = jnp.maximum(m, jnp.max(s, axis=0, keepdims=True))
            probs.append((m_new, exp_fn(m - m_new), exp_fn(s - m_new).astype(BF16)))
        out = []
        for (_, _, v_head), (m_new, alpha, p), (_, acc) in zip(chains, probs, carry):
            pv = None
            for c in range(cps):
                part = jnp.dot(vt_ref[0, v_head, j * cps + c], p[c * KV_CHUNK:(c + 1) * KV_CHUNK],
                               preferred_element_type=F32)
                pv = part if pv is None else pv + part
            out.append((m_new, acc * alpha + pv))
        return tuple(out)
    return body


def _value_matmul(vt_ref, v_head, step, cps, p):
    pv = None
    for c in range(cps):
        part = jnp.dot(vt_ref[0, v_head, step * cps + c], p[c * KV_CHUNK:(c + 1) * KV_CHUNK],
                       preferred_element_type=F32)
        pv = part if pv is None else pv + part
    return pv


def _score_block(k_ref, q, k_slab, row0, s_write, n, r):
    rows = pl.ds(pl.multiple_of(row0 + r * ROW_BLOCK, ROW_BLOCK), ROW_BLOCK)
    s = lax.dot_general(k_ref[0, rows, k_slab * LANES:(k_slab + 1) * LANES], q, _NT, preferred_element_type=F32)
    s_write[n, r * ROW_BLOCK:(r + 1) * ROW_BLOCK, :] = s
    return jnp.max(s.reshape(ROW_BLOCK // F32_SUBLANES, F32_SUBLANES, s.shape[-1]), axis=0)


def _first_scores(k_ref, chains, cps, s_write):
    out = []
    for n, (q, k_slab, _) in enumerate(chains):
        parts = [_score_block(k_ref, q, k_slab, 0, s_write, n, r) for r in range(cps * KV_CHUNK // ROW_BLOCK)]
        out.append(functools.reduce(jnp.maximum, parts))
    return out


def _pipelined_step(k_ref, vt_ref, chains, exp_fn, cps, j, carry, s_read, s_write):
    step_rows = cps * KV_CHUNK
    blocks_per_chunk = KV_CHUNK // ROW_BLOCK
    next_row0 = (j + 1) * step_rows
    out = []
    for n, ((q, k_slab, v_head), (m, acc, smax)) in enumerate(zip(chains, carry)):
        m_new = jnp.maximum(m, jnp.max(smax, axis=0, keepdims=True))
        alpha = exp_fn(m - m_new)
        next_max, p_blocks = smax, []
        for r in range(step_rows // ROW_BLOCK):
            if s_write is not None:
                part = _score_block(k_ref, q, k_slab, next_row0, s_write, n, r)
                next_max = part if r == 0 else jnp.maximum(next_max, part)
            p_blocks.append(exp_fn(s_read[n, r * ROW_BLOCK:(r + 1) * ROW_BLOCK, :] - m_new).astype(BF16))
        pv = None
        for c in range(cps):
            p = jnp.concatenate(p_blocks[c * blocks_per_chunk:(c + 1) * blocks_per_chunk], axis=0)
            part = jnp.dot(vt_ref[0, v_head, j * cps + c], p, preferred_element_type=F32)
            pv = part if pv is None else pv + part
        out.append((m_new, acc * alpha + pv, next_max))
    return tuple(out)


def _normalised(acc):
    return acc[:HEAD_DIM] / acc[HEAD_DIM:HEAD_DIM + 1]


def _single_buffered(block, index_map):
    return pl.BlockSpec(block, index_map, pipeline_mode=pl.Buffered(1))


def _dense_kernel(q_ref, k_ref, vt_ref, o_ref, s0_ref, s1_ref, *, n_chunks, heads):
    tq = q_ref.shape[1]
    lane = lax.broadcasted_iota(jnp.int32, (tq, LANES), 1)
    chains = []
    for q_slab, k_slab, v_head, shared_key_slab in heads:
        q = q_ref[0, :, q_slab * LANES:(q_slab + 1) * LANES]
        if shared_key_slab:
            q = jnp.where((lane >> HEAD_SHIFT) == pl.program_id(1), q.astype(F32), 0.0).astype(BF16)
        chains.append((q, k_slab, v_head))
    cps = s0_ref.shape[1] // KV_CHUNK
    n_steps = n_chunks // cps
    step = functools.partial(_pipelined_step, k_ref, vt_ref, chains, jnp.exp2, cps)
    first_max = _first_scores(k_ref, chains, cps, s0_ref)
    init = tuple((jnp.full((1, tq), NEG_INF, F32), jnp.zeros((V_ROWS, tq), F32), smax) for smax in first_max)

    def pair(t, carry):
        carry = step(2 * t, carry, s0_ref, s1_ref)
        return step(2 * t + 1, carry, s1_ref, s0_ref)

    carry = lax.fori_loop(0, n_steps // 2 - 1, pair, init)
    carry = step(n_steps - 2, carry, s0_ref, s1_ref)
    final = step(n_steps - 1, carry, s1_ref, None)
    for n, (_, acc, _) in enumerate(final):
        o_ref[0, n * HEAD_DIM:(n + 1) * HEAD_DIM, :] = _normalised(acc).astype(BF16)


def _dense_attention(q, k, vt, *, n_heads, kv_group, heads_per_step):
    bsz, seq, _ = q.shape
    nch = vt.shape[2]
    tq = min(Q_TILE, seq)
    hp = heads_per_step
    if kv_group:
        assert hp == kv_group and n_heads // kv_group == LANES // HEAD_DIM
        heads = tuple((s, 0, 0, True) for s in range(kv_group))
        q_spec = pl.BlockSpec((1, tq, kv_group * LANES), lambda b, g, i: (b, i, 0))
        k_spec = _single_buffered((1, seq, LANES), lambda b, g, i: (b, 0, 0))
        vt_spec = _single_buffered((1, 1, nch, V_ROWS, KV_CHUNK), lambda b, g, i: (b, g, 0, 0, 0))
    else:
        heads = tuple((n, n, n, False) for n in range(hp))
        q_spec = pl.BlockSpec((1, tq, hp * LANES), lambda b, g, i: (b, i, g))
        k_spec = _single_buffered((1, seq, hp * LANES), lambda b, g, i: (b, 0, g))
        vt_spec = _single_buffered((1, hp, nch, V_ROWS, KV_CHUNK), lambda b, g, i: (b, g, 0, 0, 0))
    assert nch % 2 == 0
    cps = math.gcd(CHUNKS_PER_STEP, nch // 2)
    s_buffer = pltpu.VMEM((hp, cps * KV_CHUNK, tq), F32)
    return pl.pallas_call(
        functools.partial(_dense_kernel, n_chunks=nch, heads=heads),
        grid=(bsz, n_heads // hp, seq // tq),
        in_specs=[q_spec, k_spec, vt_spec],
        out_specs=pl.BlockSpec((1, hp * HEAD_DIM, tq), lambda b, g, i: (b, g, i)),
        out_shape=jax.ShapeDtypeStruct((bsz, n_heads * HEAD_DIM, seq), BF16),
        scratch_shapes=[s_buffer, s_buffer],
        compiler_params=_cparams(("parallel", "parallel", "arbitrary"), ATTENTION_FLAGS),
    )(q, k, vt)


_AUG = 2 * DIFF_QK


def _diff_kernel(lam_ref, g_ref, q_ref, k_ref, vt_ref, o_ref, *, n_chunks, hp, lam_init, slope_exp0):
    i = pl.program_id(2)
    tq = q_ref.shape[1]
    lane = lax.broadcasted_iota(jnp.int32, (tq, LANES), 1)
    qpos = i * tq + lax.broadcasted_iota(jnp.int32, (tq, LANES), 0)
    q_hi = ((qpos >> 7) << 7).astype(F32)
    q_lo = (qpos & 127).astype(F32)
    zero = jnp.zeros((tq, LANES), F32)

    slopes, before, after, plain = [], [], [], []
    for n in range(hp):
        head = pl.program_id(1) * hp + n
        slope_bits = jnp.full((1, LANES), 127 - slope_exp0, jnp.int32) - head
        slope = lax.bitcast_convert_type(slope_bits << 23, F32)
        qf = q_ref[0, :, n * LANES:(n + 1) * LANES].astype(F32)
        aug = jnp.where(lane == _AUG, -slope * q_hi,
                        jnp.where(lane == _AUG + 1, -slope * q_lo,
                                  jnp.where((lane == _AUG + 2) | (lane == _AUG + 3), slope, 0.0)))
        q1 = jnp.where(lane < DIFF_QK, qf, 0.0)
        q2 = jnp.where((lane >= DIFF_QK) & (lane < 2 * DIFF_QK), qf, 0.0)
        both = lambda extra: jnp.concatenate([q1 + extra, q2 + extra], axis=0).astype(BF16)
        slopes.append(slope[:, :1])
        before.append((both(aug), n, n))
        after.append((both(-aug), n, n))
        plain.append((both(zero), n, n))

    carry = tuple((jnp.full((1, 2 * tq), NEG_INF, F32), jnp.zeros((V_ROWS, 2 * tq), F32)) for _ in range(hp))
    jd = (i * tq) // KV_CHUNK
    carry = lax.fori_loop(0, jd, _chain_body(k_ref, vt_ref, before, jnp.exp), carry)
    kpos = jd * KV_CHUNK + lax.broadcasted_iota(jnp.int32, (KV_CHUNK, tq), 0)
    tpos = i * tq + lax.broadcasted_iota(jnp.int32, (KV_CHUNK, tq), 1)
    dist = jnp.abs(tpos - kpos).astype(F32)

    def diagonal_bias(j, s, n):
        bias = -slopes[n] * dist
        return s + jnp.concatenate([bias, bias], axis=1)

    carry = _chain_body(k_ref, vt_ref, plain, jnp.exp, diagonal_bias)(jd, carry)
    final = lax.fori_loop(jd + 1, n_chunks, _chain_body(k_ref, vt_ref, after, jnp.exp), carry)

    lp = lam_ref[...]
    lam = (jnp.exp(jnp.sum(lp[0:1] * lp[1:2], axis=-1, keepdims=True))
           - jnp.exp(jnp.sum(lp[2:3] * lp[3:4], axis=-1, keepdims=True)) + lam_init)
    for n, (_, acc) in enumerate(final):
        o = _normalised(acc[:, :tq]) - lam * _normalised(acc[:, tq:])
        o = o * lax.rsqrt(jnp.mean(o * o, axis=0, keepdims=True) + NORM_EPS)
        o_ref[0, n * DIFF_V:(n + 1) * DIFF_V, :] = (o * g_ref[...] * (1.0 - lam_init)).astype(BF16)


def _diff_attention(q, k, vt, lam_params, subln_col, *, layer_idx, slope_exp0, heads_per_step):
    bsz, seq, _ = q.shape
    nch = vt.shape[2]
    tq = min(Q_TILE, seq)
    hp = heads_per_step
    assert KV_CHUNK % tq == 0 and seq <= 256 * LANES and DIFF_HEADS % hp == 0
    lam_init = 0.8 - 0.6 * math.exp(-0.3 * layer_idx)
    return pl.pallas_call(
        functools.partial(_diff_kernel, n_chunks=nch, hp=hp, lam_init=lam_init, slope_exp0=slope_exp0),
        grid=(bsz, DIFF_HEADS // hp, seq // tq),
        in_specs=[
            _resident(lam_params.shape),
            _resident(subln_col.shape),
            pl.BlockSpec((1, tq, hp * LANES), lambda b, g, i: (b, i, g)),
            _single_buffered((1, seq, hp * LANES), lambda b, g, i: (b, 0, g)),
            _single_buffered((1, hp, nch, V_ROWS, KV_CHUNK), lambda b, g, i: (b, g, 0, 0, 0)),
        ],
        out_specs=pl.BlockSpec((1, hp * DIFF_V, tq), lambda b, g, i: (b, g, i)),
        out_shape=jax.ShapeDtypeStruct((bsz, DIFF_HEADS * DIFF_V, seq), BF16),
        compiler_params=_cparams(("parallel", "parallel", "arbitrary"), ATTENTION_FLAGS),
    )(lam_params, subln_col, q, k, vt)


def _window_kernel(sink_ref, q_ref, k_ref, vt_ref, o_ref, *, seq):
    i = pl.program_id(1)
    tq = q_ref.shape[1]
    q0 = i * tq
    j_lo = jnp.maximum(q0 - WINDOW, 0) // KV_CHUNK
    j_hi = jnp.minimum(q0 + tq - 1 + WINDOW, seq - 1) // KV_CHUNK
    lane = lax.broadcasted_iota(jnp.int32, (tq, LANES), 1)
    acc_row = lax.broadcasted_iota(jnp.int32, (V_ROWS, tq), 0)
    tpos = q0 + lax.broadcasted_iota(jnp.int32, (KV_CHUNK, tq), 1)
    krow = lax.broadcasted_iota(jnp.int32, (KV_CHUNK, tq), 0)
    group = WIN_HEADS // WIN_KV_HEADS
    chains, init = [], []
    for head in range(WIN_HEADS):
        kv_head, slab = head // group, head % group
        qf = q_ref[0, :, slab * LANES:(slab + 1) * LANES].astype(F32)
        chains.append((jnp.where((lane >> HEAD_SHIFT) == kv_head, qf, 0.0).astype(BF16), 0, kv_head))
        init.append((jnp.full((1, tq), sink_ref[head], F32), jnp.where(acc_row == HEAD_DIM, 1.0, 0.0)))

    def band(j, s, n):
        dist = jnp.abs(tpos - (j * KV_CHUNK + krow))
        return jnp.where(dist <= WINDOW, s - (2.0 ** -(n + 1)) * dist.astype(F32), NEG_INF)

    final = lax.fori_loop(j_lo, j_hi + 1, _chain_body(k_ref, vt_ref, chains, jnp.exp, band), tuple(init))
    for n, (_, acc) in enumerate(final):
        o_ref[0, n * HEAD_DIM:(n + 1) * HEAD_DIM, :] = _normalised(acc).astype(BF16)


def _window_attention(q, k, vt, sink):
    bsz, seq, _ = q.shape
    nch = vt.shape[2]
    tq = min(Q_TILE, seq)
    return pl.pallas_call(
        functools.partial(_window_kernel, seq=seq),
        grid=(bsz, seq // tq),
        in_specs=[
            pl.BlockSpec(memory_space=pltpu.SMEM),
            pl.BlockSpec((1, tq, 2 * LANES), lambda b, i: (b, i, 0)),
            _single_buffered((1, seq, LANES), lambda b, i: (b, 0, 0)),
            _single_buffered((1, WIN_KV_HEADS, nch, V_ROWS, KV_CHUNK), lambda b, i: (b, 0, 0, 0, 0)),
        ],
        out_specs=pl.BlockSpec((1, WIN_HEADS * HEAD_DIM, tq), lambda b, i: (b, 0, i)),
        out_shape=jax.ShapeDtypeStruct((bsz, WIN_HEADS * HEAD_DIM, seq), BF16),
        compiler_params=_cparams(("parallel", "arbitrary")),
    )(sink, q, k, vt)


def _outproj_kernel(x_ref, oa_ref, ob_ref, oc_ref, od_ref, wt_ref, g_ref, b_ref, o_ref, *, alpha):
    mix_t = None
    for m, ref in enumerate((oa_ref, ob_ref, oc_ref, od_ref)):
        part = jnp.dot(wt_ref[:, m * 256:(m + 1) * 256], ref[0], preferred_element_type=F32)
        mix_t = part if mix_t is None else mix_t + part
    y = alpha * x_ref[0] + mix_t.T
    o_ref[0] = _layer_norm_rows(y, g_ref[...], b_ref[...])


def _outproj(x, o_a, o_b, o_c, o_d, w_out_t, g, b, alpha):
    bsz, seq, d = x.shape
    tm = min(TOK_TILE, seq)
    ot = pl.BlockSpec((1, 256, tm), lambda bb, i: (bb, 0, i))
    tok = pl.BlockSpec((1, tm, d), lambda bb, i: (bb, i, 0))
    return pl.pallas_call(
        functools.partial(_outproj_kernel, alpha=alpha),
        grid=(bsz, seq // tm),
        in_specs=[tok, ot, ot, ot, ot, _resident(w_out_t.shape), _resident(g.shape), _resident(b.shape)],
        out_specs=tok,
        out_shape=jax.ShapeDtypeStruct((bsz, seq, d), F32),
        compiler_params=_cparams(("parallel", "parallel")),
    )(x, o_a, o_b, o_c, o_d, w_out_t, g, b)


def _rot_half_cols(start, width):
    half = width // 2
    src = np.concatenate([np.arange(start + half, start + width), np.arange(start, start + half)])
    sgn = np.concatenate([-np.ones(half), np.ones(half)])
    return src, sgn


def _gqa_slab_order(n_heads, n_kv):
    group = n_heads // n_kv
    return [kv * group + s for s in range(group) for kv in range(n_kv)]


def _layer_params(w_in, mla_q_norm, mla_w_uq, mla_kv_norm, mla_w_ukv, ax_q_norm, ax_k_norm):
    d = w_in.shape[0]
    zeros = lambda n: jnp.zeros((d, n), w_in.dtype)
    a0, b0, c0, d0 = 0, A_COLS, A_COLS + B_COLS, A_COLS + B_COLS + C_COLS
    head_cols = lambda base, h: np.arange(base + h * HEAD_DIM, base + (h + 1) * HEAD_DIM)

    def axial_rot(cols):
        src, sgn = [], []
        for blk in range(0, HEAD_DIM, HEAD_DIM // 2):
            s_, g_ = _rot_half_cols(blk, HEAD_DIM // 2)
            src.append(cols[s_])
            sgn.append(g_)
        return np.concatenate(src), np.concatenate(sgn)

    order = _gqa_slab_order(AX_HEADS, AX_KV_HEADS)
    aq = np.concatenate([head_cols(a0, h) for h in order])
    ak = np.arange(a0 + 256, a0 + 384)
    cq = np.concatenate([head_cols(c0, h) for h in order])
    ck = np.arange(c0 + 256, c0 + 384)
    cq_rot = [axial_rot(head_cols(c0, h)) for h in order]
    ck_rot = [axial_rot(head_cols(c0 + 256, h)) for h in range(AX_KV_HEADS)]

    def gather(src, sgn=None):
        w = w_in[:, np.asarray(src)]
        return w if sgn is None else w * jnp.asarray(sgn, w.dtype)[None, :]

    def widen(cols):
        return jnp.concatenate([gather(cols), zeros(LANES - len(cols))], axis=1)

    kr = np.arange(b0 + MLA_Q_RANK + MLA_KV_RANK, b0 + B_COLS)
    kr_src, kr_sgn = _rot_half_cols(kr[0], MLA_ROPE)
    place_rope = lambda w: jnp.concatenate([zeros(MLA_NOPE), w, zeros(LANES - MLA_NOPE - MLA_ROPE)], axis=1)

    w_tok = jnp.concatenate(
        [gather(aq), gather(ak), gather(cq), gather(ck)]
        + [gather(s, g) for s, g in cq_rot] + [gather(s, g) for s, g in ck_rot]
        + [widen(head_cols(d0, h)) for h in range(DIFF_HEADS)]
        + [widen(head_cols(d0 + 256, h)) for h in range(DIFF_HEADS)]
        + [gather(np.arange(b0, b0 + MLA_Q_RANK + MLA_KV_RANK)),
           place_rope(gather(kr)), place_rope(gather(kr_src, kr_sgn))], axis=1)
    assert w_tok.shape[1] == _C_END
    v_cols = np.concatenate([np.arange(a0 + 384, a0 + 512), np.arange(c0 + 384, c0 + 512),
                             np.arange(d0 + 512, d0 + 768)])
    w_vt = w_in[:, v_cols].T

    qd = MLA_NOPE + MLA_ROPE
    zq = lambda n: jnp.zeros((MLA_Q_RANK, n), mla_w_uq.dtype)
    uq, uq_rot = [], []
    for h in range(MLA_HEADS):
        blk = mla_w_uq[:, h * qd:(h + 1) * qd]
        src, sgn = _rot_half_cols(MLA_NOPE, MLA_ROPE)
        rot = blk[:, src] * jnp.asarray(sgn, blk.dtype)[None, :]
        uq += [blk, zq(LANES - qd)]
        uq_rot += [zq(MLA_NOPE), rot, zq(LANES - qd)]
    ukv = mla_w_ukv.reshape(MLA_KV_RANK, MLA_HEADS, MLA_NOPE + MLA_V)
    ukv_k = jnp.concatenate([ukv[:, :, :MLA_NOPE], jnp.zeros((MLA_KV_RANK, MLA_HEADS, LANES - MLA_NOPE), ukv.dtype)],
                            axis=2).reshape(MLA_KV_RANK, MLA_HEADS * LANES)
    ukv_vt = ukv[:, :, MLA_NOPE:].reshape(MLA_KV_RANK, MLA_HEADS * MLA_V).T

    def ax_gain(g):
        src, sgn = axial_rot(np.arange(HEAD_DIM))
        return jnp.tile(g, 2)[None, :].astype(F32), jnp.tile(g[src], 2)[None, :].astype(F32)

    ax_q, ax_q_rot = ax_gain(ax_q_norm)
    ax_k, ax_k_rot = ax_gain(ax_k_norm)
    lane = np.arange(LANES)
    g_avg = (lane[:, None] // HEAD_DIM == lane[None, :] // HEAD_DIM).astype(np.float32) / HEAD_DIM
    return {
        "w_tok": w_tok.astype(BF16), "w_vt": w_vt.astype(BF16),
        "w_uq": jnp.concatenate(uq, axis=1).astype(BF16), "w_uq_rot": jnp.concatenate(uq_rot, axis=1).astype(BF16),
        "w_ukv_k": ukv_k.astype(BF16), "w_ukv_vt": ukv_vt.astype(BF16),
        "g_avg": jnp.asarray(g_avg, BF16),
        "g_q": mla_q_norm[None, :].astype(F32), "g_kv": mla_kv_norm[None, :].astype(F32),
        "ax_q": ax_q, "ax_q_rot": ax_q_rot, "ax_k": ax_k, "ax_k_rot": ax_k_rot,
    }


def _position_tables(seq):
    pos = jnp.arange(seq, dtype=jnp.int32)
    half = HEAD_DIM // 2

    def angles(p, dim):
        inv = ROPE_BASE ** (-jnp.arange(0, dim, 2, dtype=F32) / dim)
        ang = p.astype(F32)[:, None] * inv[None, :]
        return jnp.cos(ang), jnp.sin(ang)

    cb, sb = angles(pos, MLA_ROPE)
    pad = LANES - MLA_NOPE - MLA_ROPE
    cos_b = jnp.concatenate([jnp.ones((seq, MLA_NOPE), F32), cb, cb, jnp.zeros((seq, pad), F32)], axis=1)
    sin_b = jnp.concatenate([jnp.zeros((seq, MLA_NOPE), F32), sb, sb, jnp.zeros((seq, pad), F32)], axis=1)
    cr, sr = angles(pos // GRID_W, half)
    cc, sc = angles(pos % GRID_W, half)
    cos_c = jnp.tile(jnp.concatenate([cr, cr, cc, cc], axis=1), (1, 2))
    sin_c = jnp.tile(jnp.concatenate([sr, sr, sc, sc], axis=1), (1, 2))
    aug = jnp.zeros((seq, LANES), F32)
    aug = aug.at[:, _AUG].set(1.0).at[:, _AUG + 1].set(1.0)
    aug = aug.at[:, _AUG + 2].set(((pos >> 7) << 7).astype(F32)).at[:, _AUG + 3].set((pos & 127).astype(F32))
    return {"cos_b": cos_b, "sin_b": sin_b, "cos_c": cos_c, "sin_c": sin_c, "aug_k": aug}


def _ffn_weights(w_gu, w_down):
    d, two_ff = w_gu.shape
    d_ff = two_ff // 2
    n_chunks = d_ff // FF_CHUNK
    wgu = w_gu.astype(BF16).reshape(d, 2, n_chunks, FF_CHUNK).transpose(2, 0, 1, 3).reshape(n_chunks, d, 2 * FF_CHUNK)
    wd = w_down.astype(BF16).reshape(n_chunks, FF_CHUNK, d)
    return wgu, wd


def kernel(x, w_in, win_sink, mla_q_norm, mla_w_uq, mla_kv_norm, mla_w_ukv, ax_q_norm, ax_k_norm,
           diff_lambda, diff_subln, w_out, ffn_w_gu, ffn_w_down, ln_g, ln_b):
    depth = w_in.shape[0]
    bsz, seq, d = x.shape
    alpha = (2 * depth) ** 0.25
    tabs = _position_tables(seq)
    row = lambda v: v[None, :].astype(F32)
    for l in range(depth):
        wgu, wd = _ffn_weights(ffn_w_gu[l, 0], ffn_w_down[l, 0])
        x = _ffn(x.reshape(bsz * seq, d), wgu, wd, row(ln_g[l, 0]), row(ln_b[l, 0]), alpha).reshape(bsz, seq, d)

        p = _layer_params(w_in[l], mla_q_norm[l], mla_w_uq[l], mla_kv_norm[l], mla_w_ukv[l],
                          ax_q_norm[l], ax_k_norm[l])
        qa, ka, vta, qb, kb, vtb, qc, kc, vtc, qd, kd, vtd = _prep(x, p, tabs)
        o_a = _window_attention(qa, ka, vta, win_sink[l].astype(F32))
        o_b = _dense_attention(qb, kb, vtb, n_heads=MLA_HEADS, kv_group=0, heads_per_step=DENSE_HEADS_PER_STEP)
        o_c = _dense_attention(qc, kc, vtc, n_heads=AX_HEADS, kv_group=AX_HEADS // AX_KV_HEADS,
                               heads_per_step=AX_HEADS // AX_KV_HEADS)
        o_d = _diff_attention(qd, kd, vtd, diff_lambda[l].astype(F32), diff_subln[l][:, None].astype(F32),
                              layer_idx=l, slope_exp0=WIN_HEADS + 1, heads_per_step=HEADS_PER_STEP)
        x = _outproj(x, o_a, o_b, o_c, o_d, w_out[l].T.astype(BF16), row(ln_g[l, 1]), row(ln_b[l, 1]), alpha)

        wgu, wd = _ffn_weights(ffn_w_gu[l, 1], ffn_w_down[l, 1])
        x = _ffn(x.reshape(bsz * seq, d), wgu, wd, row(ln_g[l, 2]), row(ln_b[l, 2]), alpha).reshape(bsz, seq, d)
    return x
```

```python
import functools
import math

import numpy as np
import jax
import jax.numpy as jnp
from jax import lax
from jax.experimental import pallas as pl
from jax.experimental.pallas import tpu as pltpu

F32 = jnp.float32
BF16 = jnp.bfloat16

HEAD_DIM = 64
HEAD_SHIFT = 6
GRID_W = 64
WIN_HEADS, WIN_KV_HEADS, WINDOW = 4, 2, 128
MLA_HEADS, MLA_Q_RANK, MLA_KV_RANK, MLA_NOPE, MLA_ROPE, MLA_V = 4, 256, 128, 64, 32, 64
AX_HEADS, AX_KV_HEADS = 4, 2
DIFF_HEADS, DIFF_QK, DIFF_V = 4, 32, 64
A_COLS = (WIN_HEADS + 2 * WIN_KV_HEADS) * HEAD_DIM
B_COLS = MLA_Q_RANK + MLA_KV_RANK + MLA_ROPE
C_COLS = (AX_HEADS + 2 * AX_KV_HEADS) * HEAD_DIM
D_COLS = DIFF_HEADS * (4 * DIFF_QK + DIFF_V)
ROPE_BASE = 10000.0
NORM_EPS = 1e-5
NEG_INF = -1e30

LANES = 128
F32_SUBLANES = 8
BF16_SUBLANES = 16
V_ROWS = HEAD_DIM + BF16_SUBLANES
VMEM_LIMIT = 56 * 1024 * 1024

TOK_TILE = 512
KV_CHUNK = 512
Q_TILE = 512
FF_CHUNK = 256
HEADS_PER_STEP = 2
CHUNKS_PER_STEP = 4
ROW_BLOCK = 256


def _cparams(sem, flags=None):
    return pltpu.CompilerParams(dimension_semantics=sem, vmem_limit_bytes=VMEM_LIMIT, flags=flags)


ATTENTION_FLAGS = None


def _resident(shape):
    nd = len(shape)
    return pl.BlockSpec(shape, lambda *_: (0,) * nd, pipeline_mode=pl.Buffered(1))


def _layer_norm_rows(y, g, b):
    mu = jnp.mean(y, axis=-1, keepdims=True)
    yc = y - mu
    var = jnp.mean(yc * yc, axis=-1, keepdims=True)
    return yc * lax.rsqrt(var + NORM_EPS) * g + b


def _ffn_kernel(x_ref, wgu_ref, wd_ref, g_ref, b_ref, o_ref, acc_ref, *, alpha, n_chunks):
    x = x_ref[...]
    xb = x.astype(BF16)
    acc_ref[...] = jnp.zeros_like(acc_ref)

    def body(c, carry):
        gu = jnp.dot(xb, wgu_ref[c], preferred_element_type=F32)
        gate = gu[:, :FF_CHUNK]
        up = gu[:, FF_CHUNK:]
        h = (gate * jax.nn.sigmoid(gate) * up).astype(BF16)
        acc_ref[...] += jnp.dot(h, wd_ref[c], preferred_element_type=F32)
        return carry

    lax.fori_loop(0, n_chunks, body, 0)
    y = alpha * x + 0.5 * acc_ref[...]
    o_ref[...] = _layer_norm_rows(y, g_ref[...], b_ref[...])


def _ffn(x2d, wgu, wd, g, b, alpha):
    n_tok, d = x2d.shape
    n_chunks = wgu.shape[0]
    tm = min(TOK_TILE, n_tok)
    return pl.pallas_call(
        functools.partial(_ffn_kernel, alpha=alpha, n_chunks=n_chunks),
        grid=(n_tok // tm,),
        in_specs=[
            pl.BlockSpec((tm, d), lambda i: (i, 0)),
            _resident(wgu.shape),
            _resident(wd.shape),
            _resident(g.shape),
            _resident(b.shape),
        ],
        out_specs=pl.BlockSpec((tm, d), lambda i: (i, 0)),
        out_shape=jax.ShapeDtypeStruct((n_tok, d), F32),
        scratch_shapes=[pltpu.VMEM((tm, d), F32)],
        compiler_params=_cparams(("parallel",)),
    )(x2d, wgu, wd, g, b)


_C_AQ, _C_AK, _C_CQ, _C_CK, _C_CQR, _C_CKR = 0, 256, 384, 640, 768, 1024
_C_DQ, _C_DK, _C_BQ, _C_BKV, _C_BKR, _C_BKRR, _C_END = 1152, 1664, 2176, 2432, 2560, 2688, 2816
_R_AV, _R_CV, _R_DV, _R_END = 0, 128, 256, 512


def _split_hi_lo(v):
    hi = v.astype(BF16)
    lo = (v - hi.astype(F32)).astype(BF16)
    return hi, lo


def _prep_kernel(x_ref, wtok_ref, wvt_ref, wuq_ref, wuqr_ref, wukvk_ref, wukvvt_ref, gavg_ref,
                 gq_ref, gkv_ref, axq_ref, axqr_ref, axk_ref, axkr_ref,
                 cosb_ref, sinb_ref, cosc_ref, sinc_ref, augk_ref,
                 qa_ref, ka_ref, vta_ref, qb_ref, kb_ref, vtb_ref,
                 qc_ref, kc_ref, vtc_ref, qd_ref, kd_ref, vtd_ref):
    xb = x_ref[0].astype(BF16)
    tm = xb.shape[0]
    pm = jnp.dot(xb, wtok_ref[...], preferred_element_type=F32)
    nt = (((1,), (1,)), ((), ()))
    pvt = lax.dot_general(wvt_ref[...], xb, nt, preferred_element_type=F32)

    row = lax.broadcasted_iota(jnp.int32, (BF16_SUBLANES, tm), 0)
    ones_tile = jnp.where(row == 0, 1.0, 0.0).astype(BF16)

    def put_vt(ref, head, vt):
        ref[0, head, 0, 0:HEAD_DIM, :] = vt.astype(BF16)
        ref[0, head, 0, HEAD_DIM:V_ROWS, :] = ones_tile

    qa_ref[0] = (pm[:, _C_AQ:_C_AQ + 256] * (HEAD_DIM ** -0.5)).astype(BF16)
    ka_ref[0] = pm[:, _C_AK:_C_AK + 128].astype(BF16)
    for h in range(WIN_KV_HEADS):
        put_vt(vta_ref, h, pvt[_R_AV + 64 * h:_R_AV + 64 * (h + 1)])

    cosc = cosc_ref[...]
    sinc = sinc_ref[...]
    gavg = gavg_ref[...]

    def head_rms_scale(v):
        hi, lo = _split_hi_lo(v * v)
        ms = (jnp.dot(hi, gavg, preferred_element_type=F32)
              + jnp.dot(lo, gavg, preferred_element_type=F32))
        return lax.rsqrt(ms + NORM_EPS)

    def axial(off, off_rot, g_ref_, gr_ref_, scale):
        v = pm[:, off:off + LANES]
        r = head_rms_scale(v)
        vn = v * r * g_ref_[...]
        vrn = pm[:, off_rot:off_rot + LANES] * r * gr_ref_[...]
        return ((vn * cosc + vrn * sinc) * scale).astype(BF16)

    for grp in range(2):
        qc_ref[0, :, grp * LANES:(grp + 1) * LANES] = axial(
            _C_CQ + grp * LANES, _C_CQR + grp * LANES, axq_ref, axqr_ref, HEAD_DIM ** -0.5 * LOG2E)
    kc_ref[0] = axial(_C_CK, _C_CKR, axk_ref, axkr_ref, 1.0)
    for h in range(AX_KV_HEADS):
        put_vt(vtc_ref, h, pvt[_R_CV + 64 * h:_R_CV + 64 * (h + 1)])

    qd_ref[0] = (pm[:, _C_DQ:_C_DQ + 512] * (DIFF_QK ** -0.5 * LOG2E)).astype(BF16)
    augk = augk_ref[...]
    for h in range(DIFF_HEADS):
        kd_ref[0, :, h * LANES:(h + 1) * LANES] = (
            pm[:, _C_DK + h * LANES:_C_DK + (h + 1) * LANES] + augk).astype(BF16)
        put_vt(vtd_ref, h, pvt[_R_DV + 64 * h:_R_DV + 64 * (h + 1)])

    cosb = cosb_ref[...]
    sinb = sinb_ref[...]
    cq = pm[:, _C_BQ:_C_BQ + MLA_Q_RANK]
    cqn = (cq * lax.rsqrt(jnp.mean(cq * cq, axis=-1, keepdims=True) + NORM_EPS) * gq_ref[...]).astype(BF16)
    qw = jnp.dot(cqn, wuq_ref[...], preferred_element_type=F32)
    qwr = jnp.dot(cqn, wuqr_ref[...], preferred_element_type=F32)
    ckv = pm[:, _C_BKV:_C_BKV + MLA_KV_RANK]
    ckvn = (ckv * lax.rsqrt(jnp.mean(ckv * ckv, axis=-1, keepdims=True) + NORM_EPS) * gkv_ref[...]).astype(BF16)
    kw = jnp.dot(ckvn, wukvk_ref[...], preferred_element_type=F32)
    vbt = lax.dot_general(wukvvt_ref[...], ckvn, nt, preferred_element_type=F32)
    k_rope = pm[:, _C_BKR:_C_BKR + LANES] * cosb + pm[:, _C_BKRR:_C_BKRR + LANES] * sinb
    q_scale = (MLA_NOPE + MLA_ROPE) ** -0.5 * LOG2E
    for h in range(MLA_HEADS):
        sl = slice(h * LANES, (h + 1) * LANES)
        qb_ref[0, :, sl] = ((qw[:, sl] * cosb + qwr[:, sl] * sinb) * q_scale).astype(BF16)
        kb_ref[0, :, sl] = (kw[:, sl] + k_rope).astype(BF16)
        put_vt(vtb_ref, h, vbt[64 * h:64 * (h + 1)])


def _prep(x, p, tabs):
    bsz, seq, d = x.shape
    tm = KV_CHUNK
    nch = seq // tm
    tok = lambda w: pl.BlockSpec((1, tm, w), lambda b, i: (b, i, 0))
    tab = pl.BlockSpec((tm, LANES), lambda b, i: (i, 0))
    vt = lambda nh: pl.BlockSpec((1, nh, 1, V_ROWS, tm), lambda b, i: (b, 0, i, 0, 0))
    vt_shape = lambda nh: jax.ShapeDtypeStruct((bsz, nh, nch, V_ROWS, tm), BF16)
    tok_shape = lambda w: jax.ShapeDtypeStruct((bsz, seq, w), BF16)
    weights = [p["w_tok"], p["w_vt"], p["w_uq"], p["w_uq_rot"], p["w_ukv_k"], p["w_ukv_vt"], p["g_avg"],
               p["g_q"], p["g_kv"], p["ax_q"], p["ax_q_rot"], p["ax_k"], p["ax_k_rot"]]
    return pl.pallas_call(
        _prep_kernel,
        grid=(bsz, nch),
        in_specs=[tok(d)] + [_resident(w.shape) for w in weights] + [tab] * 5,
        out_specs=[tok(256), tok(128), vt(2), tok(512), tok(512), vt(4),
                   tok(256), tok(128), vt(2), tok(512), tok(512), vt(4)],
        out_shape=[tok_shape(256), tok_shape(128), vt_shape(2), tok_shape(512), tok_shape(512), vt_shape(4),
                   tok_shape(256), tok_shape(128), vt_shape(2), tok_shape(512), tok_shape(512), vt_shape(4)],
        compiler_params=_cparams(("parallel", "parallel")),
    )(x, *weights, tabs["cos_b"], tabs["sin_b"], tabs["cos_c"], tabs["sin_c"], tabs["aug_k"])


_NT = (((1,), (1,)), ((), ()))
LOG2E = math.log2(math.e)


def _bf16_parts(x, n):
    parts = []
    for _ in range(n):
        part = float(np.float32(x).astype(jnp.bfloat16).astype(np.float32))
        parts.append(part)
        x -= part
    return tuple(parts)


LOG2E_PARTS = _bf16_parts(LOG2E, 3)


def _key_rows(k_ref, j, slab, cps):
    rows = pl.ds(pl.multiple_of(j * (cps * KV_CHUNK), cps * KV_CHUNK), cps * KV_CHUNK)
    return k_ref[0, rows, slab * LANES:(slab + 1) * LANES]


def _chain_body(k_ref, vt_ref, chains, exp_fn, transform=None, cps=1):
    def body(j, carry):
        scores = []
        for n, (q, k_slab, _) in enumerate(chains):
            s = lax.dot_general(_key_rows(k_ref, j, k_slab, cps), q, _NT, preferred_element_type=F32)
            scores.append(s if transform is None else transform(j, s, n))
        probs = []
        for s, (m, _) in zip(scores, carry):
            m_new = jnp.maximum(m, jnp.max(s, axis=0, keepdims=True))
            probs.append((m_new, exp_fn(m - m_new), exp_fn(s - m_new).astype(BF16)))
        out = []
        for (_, _, v_head), (m_new, alpha, p), (_, acc) in zip(chains, probs, carry):
            pv = None
            for c in range(cps):
                part = jnp.dot(vt_ref[0, v_head, j * cps + c], p[c * KV_CHUNK:(c + 1) * KV_CHUNK],
                               preferred_element_type=F32)
                pv = part if pv is None else pv + part
            out.append((m_new, acc * alpha + pv))
        return tuple(out)
    return body


def _value_matmul(vt_ref, v_head, step, cps, p):
    pv = None
    for c in range(cps):
        part = jnp.dot(vt_ref[0, v_head, step * cps + c], p[c * KV_CHUNK:(c + 1) * KV_CHUNK],
                       preferred_element_type=F32)
        pv = part if pv is None else pv + part
    return pv


def _score_block(k_ref, q, k_slab, row0, s_write, n, r, hooks):
    start = row0 + r * ROW_BLOCK
    if not isinstance(start, int):
        start = pl.multiple_of(start, ROW_BLOCK)
    kc = k_ref[0, pl.ds(start, ROW_BLOCK), k_slab * LANES:(k_slab + 1) * LANES]
    if hooks is not None:
        kc = hooks[0](kc, start // ROW_BLOCK)
    s = lax.dot_general(kc, q, _NT, preferred_element_type=F32)
    if hooks is not None:
        s = hooks[1](s, n, start // ROW_BLOCK)
    s_write[n, r * ROW_BLOCK:(r + 1) * ROW_BLOCK, :] = s
    return jnp.max(s.reshape(ROW_BLOCK // F32_SUBLANES, F32_SUBLANES, s.shape[-1]), axis=0)


def _first_scores(k_ref, chains, cps, s_write, hooks=None):
    out = []
    for n, (q, k_slab, _) in enumerate(chains):
        parts = [_score_block(k_ref, q, k_slab, 0, s_write, n, r, hooks)
                 for r in range(cps * KV_CHUNK // ROW_BLOCK)]
        out.append(functools.reduce(jnp.maximum, parts))
    return out


def _pipelined_step(k_ref, vt_ref, chains, exp_fn, cps, hooks, j, carry, s_read, s_write):
    step_rows = cps * KV_CHUNK
    blocks_per_chunk = KV_CHUNK // ROW_BLOCK
    next_row0 = (j + 1) * step_rows
    out = []
    for n, ((q, k_slab, v_head), (m, acc, smax)) in enumerate(zip(chains, carry)):
        m_new = jnp.maximum(m, jnp.max(smax, axis=0, keepdims=True))
        alpha = exp_fn(m - m_new)
        next_max, p_blocks = smax, []
        for r in range(step_rows // ROW_BLOCK):
            if s_write is not None:
                part = _score_block(k_ref, q, k_slab, next_row0, s_write, n, r, hooks)
                next_max = part if r == 0 else jnp.maximum(next_max, part)
            p_blocks.append(exp_fn(s_read[n, r * ROW_BLOCK:(r + 1) * ROW_BLOCK, :] - m_new).astype(BF16))
        pv = None
        for c in range(cps):
            p = jnp.concatenate(p_blocks[c * blocks_per_chunk:(c + 1) * blocks_per_chunk], axis=0)
            part = jnp.dot(vt_ref[0, v_head, j * cps + c], p, preferred_element_type=F32)
            pv = part if pv is None else pv + part
        out.append((m_new, acc * alpha + pv, next_max))
    return tuple(out)


def _pipelined_attention(k_ref, vt_ref, chains, exp_fn, n_chunks, s0_ref, s1_ref, hooks=None):
    nq = chains[0][0].shape[0]
    cps = s0_ref.shape[1] // KV_CHUNK
    n_steps = n_chunks // cps
    step = functools.partial(_pipelined_step, k_ref, vt_ref, chains, exp_fn, cps, hooks)
    first_max = _first_scores(k_ref, chains, cps, s0_ref, hooks)
    init = tuple((jnp.full((1, nq), NEG_INF, F32), jnp.zeros((V_ROWS, nq), F32), smax) for smax in first_max)

    def pair(t, carry):
        carry = step(2 * t, carry, s0_ref, s1_ref)
        return step(2 * t + 1, carry, s1_ref, s0_ref)

    carry = lax.fori_loop(0, n_steps // 2 - 1, pair, init)
    carry = step(n_steps - 2, carry, s0_ref, s1_ref)
    final = step(n_steps - 1, carry, s1_ref, None)
    return [acc for _, acc, _ in final]


def _normalised(acc):
    return acc[:HEAD_DIM] / acc[HEAD_DIM:HEAD_DIM + 1]


def _single_buffered(block, index_map):
    return pl.BlockSpec(block, index_map, pipeline_mode=pl.Buffered(1))


def _dense_kernel(q_ref, k_ref, vt_ref, o_ref, s0_ref, s1_ref, *, n_chunks, heads):
    tq = q_ref.shape[1]
    lane = lax.broadcasted_iota(jnp.int32, (tq, LANES), 1)
    chains = []
    for q_slab, k_slab, v_head, shared_key_slab in heads:
        q = q_ref[0, :, q_slab * LANES:(q_slab + 1) * LANES]
        if shared_key_slab:
            q = jnp.where((lane >> HEAD_SHIFT) == pl.program_id(1), q.astype(F32), 0.0).astype(BF16)
        chains.append((q, k_slab, v_head))
    accs = _pipelined_attention(k_ref, vt_ref, chains, jnp.exp2, n_chunks, s0_ref, s1_ref)
    for n, acc in enumerate(accs):
        o_ref[0, n * HEAD_DIM:(n + 1) * HEAD_DIM, :] = _normalised(acc).astype(BF16)


def _dense_attention(q, k, vt, *, n_heads, kv_group, heads_per_step):
    bsz, seq, _ = q.shape
    nch = vt.shape[2]
    tq = min(Q_TILE, seq)
    hp = heads_per_step
    if kv_group:
        assert hp == kv_group and n_heads // kv_group == LANES // HEAD_DIM
        heads = tuple((s, 0, 0, True) for s in range(kv_group))
        q_spec = pl.BlockSpec((1, tq, kv_group * LANES), lambda b, g, i: (b, i, 0))
        k_spec = _single_buffered((1, seq, LANES), lambda b, g, i: (b, 0, 0))
        vt_spec = _single_buffered((1, 1, nch, V_ROWS, KV_CHUNK), lambda b, g, i: (b, g, 0, 0, 0))
    else:
        heads = tuple((n, n, n, False) for n in range(hp))
        q_spec = pl.BlockSpec((1, tq, hp * LANES), lambda b, g, i: (b, i, g))
        k_spec = _single_buffered((1, seq, hp * LANES), lambda b, g, i: (b, 0, g))
        vt_spec = _single_buffered((1, hp, nch, V_ROWS, KV_CHUNK), lambda b, g, i: (b, g, 0, 0, 0))
    assert nch % 2 == 0
    cps = math.gcd(CHUNKS_PER_STEP, nch // 2)
    s_buffer = pltpu.VMEM((hp, cps * KV_CHUNK, tq), F32)
    return pl.pallas_call(
        functools.partial(_dense_kernel, n_chunks=nch, heads=heads),
        grid=(bsz, n_heads // hp, seq // tq),
        in_specs=[q_spec, k_spec, vt_spec],
        out_specs=pl.BlockSpec((1, hp * HEAD_DIM, tq), lambda b, g, i: (b, g, i)),
        out_shape=jax.ShapeDtypeStruct((bsz, n_heads * HEAD_DIM, seq), BF16),
        scratch_shapes=[s_buffer, s_buffer],
        compiler_params=_cparams(("parallel", "parallel", "arbitrary"), ATTENTION_FLAGS),
    )(q, k, vt)


_AUG = 2 * DIFF_QK


def _diff_kernel(lam_ref, g_ref, q_ref, k_ref, vt_ref, o_ref, s0_ref, s1_ref, bias_ref, *,
                 n_chunks, hp, lam_init, slope_exp0):
    i = pl.program_id(2)
    tq = q_ref.shape[1]
    lane = lax.broadcasted_iota(jnp.int32, (tq, LANES), 1)
    digit = (lane - _AUG) & 3
    part = (lane - _AUG) >> 2
    is_bias_lane = (lane >= _AUG) & (lane < _AUG + 4 * len(LOG2E_PARTS))
    log2e_part = functools.reduce(lambda acc, x: jnp.where(part == x[0], x[1], acc),
                                  enumerate(LOG2E_PARTS), jnp.zeros((tq, LANES), F32))
    qpos = i * tq + lax.broadcasted_iota(jnp.int32, (tq, LANES), 0)
    q_hi = ((qpos >> 7) << 7).astype(F32)
    q_lo = (qpos & 127).astype(F32)
    own_dist = jnp.abs(lax.broadcasted_iota(jnp.int32, (ROW_BLOCK, tq), 1)
                       - lax.broadcasted_iota(jnp.int32, (ROW_BLOCK, tq), 0)).astype(F32)

    chains = []
    for n in range(hp):
        head = pl.program_id(1) * hp + n
        slope_bits = jnp.full((1, LANES), 127 - slope_exp0, jnp.int32) - head
        slope = lax.bitcast_convert_type(slope_bits << 23, F32)
        qf = q_ref[0, :, n * LANES:(n + 1) * LANES].astype(F32)
        aug = jnp.where(digit == 0, -slope * q_hi, jnp.where(digit == 1, -slope * q_lo, slope * log2e_part))
        aug = jnp.where(is_bias_lane, aug, 0.0)
        q1 = jnp.where(lane < DIFF_QK, qf, 0.0)
        q2 = jnp.where((lane >= DIFF_QK) & (lane < 2 * DIFF_QK), qf, 0.0)
        chains.append((jnp.concatenate([q1 + aug, q2 + aug], axis=0).astype(BF16), n, n))
        bias = (-LOG2E) * slope[:, :1] * own_dist
        bias_ref[n, 0] = jnp.zeros((ROW_BLOCK, 2 * tq), F32)
        bias_ref[n, 1] = jnp.concatenate([bias, bias], axis=1)

    key_lane = lax.broadcasted_iota(jnp.int32, (1, LANES), 1)
    key_bias_lane = (key_lane >= _AUG) & (key_lane < _AUG + 4 * len(LOG2E_PARTS))

    def edit_keys(kc, block):
        sign = jnp.where(block < i, 1.0, jnp.where(block > i, -1.0, 0.0))
        return kc * jnp.where(key_bias_lane, sign, 1.0).astype(BF16)

    def add_bias(s, n, block):
        return s + bias_ref[n, jnp.where(block == i, 1, 0)]

    accs = _pipelined_attention(k_ref, vt_ref, chains, jnp.exp2, n_chunks, s0_ref, s1_ref, (edit_keys, add_bias))

    lp = lam_ref[...]
    lam = (jnp.exp(jnp.sum(lp[0:1] * lp[1:2], axis=-1, keepdims=True))
           - jnp.exp(jnp.sum(lp[2:3] * lp[3:4], axis=-1, keepdims=True)) + lam_init)
    for n, acc in enumerate(accs):
        o = _normalised(acc[:, :tq]) - lam * _normalised(acc[:, tq:])
        o = o * lax.rsqrt(jnp.mean(o * o, axis=0, keepdims=True) + NORM_EPS)
        o_ref[0, n * DIFF_V:(n + 1) * DIFF_V, :] = (o * g_ref[...] * (1.0 - lam_init)).astype(BF16)


def _diff_attention(q, k, vt, lam_params, subln_col, *, layer_idx, slope_exp0, heads_per_step):
    bsz, seq, _ = q.shape
    nch = vt.shape[2]
    tq = ROW_BLOCK
    hp = heads_per_step
    assert seq % tq == 0 and seq <= 256 * LANES and DIFF_HEADS % hp == 0 and nch % 2 == 0
    lam_init = 0.8 - 0.6 * math.exp(-0.3 * layer_idx)
    cps = math.gcd(CHUNKS_PER_STEP, nch // 2)
    s_buffer = pltpu.VMEM((hp, cps * KV_CHUNK, 2 * tq), F32)
    return pl.pallas_call(
        functools.partial(_diff_kernel, n_chunks=nch, hp=hp, lam_init=lam_init, slope_exp0=slope_exp0),
        grid=(bsz, DIFF_HEADS // hp, seq // tq),
        in_specs=[
            _resident(lam_params.shape),
            _resident(subln_col.shape),
            pl.BlockSpec((1, tq, hp * LANES), lambda b, g, i: (b, i, g)),
            _single_buffered((1, seq, hp * LANES), lambda b, g, i: (b, 0, g)),
            _single_buffered((1, hp, nch, V_ROWS, KV_CHUNK), lambda b, g, i: (b, g, 0, 0, 0)),
        ],
        out_specs=pl.BlockSpec((1, hp * DIFF_V, tq), lambda b, g, i: (b, g, i)),
        out_shape=jax.ShapeDtypeStruct((bsz, DIFF_HEADS * DIFF_V, seq), BF16),
        scratch_shapes=[s_buffer, s_buffer, pltpu.VMEM((hp, 2, ROW_BLOCK, 2 * tq), F32)],
        compiler_params=_cparams(("parallel", "parallel", "arbitrary"), ATTENTION_FLAGS),
    )(lam_params, subln_col, q, k, vt)


def _window_kernel(sink_ref, q_ref, k_ref, vt_ref, o_ref, *, seq):
    i = pl.program_id(1)
    tq = q_ref.shape[1]
    q0 = i * tq
    j_lo = jnp.maximum(q0 - WINDOW, 0) // KV_CHUNK
    j_hi = jnp.minimum(q0 + tq - 1 + WINDOW, seq - 1) // KV_CHUNK
    lane = lax.broadcasted_iota(jnp.int32, (tq, LANES), 1)
    acc_row = lax.broadcasted_iota(jnp.int32, (V_ROWS, tq), 0)
    tpos = q0 + lax.broadcasted_iota(jnp.int32, (KV_CHUNK, tq), 1)
    krow = lax.broadcasted_iota(jnp.int32, (KV_CHUNK, tq), 0)
    group = WIN_HEADS // WIN_KV_HEADS
    chains, init = [], []
    for head in range(WIN_HEADS):
        kv_head, slab = head // group, head % group
        qf = q_ref[0, :, slab * LANES:(slab + 1) * LANES].astype(F32)
        chains.append((jnp.where((lane >> HEAD_SHIFT) == kv_head, qf, 0.0).astype(BF16), 0, kv_head))
        init.append((jnp.full((1, tq), sink_ref[head], F32), jnp.where(acc_row == HEAD_DIM, 1.0, 0.0)))

    def band(j, s, n):
        dist = jnp.abs(tpos - (j * KV_CHUNK + krow))
        return jnp.where(dist <= WINDOW, s - (2.0 ** -(n + 1)) * dist.astype(F32), NEG_INF)

    final = lax.fori_loop(j_lo, j_hi + 1, _chain_body(k_ref, vt_ref, chains, jnp.exp, band), tuple(init))
    for n, (_, acc) in enumerate(final):
        o_ref[0, n * HEAD_DIM:(n + 1) * HEAD_DIM, :] = _normalised(acc).astype(BF16)


def _window_attention(q, k, vt, sink):
    bsz, seq, _ = q.shape
    nch = vt.shape[2]
    tq = min(Q_TILE, seq)
    return pl.pallas_call(
        functools.partial(_window_kernel, seq=seq),
        grid=(bsz, seq // tq),
        in_specs=[
            pl.BlockSpec(memory_space=pltpu.SMEM),
            pl.BlockSpec((1, tq, 2 * LANES), lambda b, i: (b, i, 0)),
            _single_buffered((1, seq, LANES), lambda b, i: (b, 0, 0)),
            _single_buffered((1, WIN_KV_HEADS, nch, V_ROWS, KV_CHUNK), lambda b, i: (b, 0, 0, 0, 0)),
        ],
        out_specs=pl.BlockSpec((1, WIN_HEADS * HEAD_DIM, tq), lambda b, i: (b, 0, i)),
        out_shape=jax.ShapeDtypeStruct((bsz, WIN_HEADS * HEAD_DIM, seq), BF16),
        compiler_params=_cparams(("parallel", "arbitrary")),
    )(sink, q, k, vt)


def _outproj_kernel(x_ref, oa_ref, ob_ref, oc_ref, od_ref, wt_ref, g_ref, b_ref, o_ref, *, alpha):
    mix_t = None
    for m, ref in enumerate((oa_ref, ob_ref, oc_ref, od_ref)):
        part = jnp.dot(wt_ref[:, m * 256:(m + 1) * 256], ref[0], preferred_element_type=F32)
        mix_t = part if mix_t is None else mix_t + part
    y = alpha * x_ref[0] + mix_t.T
    o_ref[0] = _layer_norm_rows(y, g_ref[...], b_ref[...])


def _outproj(x, o_a, o_b, o_c, o_d, w_out_t, g, b, alpha):
    bsz, seq, d = x.shape
    tm = min(TOK_TILE, seq)
    ot = pl.BlockSpec((1, 256, tm), lambda bb, i: (bb, 0, i))
    tok = pl.BlockSpec((1, tm, d), lambda bb, i: (bb, i, 0))
    return pl.pallas_call(
        functools.partial(_outproj_kernel, alpha=alpha),
        grid=(bsz, seq // tm),
        in_specs=[tok, ot, ot, ot, ot, _resident(w_out_t.shape), _resident(g.shape), _resident(b.shape)],
        out_specs=tok,
        out_shape=jax.ShapeDtypeStruct((bsz, seq, d), F32),
        compiler_params=_cparams(("parallel", "parallel")),
    )(x, o_a, o_b, o_c, o_d, w_out_t, g, b)


def _rot_half_cols(start, width):
    half = width // 2
    src = np.concatenate([np.arange(start + half, start + width), np.arange(start, start + half)])
    sgn = np.concatenate([-np.ones(half), np.ones(half)])
    return src, sgn


def _gqa_slab_order(n_heads, n_kv):
    group = n_heads // n_kv
    return [kv * group + s for s in range(group) for kv in range(n_kv)]


def _layer_params(w_in, mla_q_norm, mla_w_uq, mla_kv_norm, mla_w_ukv, ax_q_norm, ax_k_norm):
    d = w_in.shape[0]
    zeros = lambda n: jnp.zeros((d, n), w_in.dtype)
    a0, b0, c0, d0 = 0, A_COLS, A_COLS + B_COLS, A_COLS + B_COLS + C_COLS
    head_cols = lambda base, h: np.arange(base + h * HEAD_DIM, base + (h + 1) * HEAD_DIM)

    def axial_rot(cols):
        src, sgn = [], []
        for blk in range(0, HEAD_DIM, HEAD_DIM // 2):
            s_, g_ = _rot_half_cols(blk, HEAD_DIM // 2)
            src.append(cols[s_])
            sgn.append(g_)
        return np.concatenate(src), np.concatenate(sgn)

    order = _gqa_slab_order(AX_HEADS, AX_KV_HEADS)
    aq = np.concatenate([head_cols(a0, h) for h in order])
    ak = np.arange(a0 + 256, a0 + 384)
    cq = np.concatenate([head_cols(c0, h) for h in order])
    ck = np.arange(c0 + 256, c0 + 384)
    cq_rot = [axial_rot(head_cols(c0, h)) for h in order]
    ck_rot = [axial_rot(head_cols(c0 + 256, h)) for h in range(AX_KV_HEADS)]

    def gather(src, sgn=None):
        w = w_in[:, np.asarray(src)]
        return w if sgn is None else w * jnp.asarray(sgn, w.dtype)[None, :]

    def widen(cols):
        return jnp.concatenate([gather(cols), zeros(LANES - len(cols))], axis=1)

    kr = np.arange(b0 + MLA_Q_RANK + MLA_KV_RANK, b0 + B_COLS)
    kr_src, kr_sgn = _rot_half_cols(kr[0], MLA_ROPE)
    place_rope = lambda w: jnp.concatenate([zeros(MLA_NOPE), w, zeros(LANES - MLA_NOPE - MLA_ROPE)], axis=1)

    w_tok = jnp.concatenate(
        [gather(aq), gather(ak), gather(cq), gather(ck)]
        + [gather(s, g) for s, g in cq_rot] + [gather(s, g) for s, g in ck_rot]
        + [widen(head_cols(d0, h)) for h in range(DIFF_HEADS)]
        + [widen(head_cols(d0 + 256, h)) for h in range(DIFF_HEADS)]
        + [gather(np.arange(b0, b0 + MLA_Q_RANK + MLA_KV_RANK)),
           place_rope(gather(kr)), place_rope(gather(kr_src, kr_sgn))], axis=1)
    assert w_tok.shape[1] == _C_END
    v_cols = np.concatenate([np.arange(a0 + 384, a0 + 512), np.arange(c0 + 384, c0 + 512),
                             np.arange(d0 + 512, d0 + 768)])
    w_vt = w_in[:, v_cols].T

    qd = MLA_NOPE + MLA_ROPE
    zq = lambda n: jnp.zeros((MLA_Q_RANK, n), mla_w_uq.dtype)
    uq, uq_rot = [], []
    for h in range(MLA_HEADS):
        blk = mla_w_uq[:, h * qd:(h + 1) * qd]
        src, sgn = _rot_half_cols(MLA_NOPE, MLA_ROPE)
        rot = blk[:, src] * jnp.asarray(sgn, blk.dtype)[None, :]
        uq += [blk, zq(LANES - qd)]
        uq_rot += [zq(MLA_NOPE), rot, zq(LANES - qd)]
    ukv = mla_w_ukv.reshape(MLA_KV_RANK, MLA_HEADS, MLA_NOPE + MLA_V)
    ukv_k = jnp.concatenate([ukv[:, :, :MLA_NOPE], jnp.zeros((MLA_KV_RANK, MLA_HEADS, LANES - MLA_NOPE), ukv.dtype)],
                            axis=2).reshape(MLA_KV_RANK, MLA_HEADS * LANES)
    ukv_vt = ukv[:, :, MLA_NOPE:].reshape(MLA_KV_RANK, MLA_HEADS * MLA_V).T

    def ax_gain(g):
        src, sgn = axial_rot(np.arange(HEAD_DIM))
        return jnp.tile(g, 2)[None, :].astype(F32), jnp.tile(g[src], 2)[None, :].astype(F32)

    ax_q, ax_q_rot = ax_gain(ax_q_norm)
    ax_k, ax_k_rot = ax_gain(ax_k_norm)
    lane = np.arange(LANES)
    g_avg = (lane[:, None] // HEAD_DIM == lane[None, :] // HEAD_DIM).astype(np.float32) / HEAD_DIM
    return {
        "w_tok": w_tok.astype(BF16), "w_vt": w_vt.astype(BF16),
        "w_uq": jnp.concatenate(uq, axis=1).astype(BF16), "w_uq_rot": jnp.concatenate(uq_rot, axis=1).astype(BF16),
        "w_ukv_k": ukv_k.astype(BF16), "w_ukv_vt": ukv_vt.astype(BF16),
        "g_avg": jnp.asarray(g_avg, BF16),
        "g_q": mla_q_norm[None, :].astype(F32), "g_kv": mla_kv_norm[None, :].astype(F32),
        "ax_q": ax_q, "ax_q_rot": ax_q_rot, "ax_k": ax_k, "ax_k_rot": ax_k_rot,
    }


def _position_tables(seq):
    pos = jnp.arange(seq, dtype=jnp.int32)
    half = HEAD_DIM // 2

    def angles(p, dim):
        inv = ROPE_BASE ** (-jnp.arange(0, dim, 2, dtype=F32) / dim)
        ang = p.astype(F32)[:, None] * inv[None, :]
        return jnp.cos(ang), jnp.sin(ang)

    cb, sb = angles(pos, MLA_ROPE)
    pad = LANES - MLA_NOPE - MLA_ROPE
    cos_b = jnp.concatenate([jnp.ones((seq, MLA_NOPE), F32), cb, cb, jnp.zeros((seq, pad), F32)], axis=1)
    sin_b = jnp.concatenate([jnp.zeros((seq, MLA_NOPE), F32), sb, sb, jnp.zeros((seq, pad), F32)], axis=1)
    cr, sr = angles(pos // GRID_W, half)
    cc, sc = angles(pos % GRID_W, half)
    cos_c = jnp.tile(jnp.concatenate([cr, cr, cc, cc], axis=1), (1, 2))
    sin_c = jnp.tile(jnp.concatenate([sr, sr, sc, sc], axis=1), (1, 2))
    aug = jnp.zeros((seq, LANES), F32)
    for x, part in enumerate(LOG2E_PARTS):
        c = _AUG + 4 * x
        aug = aug.at[:, c].set(part).at[:, c + 1].set(part)
        aug = aug.at[:, c + 2].set(((pos >> 7) << 7).astype(F32)).at[:, c + 3].set((pos & 127).astype(F32))
    return {"cos_b": cos_b, "sin_b": sin_b, "cos_c": cos_c, "sin_c": sin_c, "aug_k": aug}


def _ffn_weights(w_gu, w_down):
    d, two_ff = w_gu.shape
    d_ff = two_ff // 2
    n_chunks = d_ff // FF_CHUNK
    wgu = w_gu.astype(BF16).reshape(d, 2, n_chunks, FF_CHUNK).transpose(2, 0, 1, 3).reshape(n_chunks, d, 2 * FF_CHUNK)
    wd = w_down.astype(BF16).reshape(n_chunks, FF_CHUNK, d)
    return wgu, wd


def kernel(x, w_in, win_sink, mla_q_norm, mla_w_uq, mla_kv_norm, mla_w_ukv, ax_q_norm, ax_k_norm,
           diff_lambda, diff_subln, w_out, ffn_w_gu, ffn_w_down, ln_g, ln_b):
    depth = w_in.shape[0]
    bsz, seq, d = x.shape
    alpha = (2 * depth) ** 0.25
    tabs = _position_tables(seq)
    row = lambda v: v[None, :].astype(F32)
    for l in range(depth):
        wgu, wd = _ffn_weights(ffn_w_gu[l, 0], ffn_w_down[l, 0])
        x = _ffn(x.reshape(bsz * seq, d), wgu, wd, row(ln_g[l, 0]), row(ln_b[l, 0]), alpha).reshape(bsz, seq, d)

        p = _layer_params(w_in[l], mla_q_norm[l], mla_w_uq[l], mla_kv_norm[l], mla_w_ukv[l],
                          ax_q_norm[l], ax_k_norm[l])
        qa, ka, vta, qb, kb, vtb, qc, kc, vtc, qd, kd, vtd = _prep(x, p, tabs)
        o_a = _window_attention(qa, ka, vta, win_sink[l].astype(F32))
        o_b = _dense_attention(qb, kb, vtb, n_heads=MLA_HEADS, kv_group=0, heads_per_step=HEADS_PER_STEP)
        o_c = _dense_attention(qc, kc, vtc, n_heads=AX_HEADS, kv_group=AX_HEADS // AX_KV_HEADS,
                               heads_per_step=AX_HEADS // AX_KV_HEADS)
        o_d = _diff_attention(qd, kd, vtd, diff_lambda[l].astype(F32), diff_subln[l][:, None].astype(F32),
                              layer_idx=l, slope_exp0=WIN_HEADS + 1, heads_per_step=HEADS_PER_STEP)
        x = _outproj(x, o_a, o_b, o_c, o_d, w_out[l].T.astype(BF16), row(ln_g[l, 1]), row(ln_b[l, 1]), alpha)

        wgu, wd = _ffn_weights(ffn_w_gu[l, 1], ffn_w_down[l, 1])
        x = _ffn(x.reshape(bsz * seq, d), wgu, wd, row(ln_g[l, 2]), row(ln_b[l, 2]), alpha).reshape(bsz, seq, d)
    return x
```

```python
import functools
import math

import numpy as np
import jax
import jax.numpy as jnp
from jax import lax
from jax.experimental import pallas as pl
from jax.experimental.pallas import tpu as pltpu

F32 = jnp.float32
BF16 = jnp.bfloat16

HEAD_DIM = 64
HEAD_SHIFT = 6
GRID_W = 64
WIN_HEADS, WIN_KV_HEADS, WINDOW = 4, 2, 128
MLA_HEADS, MLA_Q_RANK, MLA_KV_RANK, MLA_NOPE, MLA_ROPE, MLA_V = 4, 256, 128, 64, 32, 64
AX_HEADS, AX_KV_HEADS = 4, 2
DIFF_HEADS, DIFF_QK, DIFF_V = 4, 32, 64
A_COLS = (WIN_HEADS + 2 * WIN_KV_HEADS) * HEAD_DIM
B_COLS = MLA_Q_RANK + MLA_KV_RANK + MLA_ROPE
C_COLS = (AX_HEADS + 2 * AX_KV_HEADS) * HEAD_DIM
D_COLS = DIFF_HEADS * (4 * DIFF_QK + DIFF_V)
ROPE_BASE = 10000.0
NORM_EPS = 1e-5
NEG_INF = -1e30

LANES = 128
F32_SUBLANES = 8
BF16_SUBLANES = 16
V_ROWS = HEAD_DIM + BF16_SUBLANES
VMEM_LIMIT = 56 * 1024 * 1024

TOK_TILE = 512
KV_CHUNK = 512
Q_TILE = 512
FF_CHUNK = 256
HEADS_PER_STEP = 2
CHUNKS_PER_STEP = 4
ROW_BLOCK = 256


def _cparams(sem, flags=None):
    return pltpu.CompilerParams(dimension_semantics=sem, vmem_limit_bytes=VMEM_LIMIT, flags=flags)


ATTENTION_FLAGS = None


def _resident(shape):
    nd = len(shape)
    return pl.BlockSpec(shape, lambda *_: (0,) * nd, pipeline_mode=pl.Buffered(1))


def _layer_norm_rows(y, g, b):
    mu = jnp.mean(y, axis=-1, keepdims=True)
    yc = y - mu
    var = jnp.mean(yc * yc, axis=-1, keepdims=True)
    return yc * lax.rsqrt(var + NORM_EPS) * g + b


def _ffn_kernel(x_ref, wgu_ref, wd_ref, g_ref, b_ref, o_ref, acc_ref, *, alpha, n_chunks):
    x = x_ref[...]
    xb = x.astype(BF16)
    acc_ref[...] = jnp.zeros_like(acc_ref)

    def body(c, carry):
        gu = jnp.dot(xb, wgu_ref[c], preferred_element_type=F32)
        gate = gu[:, :FF_CHUNK]
        up = gu[:, FF_CHUNK:]
        h = (gate * jax.nn.sigmoid(gate) * up).astype(BF16)
        acc_ref[...] += jnp.dot(h, wd_ref[c], preferred_element_type=F32)
        return carry

    lax.fori_loop(0, n_chunks, body, 0)
    y = alpha * x + 0.5 * acc_ref[...]
    o_ref[...] = _layer_norm_rows(y, g_ref[...], b_ref[...])


def _ffn(x2d, wgu, wd, g, b, alpha):
    n_tok, d = x2d.shape
    n_chunks = wgu.shape[0]
    tm = min(TOK_TILE, n_tok)
    return pl.pallas_call(
        functools.partial(_ffn_kernel, alpha=alpha, n_chunks=n_chunks),
        grid=(n_tok // tm,),
        in_specs=[
            pl.BlockSpec((tm, d), lambda i: (i, 0)),
            _resident(wgu.shape),
            _resident(wd.shape),
            _resident(g.shape),
            _resident(b.shape),
        ],
        out_specs=pl.BlockSpec((tm, d), lambda i: (i, 0)),
        out_shape=jax.ShapeDtypeStruct((n_tok, d), F32),
        scratch_shapes=[pltpu.VMEM((tm, d), F32)],
        compiler_params=_cparams(("parallel",)),
    )(x2d, wgu, wd, g, b)


_C_AQ, _C_AK, _C_CQ, _C_CK, _C_CQR, _C_CKR = 0, 256, 384, 640, 768, 1024
_C_DQ, _C_DK, _C_BQ, _C_BKV, _C_BKR, _C_BKRR, _C_END = 1152, 1664, 2176, 2432, 2560, 2688, 2816
_R_AV, _R_CV, _R_DV, _R_END = 0, 128, 256, 512


def _split_hi_lo(v):
    hi = v.astype(BF16)
    lo = (v - hi.astype(F32)).astype(BF16)
    return hi, lo


def _prep_kernel(x_ref, wtok_ref, wvt_ref, wuq_ref, wuqr_ref, wukvk_ref, wukvvt_ref, gavg_ref,
                 gq_ref, gkv_ref, axq_ref, axqr_ref, axk_ref, axkr_ref,
                 cosb_ref, sinb_ref, cosc_ref, sinc_ref, augk_ref,
                 qa_ref, ka_ref, vta_ref, qb_ref, kb_ref, vtb_ref,
                 qc_ref, kc_ref, vtc_ref, qd_ref, kd_ref, vtd_ref):
    xb = x_ref[0].astype(BF16)
    tm = xb.shape[0]
    pm = jnp.dot(xb, wtok_ref[...], preferred_element_type=F32)
    nt = (((1,), (1,)), ((), ()))
    pvt = lax.dot_general(wvt_ref[...], xb, nt, preferred_element_type=F32)

    row = lax.broadcasted_iota(jnp.int32, (BF16_SUBLANES, tm), 0)
    ones_tile = jnp.where(row == 0, 1.0, 0.0).astype(BF16)

    def put_vt(ref, head, vt):
        ref[0, head, 0, 0:HEAD_DIM, :] = vt.astype(BF16)
        ref[0, head, 0, HEAD_DIM:V_ROWS, :] = ones_tile

    qa_ref[0] = (pm[:, _C_AQ:_C_AQ + 256] * (HEAD_DIM ** -0.5)).astype(BF16)
    ka_ref[0] = pm[:, _C_AK:_C_AK + 128].astype(BF16)
    for h in range(WIN_KV_HEADS):
        put_vt(vta_ref, h, pvt[_R_AV + 64 * h:_R_AV + 64 * (h + 1)])

    cosc = cosc_ref[...]
    sinc = sinc_ref[...]
    gavg = gavg_ref[...]

    def head_rms_scale(v):
        hi, lo = _split_hi_lo(v * v)
        ms = (jnp.dot(hi, gavg, preferred_element_type=F32)
              + jnp.dot(lo, gavg, preferred_element_type=F32))
        return lax.rsqrt(ms + NORM_EPS)

    def axial(off, off_rot, g_ref_, gr_ref_, scale):
        v = pm[:, off:off + LANES]
        r = head_rms_scale(v)
        vn = v * r * g_ref_[...]
        vrn = pm[:, off_rot:off_rot + LANES] * r * gr_ref_[...]
        return ((vn * cosc + vrn * sinc) * scale).astype(BF16)

    for grp in range(2):
        qc_ref[0, :, grp * LANES:(grp + 1) * LANES] = axial(
            _C_CQ + grp * LANES, _C_CQR + grp * LANES, axq_ref, axqr_ref, HEAD_DIM ** -0.5 * LOG2E)
    kc_ref[0] = axial(_C_CK, _C_CKR, axk_ref, axkr_ref, 1.0)
    for h in range(AX_KV_HEADS):
        put_vt(vtc_ref, h, pvt[_R_CV + 64 * h:_R_CV + 64 * (h + 1)])

    qd_ref[0] = (pm[:, _C_DQ:_C_DQ + 512] * (DIFF_QK ** -0.5 * LOG2E)).astype(BF16)
    augk = augk_ref[...]
    for h in range(DIFF_HEADS):
        kd_ref[0, :, h * LANES:(h + 1) * LANES] = (
            pm[:, _C_DK + h * LANES:_C_DK + (h + 1) * LANES] + augk).astype(BF16)
        put_vt(vtd_ref, h, pvt[_R_DV + 64 * h:_R_DV + 64 * (h + 1)])

    cosb = cosb_ref[...]
    sinb = sinb_ref[...]
    cq = pm[:, _C_BQ:_C_BQ + MLA_Q_RANK]
    cqn = (cq * lax.rsqrt(jnp.mean(cq * cq, axis=-1, keepdims=True) + NORM_EPS) * gq_ref[...]).astype(BF16)
    qw = jnp.dot(cqn, wuq_ref[...], preferred_element_type=F32)
    qwr = jnp.dot(cqn, wuqr_ref[...], preferred_element_type=F32)
    ckv = pm[:, _C_BKV:_C_BKV + MLA_KV_RANK]
    ckvn = (ckv * lax.rsqrt(jnp.mean(ckv * ckv, axis=-1, keepdims=True) + NORM_EPS) * gkv_ref[...]).astype(BF16)
    kw = jnp.dot(ckvn, wukvk_ref[...], preferred_element_type=F32)
    vbt = lax.dot_general(wukvvt_ref[...], ckvn, nt, preferred_element_type=F32)
    k_rope = pm[:, _C_BKR:_C_BKR + LANES] * cosb + pm[:, _C_BKRR:_C_BKRR + LANES] * sinb
    q_scale = (MLA_NOPE + MLA_ROPE) ** -0.5 * LOG2E
    for h in range(MLA_HEADS):
        sl = slice(h * LANES, (h + 1) * LANES)
        qb_ref[0, :, sl] = ((qw[:, sl] * cosb + qwr[:, sl] * sinb) * q_scale).astype(BF16)
        kb_ref[0, :, sl] = (kw[:, sl] + k_rope).astype(BF16)
        put_vt(vtb_ref, h, vbt[64 * h:64 * (h + 1)])


def _prep(x, p, tabs):
    bsz, seq, d = x.shape
    tm = KV_CHUNK
    nch = seq // tm
    tok = lambda w: pl.BlockSpec((1, tm, w), lambda b, i: (b, i, 0))
    tab = pl.BlockSpec((tm, LANES), lambda b, i: (i, 0))
    vt = lambda nh: pl.BlockSpec((1, nh, 1, V_ROWS, tm), lambda b, i: (b, 0, i, 0, 0))
    vt_shape = lambda nh: jax.ShapeDtypeStruct((bsz, nh, nch, V_ROWS, tm), BF16)
    tok_shape = lambda w: jax.ShapeDtypeStruct((bsz, seq, w), BF16)
    weights = [p["w_tok"], p["w_vt"], p["w_uq"], p["w_uq_rot"], p["w_ukv_k"], p["w_ukv_vt"], p["g_avg"],
               p["g_q"], p["g_kv"], p["ax_q"], p["ax_q_rot"], p["ax_k"], p["ax_k_rot"]]
    return pl.pallas_call(
        _prep_kernel,
        grid=(bsz, nch),
        in_specs=[tok(d)] + [_resident(w.shape) for w in weights] + [tab] * 5,
        out_specs=[tok(256), tok(128), vt(2), tok(512), tok(512), vt(4),
                   tok(256), tok(128), vt(2), tok(512), tok(512), vt(4)],
        out_shape=[tok_shape(256), tok_shape(128), vt_shape(2), tok_shape(512), tok_shape(512), vt_shape(4),
                   tok_shape(256), tok_shape(128), vt_shape(2), tok_shape(512), tok_shape(512), vt_shape(4)],
        compiler_params=_cparams(("parallel", "parallel")),
    )(x, *weights, tabs["cos_b"], tabs["sin_b"], tabs["cos_c"], tabs["sin_c"], tabs["aug_k"])


_NT = (((1,), (1,)), ((), ()))
LOG2E = math.log2(math.e)


def _bf16_parts(x, n):
    parts = []
    for _ in range(n):
        part = float(np.float32(x).astype(jnp.bfloat16).astype(np.float32))
        parts.append(part)
        x -= part
    return tuple(parts)


LOG2E_PARTS = _bf16_parts(LOG2E, 3)


def _key_rows(k_ref, j, slab, cps):
    rows = pl.ds(pl.multiple_of(j * (cps * KV_CHUNK), cps * KV_CHUNK), cps * KV_CHUNK)
    return k_ref[0, rows, slab * LANES:(slab + 1) * LANES]


def _chain_body(k_ref, vt_ref, chains, exp_fn, transform=None, cps=1):
    def body(j, carry):
        scores = []
        for n, (q, k_slab, _) in enumerate(chains):
            s = lax.dot_general(_key_rows(k_ref, j, k_slab, cps), q, _NT, preferred_element_type=F32)
            scores.append(s if transform is None else transform(j, s, n))
        probs = []
        for s, (m, _) in zip(scores, carry):
            m_new = jnp.maximum(m, jnp.max(s, axis=0, keepdims=True))
            probs.append((m_new, exp_fn(m - m_new), exp_fn(s - m_new).astype(BF16)))
        out = []
        for (_, _, v_head), (m_new, alpha, p), (_, acc) in zip(chains, probs, carry):
            pv = None
            for c in range(cps):
                part = jnp.dot(vt_ref[0, v_head, j * cps + c], p[c * KV_CHUNK:(c + 1) * KV_CHUNK],
                               preferred_element_type=F32)
                pv = part if pv is None else pv + part
            out.append((m_new, acc * alpha + pv))
        return tuple(out)
    return body


def _value_matmul(vt_ref, v_head, step, cps, p):
    pv = None
    for c in range(cps):
        part = jnp.dot(vt_ref[0, v_head, step * cps + c], p[c * KV_CHUNK:(c + 1) * KV_CHUNK],
                       preferred_element_type=F32)
        pv = part if pv is None else pv + part
    return pv


def _score_block(k_ref, q, k_slab, row0, s_write, n, r, hooks):
    start = row0 + r * ROW_BLOCK
    if not isinstance(start, int):
        start = pl.multiple_of(start, ROW_BLOCK)
    kc = k_ref[0, pl.ds(start, ROW_BLOCK), k_slab * LANES:(k_slab + 1) * LANES]
    if hooks is not None:
        kc = hooks[0](kc, start // ROW_BLOCK)
    s = lax.dot_general(kc, q, _NT, preferred_element_type=F32)
    if hooks is not None:
        s = hooks[1](s, n, start // ROW_BLOCK)
    s_write[n, r * ROW_BLOCK:(r + 1) * ROW_BLOCK, :] = s
    return jnp.max(s.reshape(ROW_BLOCK // F32_SUBLANES, F32_SUBLANES, s.shape[-1]), axis=0)


def _first_scores(k_ref, chains, cps, s_write, hooks=None):
    out = []
    for n, (q, k_slab, _) in enumerate(chains):
        parts = [_score_block(k_ref, q, k_slab, 0, s_write, n, r, hooks)
                 for r in range(cps * KV_CHUNK // ROW_BLOCK)]
        out.append(functools.reduce(jnp.maximum, parts))
    return out


def _pipelined_step(k_ref, vt_ref, chains, exp_fn, cps, hooks, j, carry, s_read, s_write):
    step_rows = cps * KV_CHUNK
    blocks_per_chunk = KV_CHUNK // ROW_BLOCK
    next_row0 = (j + 1) * step_rows
    out = []
    for n, ((q, k_slab, v_head), (m, acc, smax)) in enumerate(zip(chains, carry)):
        m_new = jnp.maximum(m, jnp.max(smax, axis=0, keepdims=True))
        alpha = exp_fn(m - m_new)
        next_max, pv = smax, None
        for r in range(step_rows // ROW_BLOCK):
            if s_write is not None:
                part = _score_block(k_ref, q, k_slab, next_row0, s_write, n, r, hooks)
                next_max = part if r == 0 else jnp.maximum(next_max, part)
            p = exp_fn(s_read[n, r * ROW_BLOCK:(r + 1) * ROW_BLOCK, :] - m_new).astype(BF16)
            c, b = divmod(r, blocks_per_chunk)
            vt = vt_ref[0, v_head, j * cps + c, :, b * ROW_BLOCK:(b + 1) * ROW_BLOCK]
            part = jnp.dot(vt, p, preferred_element_type=F32)
            pv = part if pv is None else pv + part
        out.append((m_new, acc * alpha + pv, next_max))
    return tuple(out)


def _pipelined_attention(k_ref, vt_ref, chains, exp_fn, n_chunks, s0_ref, s1_ref, hooks=None):
    nq = chains[0][0].shape[0]
    cps = s0_ref.shape[1] // KV_CHUNK
    n_steps = n_chunks // cps
    step = functools.partial(_pipelined_step, k_ref, vt_ref, chains, exp_fn, cps, hooks)
    first_max = _first_scores(k_ref, chains, cps, s0_ref, hooks)
    init = tuple((jnp.full((1, nq), NEG_INF, F32), jnp.zeros((V_ROWS, nq), F32), smax) for smax in first_max)

    def pair(t, carry):
        carry = step(2 * t, carry, s0_ref, s1_ref)
        return step(2 * t + 1, carry, s1_ref, s0_ref)

    carry = lax.fori_loop(0, n_steps // 2 - 1, pair, init)
    carry = step(n_steps - 2, carry, s0_ref, s1_ref)
    final = step(n_steps - 1, carry, s1_ref, None)
    return [acc for _, acc, _ in final]


def _normalised(acc):
    return acc[:HEAD_DIM] / acc[HEAD_DIM:HEAD_DIM + 1]


def _single_buffered(block, index_map):
    return pl.BlockSpec(block, index_map, pipeline_mode=pl.Buffered(1))


def _dense_kernel(q_ref, k_ref, vt_ref, o_ref, s0_ref, s1_ref, *, n_chunks, heads):
    tq = q_ref.shape[1]
    lane = lax.broadcasted_iota(jnp.int32, (tq, LANES), 1)
    chains = []
    for q_slab, k_slab, v_head, shared_key_slab in heads:
        q = q_ref[0, :, q_slab * LANES:(q_slab + 1) * LANES]
        if shared_key_slab:
            q = jnp.where((lane >> HEAD_SHIFT) == pl.program_id(1), q.astype(F32), 0.0).astype(BF16)
        chains.append((q, k_slab, v_head))
    accs = _pipelined_attention(k_ref, vt_ref, chains, jnp.exp2, n_chunks, s0_ref, s1_ref)
    for n, acc in enumerate(accs):
        o_ref[0, n * HEAD_DIM:(n + 1) * HEAD_DIM, :] = _normalised(acc).astype(BF16)


def _dense_attention(q, k, vt, *, n_heads, kv_group, heads_per_step):
    bsz, seq, _ = q.shape
    nch = vt.shape[2]
    tq = min(Q_TILE, seq)
    hp = heads_per_step
    if kv_group:
        assert hp == kv_group and n_heads // kv_group == LANES // HEAD_DIM
        heads = tuple((s, 0, 0, True) for s in range(kv_group))
        q_spec = pl.BlockSpec((1, tq, kv_group * LANES), lambda b, g, i: (b, i, 0))
        k_spec = _single_buffered((1, seq, LANES), lambda b, g, i: (b, 0, 0))
        vt_spec = _single_buffered((1, 1, nch, V_ROWS, KV_CHUNK), lambda b, g, i: (b, g, 0, 0, 0))
    else:
        heads = tuple((n, n, n, False) for n in range(hp))
        q_spec = pl.BlockSpec((1, tq, hp * LANES), lambda b, g, i: (b, i, g))
        k_spec = _single_buffered((1, seq, hp * LANES), lambda b, g, i: (b, 0, g))
        vt_spec = _single_buffered((1, hp, nch, V_ROWS, KV_CHUNK), lambda b, g, i: (b, g, 0, 0, 0))
    assert nch % 2 == 0
    cps = math.gcd(CHUNKS_PER_STEP, nch // 2)
    s_buffer = pltpu.VMEM((hp, cps * KV_CHUNK, tq), F32)
    return pl.pallas_call(
        functools.partial(_dense_kernel, n_chunks=nch, heads=heads),
        grid=(bsz, n_heads // hp, seq // tq),
        in_specs=[q_spec, k_spec, vt_spec],
        out_specs=pl.BlockSpec((1, hp * HEAD_DIM, tq), lambda b, g, i: (b, g, i)),
        out_shape=jax.ShapeDtypeStruct((bsz, n_heads * HEAD_DIM, seq), BF16),
        scratch_shapes=[s_buffer, s_buffer],
        compiler_params=_cparams(("parallel", "parallel", "arbitrary"), ATTENTION_FLAGS),
    )(q, k, vt)


_AUG = 2 * DIFF_QK


def _diff_kernel(lam_ref, g_ref, q_ref, k_ref, vt_ref, o_ref, s0_ref, s1_ref, bias_ref, *,
                 n_chunks, hp, lam_init, slope_exp0):
    i = pl.program_id(2)
    tq = q_ref.shape[1]
    lane = lax.broadcasted_iota(jnp.int32, (tq, LANES), 1)
    digit = (lane - _AUG) & 3
    part = (lane - _AUG) >> 2
    is_bias_lane = (lane >= _AUG) & (lane < _AUG + 4 * len(LOG2E_PARTS))
    log2e_part = functools.reduce(lambda acc, x: jnp.where(part == x[0], x[1], acc),
                                  enumerate(LOG2E_PARTS), jnp.zeros((tq, LANES), F32))
    qpos = i * tq + lax.broadcasted_iota(jnp.int32, (tq, LANES), 0)
    q_hi = ((qpos >> 7) << 7).astype(F32)
    q_lo = (qpos & 127).astype(F32)
    own_dist = jnp.abs(lax.broadcasted_iota(jnp.int32, (ROW_BLOCK, tq), 1)
                       - lax.broadcasted_iota(jnp.int32, (ROW_BLOCK, tq), 0)).astype(F32)

    chains = []
    for n in range(hp):
        head = pl.program_id(1) * hp + n
        slope_bits = jnp.full((1, LANES), 127 - slope_exp0, jnp.int32) - head
        slope = lax.bitcast_convert_type(slope_bits << 23, F32)
        qf = q_ref[0, :, n * LANES:(n + 1) * LANES].astype(F32)
        aug = jnp.where(digit == 0, -slope * q_hi, jnp.where(digit == 1, -slope * q_lo, slope * log2e_part))
        aug = jnp.where(is_bias_lane, aug, 0.0)
        q1 = jnp.where(lane < DIFF_QK, qf, 0.0)
        q2 = jnp.where((lane >= DIFF_QK) & (lane < 2 * DIFF_QK), qf, 0.0)
        chains.append((jnp.concatenate([q1 + aug, q2 + aug], axis=0).astype(BF16), n, n))
        bias = (-LOG2E) * slope[:, :1] * own_dist
        bias_ref[n, 0] = jnp.zeros((ROW_BLOCK, 2 * tq), F32)
        bias_ref[n, 1] = jnp.concatenate([bias, bias], axis=1)

    key_lane = lax.broadcasted_iota(jnp.int32, (1, LANES), 1)
    key_bias_lane = (key_lane >= _AUG) & (key_lane < _AUG + 4 * len(LOG2E_PARTS))

    def edit_keys(kc, block):
        sign = jnp.where(block < i, 1.0, jnp.where(block > i, -1.0, 0.0))
        return kc * jnp.where(key_bias_lane, sign, 1.0).astype(BF16)

    def add_bias(s, n, block):
        return s + bias_ref[n, jnp.where(block == i, 1, 0)]

    accs = _pipelined_attention(k_ref, vt_ref, chains, jnp.exp2, n_chunks, s0_ref, s1_ref, (edit_keys, add_bias))

    lp = lam_ref[...]
    lam = (jnp.exp(jnp.sum(lp[0:1] * lp[1:2], axis=-1, keepdims=True))
           - jnp.exp(jnp.sum(lp[2:3] * lp[3:4], axis=-1, keepdims=True)) + lam_init)
    for n, acc in enumerate(accs):
        o = _normalised(acc[:, :tq]) - lam * _normalised(acc[:, tq:])
        o = o * lax.rsqrt(jnp.mean(o * o, axis=0, keepdims=True) + NORM_EPS)
        o_ref[0, n * DIFF_V:(n + 1) * DIFF_V, :] = (o * g_ref[...] * (1.0 - lam_init)).astype(BF16)


def _diff_attention(q, k, vt, lam_params, subln_col, *, layer_idx, slope_exp0, heads_per_step):
    bsz, seq, _ = q.shape
    nch = vt.shape[2]
    tq = ROW_BLOCK
    hp = heads_per_step
    assert seq % tq == 0 and seq <= 256 * LANES and DIFF_HEADS % hp == 0 and nch % 2 == 0
    lam_init = 0.8 - 0.6 * math.exp(-0.3 * layer_idx)
    cps = math.gcd(CHUNKS_PER_STEP, nch // 2)
    s_buffer = pltpu.VMEM((hp, cps * KV_CHUNK, 2 * tq), F32)
    return pl.pallas_call(
        functools.partial(_diff_kernel, n_chunks=nch, hp=hp, lam_init=lam_init, slope_exp0=slope_exp0),
        grid=(bsz, DIFF_HEADS // hp, seq // tq),
        in_specs=[
            _resident(lam_params.shape),
            _resident(subln_col.shape),
            pl.BlockSpec((1, tq, hp * LANES), lambda b, g, i: (b, i, g)),
            _single_buffered((1, seq, hp * LANES), lambda b, g, i: (b, 0, g)),
            _single_buffered((1, hp, nch, V_ROWS, KV_CHUNK), lambda b, g, i: (b, g, 0, 0, 0)),
        ],
        out_specs=pl.BlockSpec((1, hp * DIFF_V, tq), lambda b, g, i: (b, g, i)),
        out_shape=jax.ShapeDtypeStruct((bsz, DIFF_HEADS * DIFF_V, seq), BF16),
        scratch_shapes=[s_buffer, s_buffer, pltpu.VMEM((hp, 2, ROW_BLOCK, 2 * tq), F32)],
        compiler_params=_cparams(("parallel", "parallel", "arbitrary"), ATTENTION_FLAGS),
    )(lam_params, subln_col, q, k, vt)


def _window_kernel(sink_ref, q_ref, k_ref, vt_ref, o_ref, *, seq):
    i = pl.program_id(1)
    tq = q_ref.shape[1]
    q0 = i * tq
    j_lo = jnp.maximum(q0 - WINDOW, 0) // KV_CHUNK
    j_hi = jnp.minimum(q0 + tq - 1 + WINDOW, seq - 1) // KV_CHUNK
    lane = lax.broadcasted_iota(jnp.int32, (tq, LANES), 1)
    acc_row = lax.broadcasted_iota(jnp.int32, (V_ROWS, tq), 0)
    tpos = q0 + lax.broadcasted_iota(jnp.int32, (KV_CHUNK, tq), 1)
    krow = lax.broadcasted_iota(jnp.int32, (KV_CHUNK, tq), 0)
    group = WIN_HEADS // WIN_KV_HEADS
    chains, init = [], []
    for head in range(WIN_HEADS):
        kv_head, slab = head // group, head % group
        qf = q_ref[0, :, slab * LANES:(slab + 1) * LANES].astype(F32)
        chains.append((jnp.where((lane >> HEAD_SHIFT) == kv_head, qf, 0.0).astype(BF16), 0, kv_head))
        init.append((jnp.full((1, tq), sink_ref[head], F32), jnp.where(acc_row == HEAD_DIM, 1.0, 0.0)))

    def band(j, s, n):
        dist = jnp.abs(tpos - (j * KV_CHUNK + krow))
        return jnp.where(dist <= WINDOW, s - (2.0 ** -(n + 1)) * dist.astype(F32), NEG_INF)

    final = lax.fori_loop(j_lo, j_hi + 1, _chain_body(k_ref, vt_ref, chains, jnp.exp, band), tuple(init))
    for n, (_, acc) in enumerate(final):
        o_ref[0, n * HEAD_DIM:(n + 1) * HEAD_DIM, :] = _normalised(acc).astype(BF16)


def _window_attention(q, k, vt, sink):
    bsz, seq, _ = q.shape
    nch = vt.shape[2]
    tq = min(Q_TILE, seq)
    return pl.pallas_call(
        functools.partial(_window_kernel, seq=seq),
        grid=(bsz, seq // tq),
        in_specs=[
            pl.BlockSpec(memory_space=pltpu.SMEM),
            pl.BlockSpec((1, tq, 2 * LANES), lambda b, i: (b, i, 0)),
            _single_buffered((1, seq, LANES), lambda b, i: (b, 0, 0)),
            _single_buffered((1, WIN_KV_HEADS, nch, V_ROWS, KV_CHUNK), lambda b, i: (b, 0, 0, 0, 0)),
        ],
        out_specs=pl.BlockSpec((1, WIN_HEADS * HEAD_DIM, tq), lambda b, i: (b, 0, i)),
        out_shape=jax.ShapeDtypeStruct((bsz, WIN_HEADS * HEAD_DIM, seq), BF16),
        compiler_params=_cparams(("parallel", "arbitrary")),
    )(sink, q, k, vt)


def _outproj_kernel(x_ref, oa_ref, ob_ref, oc_ref, od_ref, wt_ref, g_ref, b_ref, o_ref, *, alpha):
    mix_t = None
    for m, ref in enumerate((oa_ref, ob_ref, oc_ref, od_ref)):
        part = jnp.dot(wt_ref[:, m * 256:(m + 1) * 256], ref[0], preferred_element_type=F32)
        mix_t = part if mix_t is None else mix_t + part
    y = alpha * x_ref[0] + mix_t.T
    o_ref[0] = _layer_norm_rows(y, g_ref[...], b_ref[...])


def _outproj(x, o_a, o_b, o_c, o_d, w_out_t, g, b, alpha):
    bsz, seq, d = x.shape
    tm = min(TOK_TILE, seq)
    ot = pl.BlockSpec((1, 256, tm), lambda bb, i: (bb, 0, i))
    tok = pl.BlockSpec((1, tm, d), lambda bb, i: (bb, i, 0))
    return pl.pallas_call(
        functools.partial(_outproj_kernel, alpha=alpha),
        grid=(bsz, seq // tm),
        in_specs=[tok, ot, ot, ot, ot, _resident(w_out_t.shape), _resident(g.shape), _resident(b.shape)],
        out_specs=tok,
        out_shape=jax.ShapeDtypeStruct((bsz, seq, d), F32),
        compiler_params=_cparams(("parallel", "parallel")),
    )(x, o_a, o_b, o_c, o_d, w_out_t, g, b)


def _rot_half_cols(start, width):
    half = width // 2
    src = np.concatenate([np.arange(start + half, start + width), np.arange(start, start + half)])
    sgn = np.concatenate([-np.ones(half), np.ones(half)])
    return src, sgn


def _gqa_slab_order(n_heads, n_kv):
    group = n_heads // n_kv
    return [kv * group + s for s in range(group) for kv in range(n_kv)]


def _layer_params(w_in, mla_q_norm, mla_w_uq, mla_kv_norm, mla_w_ukv, ax_q_norm, ax_k_norm):
    d = w_in.shape[0]
    zeros = lambda n: jnp.zeros((d, n), w_in.dtype)
    a0, b0, c0, d0 = 0, A_COLS, A_COLS + B_COLS, A_COLS + B_COLS + C_COLS
    head_cols = lambda base, h: np.arange(base + h * HEAD_DIM, base + (h + 1) * HEAD_DIM)

    def axial_rot(cols):
        src, sgn = [], []
        for blk in range(0, HEAD_DIM, HEAD_DIM // 2):
            s_, g_ = _rot_half_cols(blk, HEAD_DIM // 2)
            src.append(cols[s_])
            sgn.append(g_)
        return np.concatenate(src), np.concatenate(sgn)

    order = _gqa_slab_order(AX_HEADS, AX_KV_HEADS)
    aq = np.concatenate([head_cols(a0, h) for h in order])
    ak = np.arange(a0 + 256, a0 + 384)
    cq = np.concatenate([head_cols(c0, h) for h in order])
    ck = np.arange(c0 + 256, c0 + 384)
    cq_rot = [axial_rot(head_cols(c0, h)) for h in order]
    ck_rot = [axial_rot(head_cols(c0 + 256, h)) for h in range(AX_KV_HEADS)]

    def gather(src, sgn=None):
        w = w_in[:, np.asarray(src)]
        return w if sgn is None else w * jnp.asarray(sgn, w.dtype)[None, :]

    def widen(cols):
        return jnp.concatenate([gather(cols), zeros(LANES - len(cols))], axis=1)

    kr = np.arange(b0 + MLA_Q_RANK + MLA_KV_RANK, b0 + B_COLS)
    kr_src, kr_sgn = _rot_half_cols(kr[0], MLA_ROPE)
    place_rope = lambda w: jnp.concatenate([zeros(MLA_NOPE), w, zeros(LANES - MLA_NOPE - MLA_ROPE)], axis=1)

    w_tok = jnp.concatenate(
        [gather(aq), gather(ak), gather(cq), gather(ck)]
        + [gather(s, g) for s, g in cq_rot] + [gather(s, g) for s, g in ck_rot]
        + [widen(head_cols(d0, h)) for h in range(DIFF_HEADS)]
        + [widen(head_cols(d0 + 256, h)) for h in range(DIFF_HEADS)]
        + [gather(np.arange(b0, b0 + MLA_Q_RANK + MLA_KV_RANK)),
           place_rope(gather(kr)), place_rope(gather(kr_src, kr_sgn))], axis=1)
    assert w_tok.shape[1] == _C_END
    v_cols = np.concatenate([np.arange(a0 + 384, a0 + 512), np.arange(c0 + 384, c0 + 512),
                             np.arange(d0 + 512, d0 + 768)])
    w_vt = w_in[:, v_cols].T

    qd = MLA_NOPE + MLA_ROPE
    zq = lambda n: jnp.zeros((MLA_Q_RANK, n), mla_w_uq.dtype)
    uq, uq_rot = [], []
    for h in range(MLA_HEADS):
        blk = mla_w_uq[:, h * qd:(h + 1) * qd]
        src, sgn = _rot_half_cols(MLA_NOPE, MLA_ROPE)
        rot = blk[:, src] * jnp.asarray(sgn, blk.dtype)[None, :]
        uq += [blk, zq(LANES - qd)]
        uq_rot += [zq(MLA_NOPE), rot, zq(LANES - qd)]
    ukv = mla_w_ukv.reshape(MLA_KV_RANK, MLA_HEADS, MLA_NOPE + MLA_V)
    ukv_k = jnp.concatenate([ukv[:, :, :MLA_NOPE], jnp.zeros((MLA_KV_RANK, MLA_HEADS, LANES - MLA_NOPE), ukv.dtype)],
                            axis=2).reshape(MLA_KV_RANK, MLA_HEADS * LANES)
    ukv_vt = ukv[:, :, MLA_NOPE:].reshape(MLA_KV_RANK, MLA_HEADS * MLA_V).T

    def ax_gain(g):
        src, sgn = axial_rot(np.arange(HEAD_DIM))
        return jnp.tile(g, 2)[None, :].astype(F32), jnp.tile(g[src], 2)[None, :].astype(F32)

    ax_q, ax_q_rot = ax_gain(ax_q_norm)
    ax_k, ax_k_rot = ax_gain(ax_k_norm)
    lane = np.arange(LANES)
    g_avg = (lane[:, None] // HEAD_DIM == lane[None, :] // HEAD_DIM).astype(np.float32) / HEAD_DIM
    return {
        "w_tok": w_tok.astype(BF16), "w_vt": w_vt.astype(BF16),
        "w_uq": jnp.concatenate(uq, axis=1).astype(BF16), "w_uq_rot": jnp.concatenate(uq_rot, axis=1).astype(BF16),
        "w_ukv_k": ukv_k.astype(BF16), "w_ukv_vt": ukv_vt.astype(BF16),
        "g_avg": jnp.asarray(g_avg, BF16),
        "g_q": mla_q_norm[None, :].astype(F32), "g_kv": mla_kv_norm[None, :].astype(F32),
        "ax_q": ax_q, "ax_q_rot": ax_q_rot, "ax_k": ax_k, "ax_k_rot": ax_k_rot,
    }


def _position_tables(seq):
    pos = jnp.arange(seq, dtype=jnp.int32)
    half = HEAD_DIM // 2

    def angles(p, dim):
        inv = ROPE_BASE ** (-jnp.arange(0, dim, 2, dtype=F32) / dim)
        ang = p.astype(F32)[:, None] * inv[None, :]
        return jnp.cos(ang), jnp.sin(ang)

    cb, sb = angles(pos, MLA_ROPE)
    pad = LANES - MLA_NOPE - MLA_ROPE
    cos_b = jnp.concatenate([jnp.ones((seq, MLA_NOPE), F32), cb, cb, jnp.zeros((seq, pad), F32)], axis=1)
    sin_b = jnp.concatenate([jnp.zeros((seq, MLA_NOPE), F32), sb, sb, jnp.zeros((seq, pad), F32)], axis=1)
    cr, sr = angles(pos // GRID_W, half)
    cc, sc = angles(pos % GRID_W, half)
    cos_c = jnp.tile(jnp.concatenate([cr, cr, cc, cc], axis=1), (1, 2))
    sin_c = jnp.tile(jnp.concatenate([sr, sr, sc, sc], axis=1), (1, 2))
    aug = jnp.zeros((seq, LANES), F32)
    for x, part in enumerate(LOG2E_PARTS):
        c = _AUG + 4 * x
        aug = aug.at[:, c].set(part).at[:, c + 1].set(part)
        aug = aug.at[:, c + 2].set(((pos >> 7) << 7).astype(F32)).at[:, c + 3].set((pos & 127).astype(F32))
    return {"cos_b": cos_b, "sin_b": sin_b, "cos_c": cos_c, "sin_c": sin_c, "aug_k": aug}


def _ffn_weights(w_gu, w_down):
    d, two_ff = w_gu.shape
    d_ff = two_ff // 2
    n_chunks = d_ff // FF_CHUNK
    wgu = w_gu.astype(BF16).reshape(d, 2, n_chunks, FF_CHUNK).transpose(2, 0, 1, 3).reshape(n_chunks, d, 2 * FF_CHUNK)
    wd = w_down.astype(BF16).reshape(n_chunks, FF_CHUNK, d)
    return wgu, wd


def kernel(x, w_in, win_sink, mla_q_norm, mla_w_uq, mla_kv_norm, mla_w_ukv, ax_q_norm, ax_k_norm,
           diff_lambda, diff_subln, w_out, ffn_w_gu, ffn_w_down, ln_g, ln_b):
    depth = w_in.shape[0]
    bsz, seq, d = x.shape
    alpha = (2 * depth) ** 0.25
    tabs = _position_tables(seq)
    row = lambda v: v[None, :].astype(F32)
    for l in range(depth):
        wgu, wd = _ffn_weights(ffn_w_gu[l, 0], ffn_w_down[l, 0])
        x = _ffn(x.reshape(bsz * seq, d), wgu, wd, row(ln_g[l, 0]), row(ln_b[l, 0]), alpha).reshape(bsz, seq, d)

        p = _layer_params(w_in[l], mla_q_norm[l], mla_w_uq[l], mla_kv_norm[l], mla_w_ukv[l],
                          ax_q_norm[l], ax_k_norm[l])
        qa, ka, vta, qb, kb, vtb, qc, kc, vtc, qd, kd, vtd = _prep(x, p, tabs)
        o_a = _window_attention(qa, ka, vta, win_sink[l].astype(F32))
        o_b = _dense_attention(qb, kb, vtb, n_heads=MLA_HEADS, kv_group=0, heads_per_step=HEADS_PER_STEP)
        o_c = _dense_attention(qc, kc, vtc, n_heads=AX_HEADS, kv_group=AX_HEADS // AX_KV_HEADS,
                               heads_per_step=AX_HEADS // AX_KV_HEADS)
        o_d = _diff_attention(qd, kd, vtd, diff_lambda[l].astype(F32), diff_subln[l][:, None].astype(F32),
                              layer_idx=l, slope_exp0=WIN_HEADS + 1, heads_per_step=HEADS_PER_STEP)
        x = _outproj(x, o_a, o_b, o_c, o_d, w_out[l].T.astype(BF16), row(ln_g[l, 1]), row(ln_b[l, 1]), alpha)

        wgu, wd = _ffn_weights(ffn_w_gu[l, 1], ffn_w_down[l, 1])
        x = _ffn(x.reshape(bsz * seq, d), wgu, wd, row(ln_g[l, 2]), row(ln_b[l, 2]), alpha).reshape(bsz, seq, d)
    return x
```

```python
import functools
import math

import numpy as np
import jax
import jax.numpy as jnp
from jax import lax
from jax.experimental import pallas as pl
from jax.experimental.pallas import tpu as pltpu

F32 = jnp.float32
BF16 = jnp.bfloat16

HEAD_DIM = 64
HEAD_SHIFT = 6
GRID_W = 64
WIN_HEADS, WIN_KV_HEADS, WINDOW = 4, 2, 128
MLA_HEADS, MLA_Q_RANK, MLA_KV_RANK, MLA_NOPE, MLA_ROPE, MLA_V = 4, 256, 128, 64, 32, 64
AX_HEADS, AX_KV_HEADS = 4, 2
DIFF_HEADS, DIFF_QK, DIFF_V = 4, 32, 64
A_COLS = (WIN_HEADS + 2 * WIN_KV_HEADS) * HEAD_DIM
B_COLS = MLA_Q_RANK + MLA_KV_RANK + MLA_ROPE
C_COLS = (AX_HEADS + 2 * AX_KV_HEADS) * HEAD_DIM
D_COLS = DIFF_HEADS * (4 * DIFF_QK + DIFF_V)
ROPE_BASE = 10000.0
NORM_EPS = 1e-5
NEG_INF = -1e30

LANES = 128
F32_SUBLANES = 8
BF16_SUBLANES = 16
V_ROWS = HEAD_DIM + BF16_SUBLANES
VMEM_LIMIT = 56 * 1024 * 1024

TOK_TILE = 512
KV_CHUNK = 512
Q_TILE = 512
FF_CHUNK = 256
HEADS_PER_STEP = 2
CHUNKS_PER_STEP = 4
ROW_BLOCK = 256


def _cparams(sem, flags=None):
    return pltpu.CompilerParams(dimension_semantics=sem, vmem_limit_bytes=VMEM_LIMIT, flags=flags)


ATTENTION_FLAGS = None


def _resident(shape):
    nd = len(shape)
    return pl.BlockSpec(shape, lambda *_: (0,) * nd, pipeline_mode=pl.Buffered(1))


def _layer_norm_rows(y, g, b):
    mu = jnp.mean(y, axis=-1, keepdims=True)
    yc = y - mu
    var = jnp.mean(yc * yc, axis=-1, keepdims=True)
    return yc * lax.rsqrt(var + NORM_EPS) * g + b


def _ffn_kernel(x_ref, wgu_ref, wd_ref, g_ref, b_ref, o_ref, acc_ref, *, alpha, n_chunks):
    x = x_ref[...]
    xb = x.astype(BF16)
    acc_ref[...] = jnp.zeros_like(acc_ref)

    def body(c, carry):
        gu = jnp.dot(xb, wgu_ref[c], preferred_element_type=F32)
        gate = gu[:, :FF_CHUNK]
        up = gu[:, FF_CHUNK:]
        h = (gate * jax.nn.sigmoid(gate) * up).astype(BF16)
        acc_ref[...] += jnp.dot(h, wd_ref[c], preferred_element_type=F32)
        return carry

    lax.fori_loop(0, n_chunks, body, 0)
    y = alpha * x + 0.5 * acc_ref[...]
    o_ref[...] = _layer_norm_rows(y, g_ref[...], b_ref[...])


def _ffn(x2d, wgu, wd, g, b, alpha):
    n_tok, d = x2d.shape
    n_chunks = wgu.shape[0]
    tm = min(TOK_TILE, n_tok)
    return pl.pallas_call(
        functools.partial(_ffn_kernel, alpha=alpha, n_chunks=n_chunks),
        grid=(n_tok // tm,),
        in_specs=[
            pl.BlockSpec((tm, d), lambda i: (i, 0)),
            _resident(wgu.shape),
            _resident(wd.shape),
            _resident(g.shape),
            _resident(b.shape),
        ],
        out_specs=pl.BlockSpec((tm, d), lambda i: (i, 0)),
        out_shape=jax.ShapeDtypeStruct((n_tok, d), F32),
        scratch_shapes=[pltpu.VMEM((tm, d), F32)],
        compiler_params=_cparams(("parallel",)),
    )(x2d, wgu, wd, g, b)


_C_AQ, _C_AK, _C_CQ, _C_CK, _C_CQR, _C_CKR = 0, 256, 384, 640, 768, 1024
_C_DQ, _C_DK, _C_BQ, _C_BKV, _C_BKR, _C_BKRR, _C_END = 1152, 1664, 2176, 2432, 2560, 2688, 2816
_R_AV, _R_CV, _R_DV, _R_END = 0, 128, 256, 512


def _split_hi_lo(v):
    hi = v.astype(BF16)
    lo = (v - hi.astype(F32)).astype(BF16)
    return hi, lo


def _prep_kernel(x_ref, wtok_ref, wvt_ref, wuq_ref, wuqr_ref, wukvk_ref, wukvvt_ref, gavg_ref,
                 gq_ref, gkv_ref, axq_ref, axqr_ref, axk_ref, axkr_ref,
                 cosb_ref, sinb_ref, cosc_ref, sinc_ref, augk_ref,
                 qa_ref, ka_ref, vta_ref, qb_ref, kb_ref, vtb_ref,
                 qc_ref, kc_ref, vtc_ref, qd_ref, kd_ref, vtd_ref):
    xb = x_ref[0].astype(BF16)
    tm = xb.shape[0]
    pm = jnp.dot(xb, wtok_ref[...], preferred_element_type=F32)
    nt = (((1,), (1,)), ((), ()))
    pvt = lax.dot_general(wvt_ref[...], xb, nt, preferred_element_type=F32)

    row = lax.broadcasted_iota(jnp.int32, (BF16_SUBLANES, tm), 0)
    ones_tile = jnp.where(row == 0, 1.0, 0.0).astype(BF16)

    def put_vt(ref, head, vt):
        ref[0, head, 0, 0:HEAD_DIM, :] = vt.astype(BF16)
        ref[0, head, 0, HEAD_DIM:V_ROWS, :] = ones_tile

    qa_ref[0] = (pm[:, _C_AQ:_C_AQ + 256] * (HEAD_DIM ** -0.5)).astype(BF16)
    ka_ref[0] = pm[:, _C_AK:_C_AK + 128].astype(BF16)
    for h in range(WIN_KV_HEADS):
        put_vt(vta_ref, h, pvt[_R_AV + 64 * h:_R_AV + 64 * (h + 1)])

    cosc = cosc_ref[...]
    sinc = sinc_ref[...]
    gavg = gavg_ref[...]

    def head_rms_scale(v):
        hi, lo = _split_hi_lo(v * v)
        ms = (jnp.dot(hi, gavg, preferred_element_type=F32)
              + jnp.dot(lo, gavg, preferred_element_type=F32))
        return lax.rsqrt(ms + NORM_EPS)

    def axial(off, off_rot, g_ref_, gr_ref_, scale):
        v = pm[:, off:off + LANES]
        r = head_rms_scale(v)
        vn = v * r * g_ref_[...]
        vrn = pm[:, off_rot:off_rot + LANES] * r * gr_ref_[...]
        return ((vn * cosc + vrn * sinc) * scale).astype(BF16)

    for grp in range(2):
        qc_ref[0, :, grp * LANES:(grp + 1) * LANES] = axial(
            _C_CQ + grp * LANES, _C_CQR + grp * LANES, axq_ref, axqr_ref, HEAD_DIM ** -0.5 * LOG2E)
    kc_ref[0] = axial(_C_CK, _C_CKR, axk_ref, axkr_ref, 1.0)
    for h in range(AX_KV_HEADS):
        put_vt(vtc_ref, h, pvt[_R_CV + 64 * h:_R_CV + 64 * (h + 1)])

    qd_ref[0] = (pm[:, _C_DQ:_C_DQ + 512] * (DIFF_QK ** -0.5 * LOG2E)).astype(BF16)
    augk = augk_ref[...]
    for h in range(DIFF_HEADS):
        kd_ref[0, :, h * LANES:(h + 1) * LANES] = (
            pm[:, _C_DK + h * LANES:_C_DK + (h + 1) * LANES] + augk).astype(BF16)
        put_vt(vtd_ref, h, pvt[_R_DV + 64 * h:_R_DV + 64 * (h + 1)])

    cosb = cosb_ref[...]
    sinb = sinb_ref[...]
    cq = pm[:, _C_BQ:_C_BQ + MLA_Q_RANK]
    cqn = (cq * lax.rsqrt(jnp.mean(cq * cq, axis=-1, keepdims=True) + NORM_EPS) * gq_ref[...]).astype(BF16)
    qw = jnp.dot(cqn, wuq_ref[...], preferred_element_type=F32)
    qwr = jnp.dot(cqn, wuqr_ref[...], preferred_element_type=F32)
    ckv = pm[:, _C_BKV:_C_BKV + MLA_KV_RANK]
    ckvn = (ckv * lax.rsqrt(jnp.mean(ckv * ckv, axis=-1, keepdims=True) + NORM_EPS) * gkv_ref[...]).astype(BF16)
    kw = jnp.dot(ckvn, wukvk_ref[...], preferred_element_type=F32)
    vbt = lax.dot_general(wukvvt_ref[...], ckvn, nt, preferred_element_type=F32)
    k_rope = pm[:, _C_BKR:_C_BKR + LANES] * cosb + pm[:, _C_BKRR:_C_BKRR + LANES] * sinb
    q_scale = (MLA_NOPE + MLA_ROPE) ** -0.5 * LOG2E
    for h in range(MLA_HEADS):
        sl = slice(h * LANES, (h + 1) * LANES)
        qb_ref[0, :, sl] = ((qw[:, sl] * cosb + qwr[:, sl] * sinb) * q_scale).astype(BF16)
        kb_ref[0, :, sl] = (kw[:, sl] + k_rope).astype(BF16)
        put_vt(vtb_ref, h, vbt[64 * h:64 * (h + 1)])


def _prep(x, p, tabs):
    bsz, seq, d = x.shape
    tm = KV_CHUNK
    nch = seq // tm
    tok = lambda w: pl.BlockSpec((1, tm, w), lambda b, i: (b, i, 0))
    tab = pl.BlockSpec((tm, LANES), lambda b, i: (i, 0))
    vt = lambda nh: pl.BlockSpec((1, nh, 1, V_ROWS, tm), lambda b, i: (b, 0, i, 0, 0))
    vt_shape = lambda nh: jax.ShapeDtypeStruct((bsz, nh, nch, V_ROWS, tm), BF16)
    tok_shape = lambda w: jax.ShapeDtypeStruct((bsz, seq, w), BF16)
    weights = [p["w_tok"], p["w_vt"], p["w_uq"], p["w_uq_rot"], p["w_ukv_k"], p["w_ukv_vt"], p["g_avg"],
               p["g_q"], p["g_kv"], p["ax_q"], p["ax_q_rot"], p["ax_k"], p["ax_k_rot"]]
    return pl.pallas_call(
        _prep_kernel,
        grid=(bsz, nch),
        in_specs=[tok(d)] + [_resident(w.shape) for w in weights] + [tab] * 5,
        out_specs=[tok(256), tok(128), vt(2), tok(512), tok(512), vt(4),
                   tok(256), tok(128), vt(2), tok(512), tok(512), vt(4)],
        out_shape=[tok_shape(256), tok_shape(128), vt_shape(2), tok_shape(512), tok_shape(512), vt_shape(4),
                   tok_shape(256), tok_shape(128), vt_shape(2), tok_shape(512), tok_shape(512), vt_shape(4)],
        compiler_params=_cparams(("parallel", "parallel")),
    )(x, *weights, tabs["cos_b"], tabs["sin_b"], tabs["cos_c"], tabs["sin_c"], tabs["aug_k"])


_NT = (((1,), (1,)), ((), ()))
LOG2E = math.log2(math.e)


def _bf16_parts(x, n):
    parts = []
    for _ in range(n):
        part = float(np.float32(x).astype(jnp.bfloat16).astype(np.float32))
        parts.append(part)
        x -= part
    return tuple(parts)


LOG2E_PARTS = _bf16_parts(LOG2E, 3)


def _key_rows(k_ref, j, slab, cps):
    rows = pl.ds(pl.multiple_of(j * (cps * KV_CHUNK), cps * KV_CHUNK), cps * KV_CHUNK)
    return k_ref[0, rows, slab * LANES:(slab + 1) * LANES]


def _chain_body(k_ref, vt_ref, chains, exp_fn, transform=None, cps=1):
    def body(j, carry):
        scores = []
        for n, (q, k_slab, _) in enumerate(chains):
            s = lax.dot_general(_key_rows(k_ref, j, k_slab, cps), q, _NT, preferred_element_type=F32)
            scores.append(s if transform is None else transform(j, s, n))
        probs = []
        for s, (m, _) in zip(scores, carry):
            m_new = jnp.maximum(m, jnp.max(s, axis=0, keepdims=True))
            probs.append((m_new, exp_fn(m - m_new), exp_fn(s - m_new).astype(BF16)))
        out = []
        for (_, _, v_head), (m_new, alpha, p), (_, acc) in zip(chains, probs, carry):
            pv = None
            for c in range(cps):
                part = jnp.dot(vt_ref[0, v_head, j * cps + c], p[c * KV_CHUNK:(c + 1) * KV_CHUNK],
                               preferred_element_type=F32)
                pv = part if pv is None else pv + part
            out.append((m_new, acc * alpha + pv))
        return tuple(out)
    return body


def _value_matmul(vt_ref, v_head, step, cps, p):
    pv = None
    for c in range(cps):
        part = jnp.dot(vt_ref[0, v_head, step * cps + c], p[c * KV_CHUNK:(c + 1) * KV_CHUNK],
                       preferred_element_type=F32)
        pv = part if pv is None else pv + part
    return pv


def _score_block(k_ref, q, k_slab, row0, s_write, n, r, hooks):
    start = row0 + r * ROW_BLOCK
    if not isinstance(start, int):
        start = pl.multiple_of(start, ROW_BLOCK)
    kc = k_ref[0, pl.ds(start, ROW_BLOCK), k_slab * LANES:(k_slab + 1) * LANES]
    if hooks is not None:
        kc = hooks[0](kc, start // ROW_BLOCK)
    s = lax.dot_general(kc, q, _NT, preferred_element_type=F32)
    if hooks is not None:
        s = hooks[1](s, n, start // ROW_BLOCK)
    s_write[n, r * ROW_BLOCK:(r + 1) * ROW_BLOCK, :] = s
    return jnp.max(s.reshape(ROW_BLOCK // F32_SUBLANES, F32_SUBLANES, s.shape[-1]), axis=0)


def _first_scores(k_ref, chains, cps, s_write, hooks=None):
    out = []
    for n, (q, k_slab, _) in enumerate(chains):
        parts = [_score_block(k_ref, q, k_slab, 0, s_write, n, r, hooks)
                 for r in range(cps * KV_CHUNK // ROW_BLOCK)]
        out.append(functools.reduce(jnp.maximum, parts))
    return out


def _pipelined_step(k_ref, vt_ref, chains, exp_fn, cps, hooks, j, carry, s_read, s_write):
    step_rows = cps * KV_CHUNK
    blocks_per_chunk = KV_CHUNK // ROW_BLOCK
    next_row0 = (j + 1) * step_rows
    out = []
    for n, ((q, k_slab, v_head), (m, acc, smax)) in enumerate(zip(chains, carry)):
        m_new = jnp.maximum(m, jnp.max(smax, axis=0, keepdims=True))
        alpha = exp_fn(m - m_new)
        next_max, pv = smax, None
        for r in range(step_rows // ROW_BLOCK):
            if s_write is not None:
                part = _score_block(k_ref, q, k_slab, next_row0, s_write, n, r, hooks)
                next_max = part if r == 0 else jnp.maximum(next_max, part)
            p = exp_fn((s_read[n, r * ROW_BLOCK:(r + 1) * ROW_BLOCK, :] - m_new).astype(BF16))
            c, b = divmod(r, blocks_per_chunk)
            vt = vt_ref[0, v_head, j * cps + c, :, b * ROW_BLOCK:(b + 1) * ROW_BLOCK]
            part = jnp.dot(vt, p, preferred_element_type=F32)
            pv = part if pv is None else pv + part
        out.append((m_new, acc * alpha + pv, next_max))
    return tuple(out)


def _pipelined_attention(k_ref, vt_ref, chains, exp_fn, n_chunks, s0_ref, s1_ref, hooks=None):
    nq = chains[0][0].shape[0]
    cps = s0_ref.shape[1] // KV_CHUNK
    n_steps = n_chunks // cps
    step = functools.partial(_pipelined_step, k_ref, vt_ref, chains, exp_fn, cps, hooks)
    first_max = _first_scores(k_ref, chains, cps, s0_ref, hooks)
    init = tuple((jnp.full((1, nq), NEG_INF, F32), jnp.zeros((V_ROWS, nq), F32), smax) for smax in first_max)

    def pair(t, carry):
        carry = step(2 * t, carry, s0_ref, s1_ref)
        return step(2 * t + 1, carry, s1_ref, s0_ref)

    carry = lax.fori_loop(0, n_steps // 2 - 1, pair, init)
    carry = step(n_steps - 2, carry, s0_ref, s1_ref)
    final = step(n_steps - 1, carry, s1_ref, None)
    return [acc for _, acc, _ in final]


def _normalised(acc):
    return acc[:HEAD_DIM] / acc[HEAD_DIM:HEAD_DIM + 1]


def _single_buffered(block, index_map):
    return pl.BlockSpec(block, index_map, pipeline_mode=pl.Buffered(1))


def _dense_kernel(q_ref, k_ref, vt_ref, o_ref, s0_ref, s1_ref, *, n_chunks, heads):
    tq = q_ref.shape[1]
    lane = lax.broadcasted_iota(jnp.int32, (tq, LANES), 1)
    chains = []
    for q_slab, k_slab, v_head, shared_key_slab in heads:
        q = q_ref[0, :, q_slab * LANES:(q_slab + 1) * LANES]
        if shared_key_slab:
            q = jnp.where((lane >> HEAD_SHIFT) == pl.program_id(1), q.astype(F32), 0.0).astype(BF16)
        chains.append((q, k_slab, v_head))
    accs = _pipelined_attention(k_ref, vt_ref, chains, jnp.exp2, n_chunks, s0_ref, s1_ref)
    for n, acc in enumerate(accs):
        o_ref[0, n * HEAD_DIM:(n + 1) * HEAD_DIM, :] = _normalised(acc).astype(BF16)


def _dense_attention(q, k, vt, *, n_heads, kv_group, heads_per_step):
    bsz, seq, _ = q.shape
    nch = vt.shape[2]
    tq = min(Q_TILE, seq)
    hp = heads_per_step
    if kv_group:
        assert hp == kv_group and n_heads // kv_group == LANES // HEAD_DIM
        heads = tuple((s, 0, 0, True) for s in range(kv_group))
        q_spec = pl.BlockSpec((1, tq, kv_group * LANES), lambda b, g, i: (b, i, 0))
        k_spec = _single_buffered((1, seq, LANES), lambda b, g, i: (b, 0, 0))
        vt_spec = _single_buffered((1, 1, nch, V_ROWS, KV_CHUNK), lambda b, g, i: (b, g, 0, 0, 0))
    else:
        heads = tuple((n, n, n, False) for n in range(hp))
        q_spec = pl.BlockSpec((1, tq, hp * LANES), lambda b, g, i: (b, i, g))
        k_spec = _single_buffered((1, seq, hp * LANES), lambda b, g, i: (b, 0, g))
        vt_spec = _single_buffered((1, hp, nch, V_ROWS, KV_CHUNK), lambda b, g, i: (b, g, 0, 0, 0))
    assert nch % 2 == 0
    cps = math.gcd(CHUNKS_PER_STEP, nch // 2)
    s_buffer = pltpu.VMEM((hp, cps * KV_CHUNK, tq), F32)
    return pl.pallas_call(
        functools.partial(_dense_kernel, n_chunks=nch, heads=heads),
        grid=(bsz, n_heads // hp, seq // tq),
        in_specs=[q_spec, k_spec, vt_spec],
        out_specs=pl.BlockSpec((1, hp * HEAD_DIM, tq), lambda b, g, i: (b, g, i)),
        out_shape=jax.ShapeDtypeStruct((bsz, n_heads * HEAD_DIM, seq), BF16),
        scratch_shapes=[s_buffer, s_buffer],
        compiler_params=_cparams(("parallel", "parallel", "arbitrary"), ATTENTION_FLAGS),
    )(q, k, vt)


_AUG = 2 * DIFF_QK


def _diff_kernel(lam_ref, g_ref, q_ref, k_ref, vt_ref, o_ref, s0_ref, s1_ref, bias_ref, *,
                 n_chunks, hp, lam_init, slope_exp0):
    i = pl.program_id(2)
    tq = q_ref.shape[1]
    lane = lax.broadcasted_iota(jnp.int32, (tq, LANES), 1)
    digit = (lane - _AUG) & 3
    part = (lane - _AUG) >> 2
    is_bias_lane = (lane >= _AUG) & (lane < _AUG + 4 * len(LOG2E_PARTS))
    log2e_part = functools.reduce(lambda acc, x: jnp.where(part == x[0], x[1], acc),
                                  enumerate(LOG2E_PARTS), jnp.zeros((tq, LANES), F32))
    qpos = i * tq + lax.broadcasted_iota(jnp.int32, (tq, LANES), 0)
    q_hi = ((qpos >> 7) << 7).astype(F32)
    q_lo = (qpos & 127).astype(F32)
    own_dist = jnp.abs(lax.broadcasted_iota(jnp.int32, (ROW_BLOCK, tq), 1)
                       - lax.broadcasted_iota(jnp.int32, (ROW_BLOCK, tq), 0)).astype(F32)

    chains = []
    for n in range(hp):
        head = pl.program_id(1) * hp + n
        slope_bits = jnp.full((1, LANES), 127 - slope_exp0, jnp.int32) - head
        slope = lax.bitcast_convert_type(slope_bits << 23, F32)
        qf = q_ref[0, :, n * LANES:(n + 1) * LANES].astype(F32)
        aug = jnp.where(digit == 0, -slope * q_hi, jnp.where(digit == 1, -slope * q_lo, slope * log2e_part))
        aug = jnp.where(is_bias_lane, aug, 0.0)
        q1 = jnp.where(lane < DIFF_QK, qf, 0.0)
        q2 = jnp.where((lane >= DIFF_QK) & (lane < 2 * DIFF_QK), qf, 0.0)
        chains.append((jnp.concatenate([q1 + aug, q2 + aug], axis=0).astype(BF16), n, n))
        bias = (-LOG2E) * slope[:, :1] * own_dist
        bias_ref[n, 0] = jnp.zeros((ROW_BLOCK, 2 * tq), F32)
        bias_ref[n, 1] = jnp.concatenate([bias, bias], axis=1)

    key_lane = lax.broadcasted_iota(jnp.int32, (1, LANES), 1)
    key_bias_lane = (key_lane >= _AUG) & (key_lane < _AUG + 4 * len(LOG2E_PARTS))

    def edit_keys(kc, block):
        sign = jnp.where(block < i, 1.0, jnp.where(block > i, -1.0, 0.0))
        return kc * jnp.where(key_bias_lane, sign, 1.0).astype(BF16)

    def add_bias(s, n, block):
        return s + bias_ref[n, jnp.where(block == i, 1, 0)]

    accs = _pipelined_attention(k_ref, vt_ref, chains, jnp.exp2, n_chunks, s0_ref, s1_ref, (edit_keys, add_bias))

    lp = lam_ref[...]
    lam = (jnp.exp(jnp.sum(lp[0:1] * lp[1:2], axis=-1, keepdims=True))
           - jnp.exp(jnp.sum(lp[2:3] * lp[3:4], axis=-1, keepdims=True)) + lam_init)
    for n, acc in enumerate(accs):
        o = _normalised(acc[:, :tq]) - lam * _normalised(acc[:, tq:])
        o = o * lax.rsqrt(jnp.mean(o * o, axis=0, keepdims=True) + NORM_EPS)
        o_ref[0, n * DIFF_V:(n + 1) * DIFF_V, :] = (o * g_ref[...] * (1.0 - lam_init)).astype(BF16)


def _diff_attention(q, k, vt, lam_params, subln_col, *, layer_idx, slope_exp0, heads_per_step):
    bsz, seq, _ = q.shape
    nch = vt.shape[2]
    tq = ROW_BLOCK
    hp = heads_per_step
    assert seq % tq == 0 and seq <= 256 * LANES and DIFF_HEADS % hp == 0 and nch % 2 == 0
    lam_init = 0.8 - 0.6 * math.exp(-0.3 * layer_idx)
    cps = math.gcd(CHUNKS_PER_STEP, nch // 2)
    s_buffer = pltpu.VMEM((hp, cps * KV_CHUNK, 2 * tq), F32)
    return pl.pallas_call(
        functools.partial(_diff_kernel, n_chunks=nch, hp=hp, lam_init=lam_init, slope_exp0=slope_exp0),
        grid=(bsz, DIFF_HEADS // hp, seq // tq),
        in_specs=[
            _resident(lam_params.shape),
            _resident(subln_col.shape),
            pl.BlockSpec((1, tq, hp * LANES), lambda b, g, i: (b, i, g)),
            _single_buffered((1, seq, hp * LANES), lambda b, g, i: (b, 0, g)),
            _single_buffered((1, hp, nch, V_ROWS, KV_CHUNK), lambda b, g, i: (b, g, 0, 0, 0)),
        ],
        out_specs=pl.BlockSpec((1, hp * DIFF_V, tq), lambda b, g, i: (b, g, i)),
        out_shape=jax.ShapeDtypeStruct((bsz, DIFF_HEADS * DIFF_V, seq), BF16),
        scratch_shapes=[s_buffer, s_buffer, pltpu.VMEM((hp, 2, ROW_BLOCK, 2 * tq), F32)],
        compiler_params=_cparams(("parallel", "parallel", "arbitrary"), ATTENTION_FLAGS),
    )(lam_params, subln_col, q, k, vt)


def _window_kernel(sink_ref, q_ref, k_ref, vt_ref, o_ref, *, seq):
    i = pl.program_id(1)
    tq = q_ref.shape[1]
    q0 = i * tq
    j_lo = jnp.maximum(q0 - WINDOW, 0) // KV_CHUNK
    j_hi = jnp.minimum(q0 + tq - 1 + WINDOW, seq - 1) // KV_CHUNK
    lane = lax.broadcasted_iota(jnp.int32, (tq, LANES), 1)
    acc_row = lax.broadcasted_iota(jnp.int32, (V_ROWS, tq), 0)
    tpos = q0 + lax.broadcasted_iota(jnp.int32, (KV_CHUNK, tq), 1)
    krow = lax.broadcasted_iota(jnp.int32, (KV_CHUNK, tq), 0)
    group = WIN_HEADS // WIN_KV_HEADS
    chains, init = [], []
    for head in range(WIN_HEADS):
        kv_head, slab = head // group, head % group
        qf = q_ref[0, :, slab * LANES:(slab + 1) * LANES].astype(F32)
        chains.append((jnp.where((lane >> HEAD_SHIFT) == kv_head, qf, 0.0).astype(BF16), 0, kv_head))
        init.append((jnp.full((1, tq), sink_ref[head], F32), jnp.where(acc_row == HEAD_DIM, 1.0, 0.0)))

    def band(j, s, n):
        dist = jnp.abs(tpos - (j * KV_CHUNK + krow))
        return jnp.where(dist <= WINDOW, s - (2.0 ** -(n + 1)) * dist.astype(F32), NEG_INF)

    final = lax.fori_loop(j_lo, j_hi + 1, _chain_body(k_ref, vt_ref, chains, jnp.exp, band), tuple(init))
    for n, (_, acc) in enumerate(final):
        o_ref[0, n * HEAD_DIM:(n + 1) * HEAD_DIM, :] = _normalised(acc).astype(BF16)


def _window_attention(q, k, vt, sink):
    bsz, seq, _ = q.shape
    nch = vt.shape[2]
    tq = min(Q_TILE, seq)
    return pl.pallas_call(
        functools.partial(_window_kernel, seq=seq),
        grid=(bsz, seq // tq),
        in_specs=[
            pl.BlockSpec(memory_space=pltpu.SMEM),
            pl.BlockSpec((1, tq, 2 * LANES), lambda b, i: (b, i, 0)),
            _single_buffered((1, seq, LANES), lambda b, i: (b, 0, 0)),
            _single_buffered((1, WIN_KV_HEADS, nch, V_ROWS, KV_CHUNK), lambda b, i: (b, 0, 0, 0, 0)),
        ],
        out_specs=pl.BlockSpec((1, WIN_HEADS * HEAD_DIM, tq), lambda b, i: (b, 0, i)),
        out_shape=jax.ShapeDtypeStruct((bsz, WIN_HEADS * HEAD_DIM, seq), BF16),
        compiler_params=_cparams(("parallel", "arbitrary")),
    )(sink, q, k, vt)


def _outproj_kernel(x_ref, oa_ref, ob_ref, oc_ref, od_ref, wt_ref, g_ref, b_ref, o_ref, *, alpha):
    mix_t = None
    for m, ref in enumerate((oa_ref, ob_ref, oc_ref, od_ref)):
        part = jnp.dot(wt_ref[:, m * 256:(m + 1) * 256], ref[0], preferred_element_type=F32)
        mix_t = part if mix_t is None else mix_t + part
    y = alpha * x_ref[0] + mix_t.T
    o_ref[0] = _layer_norm_rows(y, g_ref[...], b_ref[...])


def _outproj(x, o_a, o_b, o_c, o_d, w_out_t, g, b, alpha):
    bsz, seq, d = x.shape
    tm = min(TOK_TILE, seq)
    ot = pl.BlockSpec((1, 256, tm), lambda bb, i: (bb, 0, i))
    tok = pl.BlockSpec((1, tm, d), lambda bb, i: (bb, i, 0))
    return pl.pallas_call(
        functools.partial(_outproj_kernel, alpha=alpha),
        grid=(bsz, seq // tm),
        in_specs=[tok, ot, ot, ot, ot, _resident(w_out_t.shape), _resident(g.shape), _resident(b.shape)],
        out_specs=tok,
        out_shape=jax.ShapeDtypeStruct((bsz, seq, d), F32),
        compiler_params=_cparams(("parallel", "parallel")),
    )(x, o_a, o_b, o_c, o_d, w_out_t, g, b)


def _rot_half_cols(start, width):
    half = width // 2
    src = np.concatenate([np.arange(start + half, start + width), np.arange(start, start + half)])
    sgn = np.concatenate([-np.ones(half), np.ones(half)])
    return src, sgn


def _gqa_slab_order(n_heads, n_kv):
    group = n_heads // n_kv
    return [kv * group + s for s in range(group) for kv in range(n_kv)]


def _layer_params(w_in, mla_q_norm, mla_w_uq, mla_kv_norm, mla_w_ukv, ax_q_norm, ax_k_norm):
    d = w_in.shape[0]
    zeros = lambda n: jnp.zeros((d, n), w_in.dtype)
    a0, b0, c0, d0 = 0, A_COLS, A_COLS + B_COLS, A_COLS + B_COLS + C_COLS
    head_cols = lambda base, h: np.arange(base + h * HEAD_DIM, base + (h + 1) * HEAD_DIM)

    def axial_rot(cols):
        src, sgn = [], []
        for blk in range(0, HEAD_DIM, HEAD_DIM // 2):
            s_, g_ = _rot_half_cols(blk, HEAD_DIM // 2)
            src.append(cols[s_])
            sgn.append(g_)
        return np.concatenate(src), np.concatenate(sgn)

    order = _gqa_slab_order(AX_HEADS, AX_KV_HEADS)
    aq = np.concatenate([head_cols(a0, h) for h in order])
    ak = np.arange(a0 + 256, a0 + 384)
    cq = np.concatenate([head_cols(c0, h) for h in order])
    ck = np.arange(c0 + 256, c0 + 384)
    cq_rot = [axial_rot(head_cols(c0, h)) for h in order]
    ck_rot = [axial_rot(head_cols(c0 + 256, h)) for h in range(AX_KV_HEADS)]

    def gather(src, sgn=None):
        w = w_in[:, np.asarray(src)]
        return w if sgn is None else w * jnp.asarray(sgn, w.dtype)[None, :]

    def widen(cols):
        return jnp.concatenate([gather(cols), zeros(LANES - len(cols))], axis=1)

    kr = np.arange(b0 + MLA_Q_RANK + MLA_KV_RANK, b0 + B_COLS)
    kr_src, kr_sgn = _rot_half_cols(kr[0], MLA_ROPE)
    place_rope = lambda w: jnp.concatenate([zeros(MLA_NOPE), w, zeros(LANES - MLA_NOPE - MLA_ROPE)], axis=1)

    w_tok = jnp.concatenate(
        [gather(aq), gather(ak), gather(cq), gather(ck)]
        + [gather(s, g) for s, g in cq_rot] + [gather(s, g) for s, g in ck_rot]
        + [widen(head_cols(d0, h)) for h in range(DIFF_HEADS)]
        + [widen(head_cols(d0 + 256, h)) for h in range(DIFF_HEADS)]
        + [gather(np.arange(b0, b0 + MLA_Q_RANK + MLA_KV_RANK)),
           place_rope(gather(kr)), place_rope(gather(kr_src, kr_sgn))], axis=1)
    assert w_tok.shape[1] == _C_END
    v_cols = np.concatenate([np.arange(a0 + 384, a0 + 512), np.arange(c0 + 384, c0 + 512),
                             np.arange(d0 + 512, d0 + 768)])
    w_vt = w_in[:, v_cols].T

    qd = MLA_NOPE + MLA_ROPE
    zq = lambda n: jnp.zeros((MLA_Q_RANK, n), mla_w_uq.dtype)
    uq, uq_rot = [], []
    for h in range(MLA_HEADS):
        blk = mla_w_uq[:, h * qd:(h + 1) * qd]
        src, sgn = _rot_half_cols(MLA_NOPE, MLA_ROPE)
        rot = blk[:, src] * jnp.asarray(sgn, blk.dtype)[None, :]
        uq += [blk, zq(LANES - qd)]
        uq_rot += [zq(MLA_NOPE), rot, zq(LANES - qd)]
    ukv = mla_w_ukv.reshape(MLA_KV_RANK, MLA_HEADS, MLA_NOPE + MLA_V)
    ukv_k = jnp.concatenate([ukv[:, :, :MLA_NOPE], jnp.zeros((MLA_KV_RANK, MLA_HEADS, LANES - MLA_NOPE), ukv.dtype)],
                            axis=2).reshape(MLA_KV_RANK, MLA_HEADS * LANES)
    ukv_vt = ukv[:, :, MLA_NOPE:].reshape(MLA_KV_RANK, MLA_HEADS * MLA_V).T

    def ax_gain(g):
        src, sgn = axial_rot(np.arange(HEAD_DIM))
        return jnp.tile(g, 2)[None, :].astype(F32), jnp.tile(g[src], 2)[None, :].astype(F32)

    ax_q, ax_q_rot = ax_gain(ax_q_norm)
    ax_k, ax_k_rot = ax_gain(ax_k_norm)
    lane = np.arange(LANES)
    g_avg = (lane[:, None] // HEAD_DIM == lane[None, :] // HEAD_DIM).astype(np.float32) / HEAD_DIM
    return {
        "w_tok": w_tok.astype(BF16), "w_vt": w_vt.astype(BF16),
        "w_uq": jnp.concatenate(uq, axis=1).astype(BF16), "w_uq_rot": jnp.concatenate(uq_rot, axis=1).astype(BF16),
        "w_ukv_k": ukv_k.astype(BF16), "w_ukv_vt": ukv_vt.astype(BF16),
        "g_avg": jnp.asarray(g_avg, BF16),
        "g_q": mla_q_norm[None, :].astype(F32), "g_kv": mla_kv_norm[None, :].astype(F32),
        "ax_q": ax_q, "ax_q_rot": ax_q_rot, "ax_k": ax_k, "ax_k_rot": ax_k_rot,
    }


def _position_tables(seq):
    pos = jnp.arange(seq, dtype=jnp.int32)
    half = HEAD_DIM // 2

    def angles(p, dim):
        inv = ROPE_BASE ** (-jnp.arange(0, dim, 2, dtype=F32) / dim)
        ang = p.astype(F32)[:, None] * inv[None, :]
        return jnp.cos(ang), jnp.sin(ang)

    cb, sb = angles(pos, MLA_ROPE)
    pad = LANES - MLA_NOPE - MLA_ROPE
    cos_b = jnp.concatenate([jnp.ones((seq, MLA_NOPE), F32), cb, cb, jnp.zeros((seq, pad), F32)], axis=1)
    sin_b = jnp.concatenate([jnp.zeros((seq, MLA_NOPE), F32), sb, sb, jnp.zeros((seq, pad), F32)], axis=1)
    cr, sr = angles(pos // GRID_W, half)
    cc, sc = angles(pos % GRID_W, half)
    cos_c = jnp.tile(jnp.concatenate([cr, cr, cc, cc], axis=1), (1, 2))
    sin_c = jnp.tile(jnp.concatenate([sr, sr, sc, sc], axis=1), (1, 2))
    aug = jnp.zeros((seq, LANES), F32)
    for x, part in enumerate(LOG2E_PARTS):
        c = _AUG + 4 * x
        aug = aug.at[:, c].set(part).at[:, c + 1].set(part)
        aug = aug.at[:, c + 2].set(((pos >> 7) << 7).astype(F32)).at[:, c + 3].set((pos & 127).astype(F32))
    return {"cos_b": cos_b, "sin_b": sin_b, "cos_c": cos_c, "sin_c": sin_c, "aug_k": aug}


def _ffn_weights(w_gu, w_down):
    d, two_ff = w_gu.shape
    d_ff = two_ff // 2
    n_chunks = d_ff // FF_CHUNK
    wgu = w_gu.astype(BF16).reshape(d, 2, n_chunks, FF_CHUNK).transpose(2, 0, 1, 3).reshape(n_chunks, d, 2 * FF_CHUNK)
    wd = w_down.astype(BF16).reshape(n_chunks, FF_CHUNK, d)
    return wgu, wd


def kernel(x, w_in, win_sink, mla_q_norm, mla_w_uq, mla_kv_norm, mla_w_ukv, ax_q_norm, ax_k_norm,
           diff_lambda, diff_subln, w_out, ffn_w_gu, ffn_w_down, ln_g, ln_b):
    depth = w_in.shape[0]
    bsz, seq, d = x.shape
    alpha = (2 * depth) ** 0.25
    tabs = _position_tables(seq)
    row = lambda v: v[None, :].astype(F32)
    for l in range(depth):
        wgu, wd = _ffn_weights(ffn_w_gu[l, 0], ffn_w_down[l, 0])
        x = _ffn(x.reshape(bsz * seq, d), wgu, wd, row(ln_g[l, 0]), row(ln_b[l, 0]), alpha).reshape(bsz, seq, d)

        p = _layer_params(w_in[l], mla_q_norm[l], mla_w_uq[l], mla_kv_norm[l], mla_w_ukv[l],
                          ax_q_norm[l], ax_k_norm[l])
        qa, ka, vta, qb, kb, vtb, qc, kc, vtc, qd, kd, vtd = _prep(x, p, tabs)
        o_a = _window_attention(qa, ka, vta, win_sink[l].astype(F32))
        o_b = _dense_attention(qb, kb, vtb, n_heads=MLA_HEADS, kv_group=0, heads_per_step=HEADS_PER_STEP)
        o_c = _dense_attention(qc, kc, vtc, n_heads=AX_HEADS, kv_group=AX_HEADS // AX_KV_HEADS,
                               heads_per_step=AX_HEADS // AX_KV_HEADS)
        o_d = _diff_attention(qd, kd, vtd, diff_lambda[l].astype(F32), diff_subln[l][:, None].astype(F32),
                              layer_idx=l, slope_exp0=WIN_HEADS + 1, heads_per_step=HEADS_PER_STEP)
        x = _outproj(x, o_a, o_b, o_c, o_d, w_out[l].T.astype(BF16), row(ln_g[l, 1]), row(ln_b[l, 1]), alpha)

        wgu, wd = _ffn_weights(ffn_w_gu[l, 1], ffn_w_down[l, 1])
        x = _ffn(x.reshape(bsz * seq, d), wgu, wd, row(ln_g[l, 2]), row(ln_b[l, 2]), alpha).reshape(bsz, seq, d)
    return x
```

```python
import functools
import math

import numpy as np
import jax
import jax.numpy as jnp
from jax import lax
from jax.experimental import pallas as pl
from jax.experimental.pallas import tpu as pltpu

F32 = jnp.float32
BF16 = jnp.bfloat16

HEAD_DIM = 64
HEAD_SHIFT = 6
GRID_W = 64
WIN_HEADS, WIN_KV_HEADS, WINDOW = 4, 2, 128
MLA_HEADS, MLA_Q_RANK, MLA_KV_RANK, MLA_NOPE, MLA_ROPE, MLA_V = 4, 256, 128, 64, 32, 64
AX_HEADS, AX_KV_HEADS = 4, 2
DIFF_HEADS, DIFF_QK, DIFF_V = 4, 32, 64
A_COLS = (WIN_HEADS + 2 * WIN_KV_HEADS) * HEAD_DIM
B_COLS = MLA_Q_RANK + MLA_KV_RANK + MLA_ROPE
C_COLS = (AX_HEADS + 2 * AX_KV_HEADS) * HEAD_DIM
D_COLS = DIFF_HEADS * (4 * DIFF_QK + DIFF_V)
ROPE_BASE = 10000.0
NORM_EPS = 1e-5
NEG_INF = -1e30

LANES = 128
F32_SUBLANES = 8
BF16_SUBLANES = 16
V_ROWS = HEAD_DIM + BF16_SUBLANES
VMEM_LIMIT = 56 * 1024 * 1024

TOK_TILE = 512
FFN_TOK_TILE = 1024
KV_CHUNK = 512
Q_TILE = 512
FF_CHUNK = 256
HEADS_PER_STEP = 2
CHUNKS_PER_STEP = 4
ROW_BLOCK = 256
DENSE_ROW_BLOCK = 256


def _cparams(sem, flags=None):
    return pltpu.CompilerParams(dimension_semantics=sem, vmem_limit_bytes=VMEM_LIMIT, flags=flags)


ATTENTION_FLAGS = None


def _resident(shape):
    nd = len(shape)
    return pl.BlockSpec(shape, lambda *_: (0,) * nd, pipeline_mode=pl.Buffered(1))


def _layer_norm_rows(y, g, b):
    mu = jnp.mean(y, axis=-1, keepdims=True)
    yc = y - mu
    var = jnp.mean(yc * yc, axis=-1, keepdims=True)
    return yc * lax.rsqrt(var + NORM_EPS) * g + b


def _ffn_kernel(x_ref, wgu_ref, wd_ref, g_ref, b_ref, o_ref, acc_ref, *, alpha, n_chunks):
    x = x_ref[...]
    xb = x.astype(BF16)
    acc_ref[...] = jnp.zeros_like(acc_ref)

    def body(c, carry):
        gu = jnp.dot(xb, wgu_ref[c], preferred_element_type=F32)
        gate = gu[:, :FF_CHUNK]
        up = gu[:, FF_CHUNK:]
        h = (gate * jax.nn.sigmoid(gate) * up).astype(BF16)
        acc_ref[...] += jnp.dot(h, wd_ref[c], preferred_element_type=F32)
        return carry

    lax.fori_loop(0, n_chunks, body, 0)
    y = alpha * x + 0.5 * acc_ref[...]
    o_ref[...] = _layer_norm_rows(y, g_ref[...], b_ref[...])


def _ffn(x2d, wgu, wd, g, b, alpha):
    n_tok, d = x2d.shape
    n_chunks = wgu.shape[0]
    tm = min(FFN_TOK_TILE, n_tok)
    return pl.pallas_call(
        functools.partial(_ffn_kernel, alpha=alpha, n_chunks=n_chunks),
        grid=(n_tok // tm,),
        in_specs=[
            pl.BlockSpec((tm, d), lambda i: (i, 0)),
            _resident(wgu.shape),
            _resident(wd.shape),
            _resident(g.shape),
            _resident(b.shape),
        ],
        out_specs=pl.BlockSpec((tm, d), lambda i: (i, 0)),
        out_shape=jax.ShapeDtypeStruct((n_tok, d), F32),
        scratch_shapes=[pltpu.VMEM((tm, d), F32)],
        compiler_params=_cparams(("parallel",)),
    )(x2d, wgu, wd, g, b)


_C_AQ, _C_AK, _C_CQ, _C_CK, _C_CQR, _C_CKR = 0, 256, 384, 640, 768, 1024
_C_DQ, _C_DK, _C_BQ, _C_BKV, _C_BKR, _C_BKRR, _C_END = 1152, 1664, 2176, 2432, 2560, 2688, 2816
_R_AV, _R_CV, _R_DV, _R_END = 0, 128, 256, 512


def _split_hi_lo(v):
    hi = v.astype(BF16)
    lo = (v - hi.astype(F32)).astype(BF16)
    return hi, lo


def _prep_kernel(x_ref, wtok_ref, wvt_ref, wuq_ref, wuqr_ref, wukvk_ref, wukvvt_ref, gavg_ref,
                 gq_ref, gkv_ref, axq_ref, axqr_ref, axk_ref, axkr_ref,
                 cosb_ref, sinb_ref, cosc_ref, sinc_ref, augk_ref,
                 qa_ref, ka_ref, vta_ref, qb_ref, kb_ref, vtb_ref,
                 qc_ref, kc_ref, vtc_ref, qd_ref, kd_ref, vtd_ref):
    xb = x_ref[0].astype(BF16)
    tm = xb.shape[0]
    pm = jnp.dot(xb, wtok_ref[...], preferred_element_type=F32)
    nt = (((1,), (1,)), ((), ()))
    pvt = lax.dot_general(wvt_ref[...], xb, nt, preferred_element_type=F32)

    row = lax.broadcasted_iota(jnp.int32, (BF16_SUBLANES, tm), 0)
    ones_tile = jnp.where(row == 0, 1.0, 0.0).astype(BF16)

    def put_vt(ref, head, vt):
        ref[0, head, 0, 0:HEAD_DIM, :] = vt.astype(BF16)
        ref[0, head, 0, HEAD_DIM:V_ROWS, :] = ones_tile

    qa_ref[0] = (pm[:, _C_AQ:_C_AQ + 256] * (HEAD_DIM ** -0.5)).astype(BF16)
    ka_ref[0] = pm[:, _C_AK:_C_AK + 128].astype(BF16)
    for h in range(WIN_KV_HEADS):
        put_vt(vta_ref, h, pvt[_R_AV + 64 * h:_R_AV + 64 * (h + 1)])

    cosc = cosc_ref[...]
    sinc = sinc_ref[...]
    gavg = gavg_ref[...]

    def head_rms_scale(v):
        hi, lo = _split_hi_lo(v * v)
        ms = (jnp.dot(hi, gavg, preferred_element_type=F32)
              + jnp.dot(lo, gavg, preferred_element_type=F32))
        return lax.rsqrt(ms + NORM_EPS)

    def axial(off, off_rot, g_ref_, gr_ref_, scale):
        v = pm[:, off:off + LANES]
        r = head_rms_scale(v)
        vn = v * r * g_ref_[...]
        vrn = pm[:, off_rot:off_rot + LANES] * r * gr_ref_[...]
        return ((vn * cosc + vrn * sinc) * scale).astype(BF16)

    for grp in range(2):
        qc_ref[0, :, grp * LANES:(grp + 1) * LANES] = axial(
            _C_CQ + grp * LANES, _C_CQR + grp * LANES, axq_ref, axqr_ref, HEAD_DIM ** -0.5 * LOG2E)
    kc_ref[0] = axial(_C_CK, _C_CKR, axk_ref, axkr_ref, 1.0)
    for h in range(AX_KV_HEADS):
        put_vt(vtc_ref, h, pvt[_R_CV + 64 * h:_R_CV + 64 * (h + 1)])

    qd_ref[0] = (pm[:, _C_DQ:_C_DQ + 512] * (DIFF_QK ** -0.5 * LOG2E)).astype(BF16)
    augk = augk_ref[...]
    for h in range(DIFF_HEADS):
        kd_ref[0, :, h * LANES:(h + 1) * LANES] = (
            pm[:, _C_DK + h * LANES:_C_DK + (h + 1) * LANES] + augk).astype(BF16)
        put_vt(vtd_ref, h, pvt[_R_DV + 64 * h:_R_DV + 64 * (h + 1)])

    cosb = cosb_ref[...]
    sinb = sinb_ref[...]
    cq = pm[:, _C_BQ:_C_BQ + MLA_Q_RANK]
    cqn = (cq * lax.rsqrt(jnp.mean(cq * cq, axis=-1, keepdims=True) + NORM_EPS) * gq_ref[...]).astype(BF16)
    qw = jnp.dot(cqn, wuq_ref[...], preferred_element_type=F32)
    qwr = jnp.dot(cqn, wuqr_ref[...], preferred_element_type=F32)
    ckv = pm[:, _C_BKV:_C_BKV + MLA_KV_RANK]
    ckvn = (ckv * lax.rsqrt(jnp.mean(ckv * ckv, axis=-1, keepdims=True) + NORM_EPS) * gkv_ref[...]).astype(BF16)
    kw = jnp.dot(ckvn, wukvk_ref[...], preferred_element_type=F32)
    vbt = lax.dot_general(wukvvt_ref[...], ckvn, nt, preferred_element_type=F32)
    k_rope = pm[:, _C_BKR:_C_BKR + LANES] * cosb + pm[:, _C_BKRR:_C_BKRR + LANES] * sinb
    q_scale = (MLA_NOPE + MLA_ROPE) ** -0.5 * LOG2E
    for h in range(MLA_HEADS):
        sl = slice(h * LANES, (h + 1) * LANES)
        qb_ref[0, :, sl] = ((qw[:, sl] * cosb + qwr[:, sl] * sinb) * q_scale).astype(BF16)
        kb_ref[0, :, sl] = (kw[:, sl] + k_rope).astype(BF16)
        put_vt(vtb_ref, h, vbt[64 * h:64 * (h + 1)])


def _prep(x, p, tabs):
    bsz, seq, d = x.shape
    tm = KV_CHUNK
    nch = seq // tm
    tok = lambda w: pl.BlockSpec((1, tm, w), lambda b, i: (b, i, 0))
    tab = pl.BlockSpec((tm, LANES), lambda b, i: (i, 0))
    vt = lambda nh: pl.BlockSpec((1, nh, 1, V_ROWS, tm), lambda b, i: (b, 0, i, 0, 0))
    vt_shape = lambda nh: jax.ShapeDtypeStruct((bsz, nh, nch, V_ROWS, tm), BF16)
    tok_shape = lambda w: jax.ShapeDtypeStruct((bsz, seq, w), BF16)
    weights = [p["w_tok"], p["w_vt"], p["w_uq"], p["w_uq_rot"], p["w_ukv_k"], p["w_ukv_vt"], p["g_avg"],
               p["g_q"], p["g_kv"], p["ax_q"], p["ax_q_rot"], p["ax_k"], p["ax_k_rot"]]
    return pl.pallas_call(
        _prep_kernel,
        grid=(bsz, nch),
        in_specs=[tok(d)] + [_resident(w.shape) for w in weights] + [tab] * 5,
        out_specs=[tok(256), tok(128), vt(2), tok(512), tok(512), vt(4),
                   tok(256), tok(128), vt(2), tok(512), tok(512), vt(4)],
        out_shape=[tok_shape(256), tok_shape(128), vt_shape(2), tok_shape(512), tok_shape(512), vt_shape(4),
                   tok_shape(256), tok_shape(128), vt_shape(2), tok_shape(512), tok_shape(512), vt_shape(4)],
        compiler_params=_cparams(("parallel", "parallel")),
    )(x, *weights, tabs["cos_b"], tabs["sin_b"], tabs["cos_c"], tabs["sin_c"], tabs["aug_k"])


_NT = (((1,), (1,)), ((), ()))
LOG2E = math.log2(math.e)


def _bf16_parts(x, n):
    parts = []
    for _ in range(n):
        part = float(np.float32(x).astype(jnp.bfloat16).astype(np.float32))
        parts.append(part)
        x -= part
    return tuple(parts)


LOG2E_PARTS = _bf16_parts(LOG2E, 3)


def _key_rows(k_ref, j, slab, cps):
    rows = pl.ds(pl.multiple_of(j * (cps * KV_CHUNK), cps * KV_CHUNK), cps * KV_CHUNK)
    return k_ref[0, rows, slab * LANES:(slab + 1) * LANES]


def _chain_body(k_ref, vt_ref, chains, exp_fn, transform=None, cps=1):
    def body(j, carry):
        scores = []
        for n, (q, k_slab, _) in enumerate(chains):
            s = lax.dot_general(_key_rows(k_ref, j, k_slab, cps), q, _NT, preferred_element_type=F32)
            scores.append(s if transform is None else transform(j, s, n))
        probs = []
        for s, (m, _) in zip(scores, carry):
            m_new = jnp.maximum(m, jnp.max(s, axis=0, keepdims=True))
            probs.append((m_new, exp_fn(m - m_new), exp_fn(s - m_new).astype(BF16)))
        out = []
        for (_, _, v_head), (m_new, alpha, p), (_, acc) in zip(chains, probs, carry):
            pv = None
            for c in range(cps):
                part = jnp.dot(vt_ref[0, v_head, j * cps + c], p[c * KV_CHUNK:(c + 1) * KV_CHUNK],
                               preferred_element_type=F32)
                pv = part if pv is None else pv + part
            out.append((m_new, acc * alpha + pv))
        return tuple(out)
    return body


def _value_matmul(vt_ref, v_head, step, cps, p):
    pv = None
    for c in range(cps):
        part = jnp.dot(vt_ref[0, v_head, step * cps + c], p[c * KV_CHUNK:(c + 1) * KV_CHUNK],
                       preferred_element_type=F32)
        pv = part if pv is None else pv + part
    return pv


def _score_block(k_ref, q, k_slab, rb, row0, s_write, n, r, hooks):
    start = row0 + r * rb
    if not isinstance(start, int):
        start = pl.multiple_of(start, rb)
    kc = k_ref[0, pl.ds(start, rb), k_slab * LANES:(k_slab + 1) * LANES]
    if hooks is not None:
        kc = hooks[0](kc, start // rb)
    s = lax.dot_general(kc, q, _NT, preferred_element_type=F32)
    if hooks is not None:
        s = hooks[1](s, n, start // rb)
    s_write[n, r * rb:(r + 1) * rb, :] = s
    return jnp.max(s.reshape(rb // F32_SUBLANES, F32_SUBLANES, s.shape[-1]), axis=0)


def _first_scores(k_ref, chains, cps, rb, s_write, hooks):
    out = []
    for n, (q, k_slab, _) in enumerate(chains):
        parts = [_score_block(k_ref, q, k_slab, rb, 0, s_write, n, r, hooks) for r in range(cps * KV_CHUNK // rb)]
        out.append(functools.reduce(jnp.maximum, parts))
    return out


def _pipelined_step(k_ref, vt_ref, chains, exp_fn, cps, rb, hooks, j, carry, s_read, s_write):
    step_rows = cps * KV_CHUNK
    blocks_per_chunk = KV_CHUNK // rb
    next_row0 = (j + 1) * step_rows
    out = []
    for n, ((q, k_slab, v_head), (m, acc, smax)) in enumerate(zip(chains, carry)):
        m_new = jnp.maximum(m, jnp.max(smax, axis=0, keepdims=True))
        alpha = exp_fn(m - m_new)
        next_max, pv = smax, None
        for r in range(step_rows // rb):
            if s_write is not None:
                part = _score_block(k_ref, q, k_slab, rb, next_row0, s_write, n, r, hooks)
                next_max = part if r == 0 else jnp.maximum(next_max, part)
            p = exp_fn(s_read[n, r * rb:(r + 1) * rb, :] - m_new).astype(BF16)
            c, b = divmod(r, blocks_per_chunk)
            part = jnp.dot(vt_ref[0, v_head, j * cps + c, :, b * rb:(b + 1) * rb], p, preferred_element_type=F32)
            pv = part if pv is None else pv + part
        out.append((m_new, acc * alpha + pv, next_max))
    return tuple(out)


def _pipelined_attention(k_ref, vt_ref, chains, exp_fn, n_chunks, rb, s0_ref, s1_ref, hooks=None):
    nq = chains[0][0].shape[0]
    cps = s0_ref.shape[1] // KV_CHUNK
    n_steps = n_chunks // cps
    step = functools.partial(_pipelined_step, k_ref, vt_ref, chains, exp_fn, cps, rb, hooks)
    first_max = _first_scores(k_ref, chains, cps, rb, s0_ref, hooks)
    init = tuple((jnp.full((1, nq), NEG_INF, F32), jnp.zeros((V_ROWS, nq), F32), smax) for smax in first_max)

    def pair(t, carry):
        carry = step(2 * t, carry, s0_ref, s1_ref)
        return step(2 * t + 1, carry, s1_ref, s0_ref)

    carry = lax.fori_loop(0, n_steps // 2 - 1, pair, init)
    carry = step(n_steps - 2, carry, s0_ref, s1_ref)
    final = step(n_steps - 1, carry, s1_ref, None)
    return [acc for _, acc, _ in final]


def _normalised(acc):
    return acc[:HEAD_DIM] / acc[HEAD_DIM:HEAD_DIM + 1]


def _single_buffered(block, index_map):
    return pl.BlockSpec(block, index_map, pipeline_mode=pl.Buffered(1))


def _dense_kernel(q_ref, k_ref, vt_ref, o_ref, s0_ref, s1_ref, *, n_chunks, heads):
    tq = q_ref.shape[1]
    lane = lax.broadcasted_iota(jnp.int32, (tq, LANES), 1)
    chains = []
    for q_slab, k_slab, v_head, shared_key_slab in heads:
        q = q_ref[0, :, q_slab * LANES:(q_slab + 1) * LANES]
        if shared_key_slab:
            q = jnp.where((lane >> HEAD_SHIFT) == pl.program_id(1), q.astype(F32), 0.0).astype(BF16)
        chains.append((q, k_slab, v_head))
    accs = _pipelined_attention(k_ref, vt_ref, chains, jnp.exp2, n_chunks, DENSE_ROW_BLOCK, s0_ref, s1_ref)
    for n, acc in enumerate(accs):
        o_ref[0, n * HEAD_DIM:(n + 1) * HEAD_DIM, :] = _normalised(acc).astype(BF16)


def _dense_attention(q, k, vt, *, n_heads, kv_group, heads_per_step):
    bsz, seq, _ = q.shape
    nch = vt.shape[2]
    tq = min(Q_TILE, seq)
    hp = heads_per_step
    if kv_group:
        assert hp == kv_group and n_heads // kv_group == LANES // HEAD_DIM
        heads = tuple((s, 0, 0, True) for s in range(kv_group))
        q_spec = pl.BlockSpec((1, tq, kv_group * LANES), lambda b, g, i: (b, i, 0))
        k_spec = pl.BlockSpec((1, seq, LANES), lambda b, g, i: (b, 0, 0))
        vt_spec = pl.BlockSpec((1, 1, nch, V_ROWS, KV_CHUNK), lambda b, g, i: (b, g, 0, 0, 0))
    else:
        heads = tuple((n, n, n, False) for n in range(hp))
        q_spec = pl.BlockSpec((1, tq, hp * LANES), lambda b, g, i: (b, i, g))
        k_spec = pl.BlockSpec((1, seq, hp * LANES), lambda b, g, i: (b, 0, g))
        vt_spec = pl.BlockSpec((1, hp, nch, V_ROWS, KV_CHUNK), lambda b, g, i: (b, g, 0, 0, 0))
    assert nch % 2 == 0
    cps = math.gcd(CHUNKS_PER_STEP, nch // 2)
    s_buffer = pltpu.VMEM((hp, cps * KV_CHUNK, tq), F32)
    return pl.pallas_call(
        functools.partial(_dense_kernel, n_chunks=nch, heads=heads),
        grid=(bsz, n_heads // hp, seq // tq),
        in_specs=[q_spec, k_spec, vt_spec],
        out_specs=pl.BlockSpec((1, hp * HEAD_DIM, tq), lambda b, g, i: (b, g, i)),
        out_shape=jax.ShapeDtypeStruct((bsz, n_heads * HEAD_DIM, seq), BF16),
        scratch_shapes=[s_buffer, s_buffer],
        compiler_params=_cparams(("parallel", "parallel", "arbitrary"), ATTENTION_FLAGS),
    )(q, k, vt)


_AUG = 2 * DIFF_QK


def _diff_kernel(lam_ref, g_ref, q_ref, k_ref, vt_ref, o_ref, s0_ref, s1_ref, bias_ref, *,
                 n_chunks, hp, lam_init, slope_exp0):
    i = pl.program_id(2)
    tq = q_ref.shape[1]
    lane = lax.broadcasted_iota(jnp.int32, (tq, LANES), 1)
    digit = (lane - _AUG) & 3
    part = (lane - _AUG) >> 2
    is_bias_lane = (lane >= _AUG) & (lane < _AUG + 4 * len(LOG2E_PARTS))
    log2e_part = functools.reduce(lambda acc, x: jnp.where(part == x[0], x[1], acc),
                                  enumerate(LOG2E_PARTS), jnp.zeros((tq, LANES), F32))
    qpos = i * tq + lax.broadcasted_iota(jnp.int32, (tq, LANES), 0)
    q_hi = ((qpos >> 7) << 7).astype(F32)
    q_lo = (qpos & 127).astype(F32)
    own_dist = jnp.abs(lax.broadcasted_iota(jnp.int32, (ROW_BLOCK, tq), 1)
                       - lax.broadcasted_iota(jnp.int32, (ROW_BLOCK, tq), 0)).astype(F32)

    chains = []
    for n in range(hp):
        head = pl.program_id(1) * hp + n
        slope_bits = jnp.full((1, LANES), 127 - slope_exp0, jnp.int32) - head
        slope = lax.bitcast_convert_type(slope_bits << 23, F32)
        qf = q_ref[0, :, n * LANES:(n + 1) * LANES].astype(F32)
        aug = jnp.where(digit == 0, -slope * q_hi, jnp.where(digit == 1, -slope * q_lo, slope * log2e_part))
        aug = jnp.where(is_bias_lane, aug, 0.0)
        q1 = jnp.where(lane < DIFF_QK, qf, 0.0)
        q2 = jnp.where((lane >= DIFF_QK) & (lane < 2 * DIFF_QK), qf, 0.0)
        chains.append((jnp.concatenate([q1 + aug, q2 + aug], axis=0).astype(BF16), n, n))
        bias = (-LOG2E) * slope[:, :1] * own_dist
        bias_ref[n, 0] = jnp.zeros((ROW_BLOCK, 2 * tq), F32)
        bias_ref[n, 1] = jnp.concatenate([bias, bias], axis=1)

    key_lane = lax.broadcasted_iota(jnp.int32, (1, LANES), 1)
    key_bias_lane = (key_lane >= _AUG) & (key_lane < _AUG + 4 * len(LOG2E_PARTS))

    def edit_keys(kc, block):
        sign = jnp.where(block < i, 1.0, jnp.where(block > i, -1.0, 0.0))
        return kc * jnp.where(key_bias_lane, sign, 1.0).astype(BF16)

    def add_bias(s, n, block):
        return s + bias_ref[n, jnp.where(block == i, 1, 0)]

    accs = _pipelined_attention(k_ref, vt_ref, chains, jnp.exp2, n_chunks, ROW_BLOCK, s0_ref, s1_ref,
                                (edit_keys, add_bias))

    lp = lam_ref[...]
    lam = (jnp.exp(jnp.sum(lp[0:1] * lp[1:2], axis=-1, keepdims=True))
           - jnp.exp(jnp.sum(lp[2:3] * lp[3:4], axis=-1, keepdims=True)) + lam_init)
    for n, acc in enumerate(accs):
        o = _normalised(acc[:, :tq]) - lam * _normalised(acc[:, tq:])
        o = o * lax.rsqrt(jnp.mean(o * o, axis=0, keepdims=True) + NORM_EPS)
        o_ref[0, n * DIFF_V:(n + 1) * DIFF_V, :] = (o * g_ref[...] * (1.0 - lam_init)).astype(BF16)


def _diff_attention(q, k, vt, lam_params, subln_col, *, layer_idx, slope_exp0, heads_per_step):
    bsz, seq, _ = q.shape
    nch = vt.shape[2]
    tq = ROW_BLOCK
    hp = heads_per_step
    assert seq % tq == 0 and seq <= 256 * LANES and DIFF_HEADS % hp == 0 and nch % 2 == 0
    lam_init = 0.8 - 0.6 * math.exp(-0.3 * layer_idx)
    cps = math.gcd(CHUNKS_PER_STEP, nch // 2)
    s_buffer = pltpu.VMEM((hp, cps * KV_CHUNK, 2 * tq), F32)
    return pl.pallas_call(
        functools.partial(_diff_kernel, n_chunks=nch, hp=hp, lam_init=lam_init, slope_exp0=slope_exp0),
        grid=(bsz, DIFF_HEADS // hp, seq // tq),
        in_specs=[
            _resident(lam_params.shape),
            _resident(subln_col.shape),
            pl.BlockSpec((1, tq, hp * LANES), lambda b, g, i: (b, i, g)),
            pl.BlockSpec((1, seq, hp * LANES), lambda b, g, i: (b, 0, g)),
            pl.BlockSpec((1, hp, nch, V_ROWS, KV_CHUNK), lambda b, g, i: (b, g, 0, 0, 0)),
        ],
        out_specs=pl.BlockSpec((1, hp * DIFF_V, tq), lambda b, g, i: (b, g, i)),
        out_shape=jax.ShapeDtypeStruct((bsz, DIFF_HEADS * DIFF_V, seq), BF16),
        scratch_shapes=[s_buffer, s_buffer, pltpu.VMEM((hp, 2, ROW_BLOCK, 2 * tq), F32)],
        compiler_params=_cparams(("parallel", "parallel", "arbitrary"), ATTENTION_FLAGS),
    )(lam_params, subln_col, q, k, vt)


def _window_kernel(sink_ref, q_ref, k_ref, vt_ref, o_ref, *, seq):
    i = pl.program_id(1)
    tq = q_ref.shape[1]
    q0 = i * tq
    j_lo = jnp.maximum(q0 - WINDOW, 0) // KV_CHUNK
    j_hi = jnp.minimum(q0 + tq - 1 + WINDOW, seq - 1) // KV_CHUNK
    lane = lax.broadcasted_iota(jnp.int32, (tq, LANES), 1)
    acc_row = lax.broadcasted_iota(jnp.int32, (V_ROWS, tq), 0)
    tpos = q0 + lax.broadcasted_iota(jnp.int32, (KV_CHUNK, tq), 1)
    krow = lax.broadcasted_iota(jnp.int32, (KV_CHUNK, tq), 0)
    group = WIN_HEADS // WIN_KV_HEADS
    chains, init = [], []
    for head in range(WIN_HEADS):
        kv_head, slab = head // group, head % group
        qf = q_ref[0, :, slab * LANES:(slab + 1) * LANES].astype(F32)
        chains.append((jnp.where((lane >> HEAD_SHIFT) == kv_head, qf, 0.0).astype(BF16), 0, kv_head))
        init.append((jnp.full((1, tq), sink_ref[head], F32), jnp.where(acc_row == HEAD_DIM, 1.0, 0.0)))

    def band(j, s, n):
        dist = jnp.abs(tpos - (j * KV_CHUNK + krow))
        return jnp.where(dist <= WINDOW, s - (2.0 ** -(n + 1)) * dist.astype(F32), NEG_INF)

    final = lax.fori_loop(j_lo, j_hi + 1, _chain_body(k_ref, vt_ref, chains, jnp.exp, band), tuple(init))
    for n, (_, acc) in enumerate(final):
        o_ref[0, n * HEAD_DIM:(n + 1) * HEAD_DIM, :] = _normalised(acc).astype(BF16)


def _window_attention(q, k, vt, sink):
    bsz, seq, _ = q.shape
    nch = vt.shape[2]
    tq = min(Q_TILE, seq)
    return pl.pallas_call(
        functools.partial(_window_kernel, seq=seq),
        grid=(bsz, seq // tq),
        in_specs=[
            pl.BlockSpec(memory_space=pltpu.SMEM),
            pl.BlockSpec((1, tq, 2 * LANES), lambda b, i: (b, i, 0)),
            _single_buffered((1, seq, LANES), lambda b, i: (b, 0, 0)),
            _single_buffered((1, WIN_KV_HEADS, nch, V_ROWS, KV_CHUNK), lambda b, i: (b, 0, 0, 0, 0)),
        ],
        out_specs=pl.BlockSpec((1, WIN_HEADS * HEAD_DIM, tq), lambda b, i: (b, 0, i)),
        out_shape=jax.ShapeDtypeStruct((bsz, WIN_HEADS * HEAD_DIM, seq), BF16),
        compiler_params=_cparams(("parallel", "arbitrary")),
    )(sink, q, k, vt)


def _outproj_kernel(x_ref, oa_ref, ob_ref, oc_ref, od_ref, wt_ref, g_ref, b_ref, o_ref, *, alpha):
    mix_t = None
    for m, ref in enumerate((oa_ref, ob_ref, oc_ref, od_ref)):
        part = jnp.dot(wt_ref[:, m * 256:(m + 1) * 256], ref[0], preferred_element_type=F32)
        mix_t = part if mix_t is None else mix_t + part
    y = alpha * x_ref[0] + mix_t.T
    o_ref[0] = _layer_norm_rows(y, g_ref[...], b_ref[...])


def _outproj(x, o_a, o_b, o_c, o_d, w_out_t, g, b, alpha):
    bsz, seq, d = x.shape
    tm = min(TOK_TILE, seq)
    ot = pl.BlockSpec((1, 256, tm), lambda bb, i: (bb, 0, i))
    tok = pl.BlockSpec((1, tm, d), lambda bb, i: (bb, i, 0))
    return pl.pallas_call(
        functools.partial(_outproj_kernel, alpha=alpha),
        grid=(bsz, seq // tm),
        in_specs=[tok, ot, ot, ot, ot, _resident(w_out_t.shape), _resident(g.shape), _resident(b.shape)],
        out_specs=tok,
        out_shape=jax.ShapeDtypeStruct((bsz, seq, d), F32),
        compiler_params=_cparams(("parallel", "parallel")),
    )(x, o_a, o_b, o_c, o_d, w_out_t, g, b)


def _rot_half_cols(start, width):
    half = width // 2
    src = np.concatenate([np.arange(start + half, start + width), np.arange(start, start + half)])
    sgn = np.concatenate([-np.ones(half), np.ones(half)])
    return src, sgn


def _gqa_slab_order(n_heads, n_kv):
    group = n_heads // n_kv
    return [kv * group + s for s in range(group) for kv in range(n_kv)]


def _layer_params(w_in, mla_q_norm, mla_w_uq, mla_kv_norm, mla_w_ukv, ax_q_norm, ax_k_norm):
    d = w_in.shape[0]
    zeros = lambda n: jnp.zeros((d, n), w_in.dtype)
    a0, b0, c0, d0 = 0, A_COLS, A_COLS + B_COLS, A_COLS + B_COLS + C_COLS
    head_cols = lambda base, h: np.arange(base + h * HEAD_DIM, base + (h + 1) * HEAD_DIM)

    def axial_rot(cols):
        src, sgn = [], []
        for blk in range(0, HEAD_DIM, HEAD_DIM // 2):
            s_, g_ = _rot_half_cols(blk, HEAD_DIM // 2)
            src.append(cols[s_])
            sgn.append(g_)
        return np.concatenate(src), np.concatenate(sgn)

    order = _gqa_slab_order(AX_HEADS, AX_KV_HEADS)
    aq = np.concatenate([head_cols(a0, h) for h in order])
    ak = np.arange(a0 + 256, a0 + 384)
    cq = np.concatenate([head_cols(c0, h) for h in order])
    ck = np.arange(c0 + 256, c0 + 384)
    cq_rot = [axial_rot(head_cols(c0, h)) for h in order]
    ck_rot = [axial_rot(head_cols(c0 + 256, h)) for h in range(AX_KV_HEADS)]

    def gather(src, sgn=None):
        w = w_in[:, np.asarray(src)]
        return w if sgn is None else w * jnp.asarray(sgn, w.dtype)[None, :]

    def widen(cols):
        return jnp.concatenate([gather(cols), zeros(LANES - len(cols))], axis=1)

    kr = np.arange(b0 + MLA_Q_RANK + MLA_KV_RANK, b0 + B_COLS)
    kr_src, kr_sgn = _rot_half_cols(kr[0], MLA_ROPE)
    place_rope = lambda w: jnp.concatenate([zeros(MLA_NOPE), w, zeros(LANES - MLA_NOPE - MLA_ROPE)], axis=1)

    w_tok = jnp.concatenate(
        [gather(aq), gather(ak), gather(cq), gather(ck)]
        + [gather(s, g) for s, g in cq_rot] + [gather(s, g) for s, g in ck_rot]
        + [widen(head_cols(d0, h)) for h in range(DIFF_HEADS)]
        + [widen(head_cols(d0 + 256, h)) for h in range(DIFF_HEADS)]
        + [gather(np.arange(b0, b0 + MLA_Q_RANK + MLA_KV_RANK)),
           place_rope(gather(kr)), place_rope(gather(kr_src, kr_sgn))], axis=1)
    assert w_tok.shape[1] == _C_END
    v_cols = np.concatenate([np.arange(a0 + 384, a0 + 512), np.arange(c0 + 384, c0 + 512),
                             np.arange(d0 + 512, d0 + 768)])
    w_vt = w_in[:, v_cols].T

    qd = MLA_NOPE + MLA_ROPE
    zq = lambda n: jnp.zeros((MLA_Q_RANK, n), mla_w_uq.dtype)
    uq, uq_rot = [], []
    for h in range(MLA_HEADS):
        blk = mla_w_uq[:, h * qd:(h + 1) * qd]
        src, sgn = _rot_half_cols(MLA_NOPE, MLA_ROPE)
        rot = blk[:, src] * jnp.asarray(sgn, blk.dtype)[None, :]
        uq += [blk, zq(LANES - qd)]
        uq_rot += [zq(MLA_NOPE), rot, zq(LANES - qd)]
    ukv = mla_w_ukv.reshape(MLA_KV_RANK, MLA_HEADS, MLA_NOPE + MLA_V)
    ukv_k = jnp.concatenate([ukv[:, :, :MLA_NOPE], jnp.zeros((MLA_KV_RANK, MLA_HEADS, LANES - MLA_NOPE), ukv.dtype)],
                            axis=2).reshape(MLA_KV_RANK, MLA_HEADS * LANES)
    ukv_vt = ukv[:, :, MLA_NOPE:].reshape(MLA_KV_RANK, MLA_HEADS * MLA_V).T

    def ax_gain(g):
        src, sgn = axial_rot(np.arange(HEAD_DIM))
        return jnp.tile(g, 2)[None, :].astype(F32), jnp.tile(g[src], 2)[None, :].astype(F32)

    ax_q, ax_q_rot = ax_gain(ax_q_norm)
    ax_k, ax_k_rot = ax_gain(ax_k_norm)
    lane = np.arange(LANES)
    g_avg = (lane[:, None] // HEAD_DIM == lane[None, :] // HEAD_DIM).astype(np.float32) / HEAD_DIM
    return {
        "w_tok": w_tok.astype(BF16), "w_vt": w_vt.astype(BF16),
        "w_uq": jnp.concatenate(uq, axis=1).astype(BF16), "w_uq_rot": jnp.concatenate(uq_rot, axis=1).astype(BF16),
        "w_ukv_k": ukv_k.astype(BF16), "w_ukv_vt": ukv_vt.astype(BF16),
        "g_avg": jnp.asarray(g_avg, BF16),
        "g_q": mla_q_norm[None, :].astype(F32), "g_kv": mla_kv_norm[None, :].astype(F32),
        "ax_q": ax_q, "ax_q_rot": ax_q_rot, "ax_k": ax_k, "ax_k_rot": ax_k_rot,
    }


def _position_tables(seq):
    pos = jnp.arange(seq, dtype=jnp.int32)
    half = HEAD_DIM // 2

    def angles(p, dim):
        inv = ROPE_BASE ** (-jnp.arange(0, dim, 2, dtype=F32) / dim)
        ang = p.astype(F32)[:, None] * inv[None, :]
        return jnp.cos(ang), jnp.sin(ang)

    cb, sb = angles(pos, MLA_ROPE)
    pad = LANES - MLA_NOPE - MLA_ROPE
    cos_b = jnp.concatenate([jnp.ones((seq, MLA_NOPE), F32), cb, cb, jnp.zeros((seq, pad), F32)], axis=1)
    sin_b = jnp.concatenate([jnp.zeros((seq, MLA_NOPE), F32), sb, sb, jnp.zeros((seq, pad), F32)], axis=1)
    cr, sr = angles(pos // GRID_W, half)
    cc, sc = angles(pos % GRID_W, half)
    cos_c = jnp.tile(jnp.concatenate([cr, cr, cc, cc], axis=1), (1, 2))
    sin_c = jnp.tile(jnp.concatenate([sr, sr, sc, sc], axis=1), (1, 2))
    rel = np.arange(LANES) - _AUG
    in_aug = (rel >= 0) & (rel < 4 * len(LOG2E_PARTS))
    part_row = np.where(in_aug, np.asarray(LOG2E_PARTS + (0.0,), np.float32)[np.clip(rel >> 2, 0, len(LOG2E_PARTS))], 0.0)
    digit = jnp.asarray((rel & 3)[None, :])
    pos_hi = ((pos >> 7) << 7).astype(F32)[:, None]
    pos_lo = (pos & 127).astype(F32)[:, None]
    aug = jnp.where(digit < 2, jnp.asarray(part_row, F32)[None, :], jnp.where(digit == 2, pos_hi, pos_lo))
    aug = jnp.where(jnp.asarray(in_aug[None, :]), aug, 0.0)
    return {"cos_b": cos_b, "sin_b": sin_b, "cos_c": cos_c, "sin_c": sin_c, "aug_k": aug}


def _ffn_weights(w_gu, w_down):
    d, two_ff = w_gu.shape
    d_ff = two_ff // 2
    n_chunks = d_ff // FF_CHUNK
    wgu = w_gu.astype(BF16).reshape(d, 2, n_chunks, FF_CHUNK).transpose(2, 0, 1, 3).reshape(n_chunks, d, 2 * FF_CHUNK)
    wd = w_down.astype(BF16).reshape(n_chunks, FF_CHUNK, d)
    return wgu, wd


def kernel(x, w_in, win_sink, mla_q_norm, mla_w_uq, mla_kv_norm, mla_w_ukv, ax_q_norm, ax_k_norm,
           diff_lambda, diff_subln, w_out, ffn_w_gu, ffn_w_down, ln_g, ln_b):
    depth = w_in.shape[0]
    bsz, seq, d = x.shape
    alpha = (2 * depth) ** 0.25
    tabs = _position_tables(seq)
    row = lambda v: v[None, :].astype(F32)
    for l in range(depth):
        wgu, wd = _ffn_weights(ffn_w_gu[l, 0], ffn_w_down[l, 0])
        x = _ffn(x.reshape(bsz * seq, d), wgu, wd, row(ln_g[l, 0]), row(ln_b[l, 0]), alpha).reshape(bsz, seq, d)

        p = _layer_params(w_in[l], mla_q_norm[l], mla_w_uq[l], mla_kv_norm[l], mla_w_ukv[l],
                          ax_q_norm[l], ax_k_norm[l])
        qa, ka, vta, qb, kb, vtb, qc, kc, vtc, qd, kd, vtd = _prep(x, p, tabs)
        o_a = _window_attention(qa, ka, vta, win_sink[l].astype(F32))
        o_b = _dense_attention(qb, kb, vtb, n_heads=MLA_HEADS, kv_group=0, heads_per_step=HEADS_PER_STEP)
        o_c = _dense_attention(qc, kc, vtc, n_heads=AX_HEADS, kv_group=AX_HEADS // AX_KV_HEADS,
                               heads_per_step=AX_HEADS // AX_KV_HEADS)
        o_d = _diff_attention(qd, kd, vtd, diff_lambda[l].astype(F32), diff_subln[l][:, None].astype(F32),
                              layer_idx=l, slope_exp0=WIN_HEADS + 1, heads_per_step=HEADS_PER_STEP)
        x = _outproj(x, o_a, o_b, o_c, o_d, w_out[l].T.astype(BF16), row(ln_g[l, 1]), row(ln_b[l, 1]), alpha)

        wgu, wd = _ffn_weights(ffn_w_gu[l, 1], ffn_w_down[l, 1])
        x = _ffn(x.reshape(bsz * seq, d), wgu, wd, row(ln_g[l, 2]), row(ln_b[l, 2]), alpha).reshape(bsz, seq, d)
    return x
```

```python
import functools
import math

import numpy as np
import jax
import jax.numpy as jnp
from jax import lax
from jax.experimental import pallas as pl
from jax.experimental.pallas import tpu as pltpu

F32 = jnp.float32
BF16 = jnp.bfloat16

HEAD_DIM = 64
HEAD_SHIFT = 6
GRID_W = 64
WIN_HEADS, WIN_KV_HEADS, WINDOW = 4, 2, 128
MLA_HEADS, MLA_Q_RANK, MLA_KV_RANK, MLA_NOPE, MLA_ROPE, MLA_V = 4, 256, 128, 64, 32, 64
AX_HEADS, AX_KV_HEADS = 4, 2
DIFF_HEADS, DIFF_QK, DIFF_V = 4, 32, 64
A_COLS = (WIN_HEADS + 2 * WIN_KV_HEADS) * HEAD_DIM
B_COLS = MLA_Q_RANK + MLA_KV_RANK + MLA_ROPE
C_COLS = (AX_HEADS + 2 * AX_KV_HEADS) * HEAD_DIM
D_COLS = DIFF_HEADS * (4 * DIFF_QK + DIFF_V)
ROPE_BASE = 10000.0
NORM_EPS = 1e-5
NEG_INF = -1e30

LANES = 128
F32_SUBLANES = 8
BF16_SUBLANES = 16
V_ROWS = HEAD_DIM + BF16_SUBLANES
VMEM_LIMIT = 56 * 1024 * 1024

TOK_TILE = 512
FFN_TOK_TILE = 1024
KV_CHUNK = 512
Q_TILE = 512
FF_CHUNK = 256
HEADS_PER_STEP = 2
CHUNKS_PER_STEP = 4
ROW_BLOCK = 256
DENSE_ROW_BLOCK = 256
WIN_TILE = 256
assert WIN_TILE >= WINDOW and KV_CHUNK % WIN_TILE == 0


def _cparams(sem, flags=None):
    return pltpu.CompilerParams(dimension_semantics=sem, vmem_limit_bytes=VMEM_LIMIT, flags=flags)


ATTENTION_FLAGS = None


def _resident(shape):
    nd = len(shape)
    return pl.BlockSpec(shape, lambda *_: (0,) * nd, pipeline_mode=pl.Buffered(1))


def _layer_norm_rows(y, g, b):
    mu = jnp.mean(y, axis=-1, keepdims=True)
    yc = y - mu
    var = jnp.mean(yc * yc, axis=-1, keepdims=True)
    return yc * lax.rsqrt(var + NORM_EPS) * g + b


def _ffn_kernel(x_ref, wgu_ref, wd_ref, g_ref, b_ref, o_ref, acc_ref, *, alpha, n_chunks):
    x = x_ref[...]
    xb = x.astype(BF16)
    acc_ref[...] = jnp.zeros_like(acc_ref)

    def body(c, carry):
        gu = jnp.dot(xb, wgu_ref[c], preferred_element_type=F32)
        gate = gu[:, :FF_CHUNK]
        up = gu[:, FF_CHUNK:]
        h = (gate * jax.nn.sigmoid(gate) * up).astype(BF16)
        acc_ref[...] += jnp.dot(h, wd_ref[c], preferred_element_type=F32)
        return carry

    lax.fori_loop(0, n_chunks, body, 0)
    y = alpha * x + 0.5 * acc_ref[...]
    o_ref[...] = _layer_norm_rows(y, g_ref[...], b_ref[...])


def _ffn(x2d, wgu, wd, g, b, alpha):
    n_tok, d = x2d.shape
    n_chunks = wgu.shape[0]
    tm = min(FFN_TOK_TILE, n_tok)
    return pl.pallas_call(
        functools.partial(_ffn_kernel, alpha=alpha, n_chunks=n_chunks),
        grid=(n_tok // tm,),
        in_specs=[
            pl.BlockSpec((tm, d), lambda i: (i, 0)),
            _resident(wgu.shape),
            _resident(wd.shape),
            _resident(g.shape),
            _resident(b.shape),
        ],
        out_specs=pl.BlockSpec((tm, d), lambda i: (i, 0)),
        out_shape=jax.ShapeDtypeStruct((n_tok, d), F32),
        scratch_shapes=[pltpu.VMEM((tm, d), F32)],
        compiler_params=_cparams(("parallel",)),
    )(x2d, wgu, wd, g, b)


_C_AQ, _C_AK, _C_CQ, _C_CK, _C_CQR, _C_CKR = 0, 256, 384, 640, 768, 1024
_C_DQ, _C_DK, _C_BQ, _C_BKV, _C_BKR, _C_BKRR, _C_END = 1152, 1664, 2176, 2432, 2560, 2688, 2816
_R_AV, _R_CV, _R_DV, _R_END = 0, 128, 256, 512


def _split_hi_lo(v):
    hi = v.astype(BF16)
    lo = (v - hi.astype(F32)).astype(BF16)
    return hi, lo


def _prep_kernel(x_ref, wtok_ref, wvt_ref, wuq_ref, wuqr_ref, wukvk_ref, wukvvt_ref, gavg_ref,
                 gq_ref, gkv_ref, axq_ref, axqr_ref, axk_ref, axkr_ref,
                 cosb_ref, sinb_ref, cosc_ref, sinc_ref, augk_ref,
                 qa_ref, ka_ref, vta_ref, qb_ref, kb_ref, vtb_ref,
                 qc_ref, kc_ref, vtc_ref, qd_ref, kd_ref, vtd_ref):
    xb = x_ref[0].astype(BF16)
    tm = xb.shape[0]
    pm = jnp.dot(xb, wtok_ref[...], preferred_element_type=F32)
    nt = (((1,), (1,)), ((), ()))
    pvt = lax.dot_general(wvt_ref[...], xb, nt, preferred_element_type=F32)

    def ones_row_tile(width):
        row = lax.broadcasted_iota(jnp.int32, (BF16_SUBLANES, width), 0)
        return jnp.where(row == 0, 1.0, 0.0).astype(BF16)

    ones_tile, ones_tile_narrow = ones_row_tile(tm), ones_row_tile(WIN_TILE)

    def put_vt(ref, head, vt):
        width = ref.shape[-1]
        ones = ones_tile if width == tm else ones_tile_narrow
        for piece in range(tm // width):
            ref[0, head, piece, 0:HEAD_DIM, :] = vt[:, piece * width:(piece + 1) * width].astype(BF16)
            ref[0, head, piece, HEAD_DIM:V_ROWS, :] = ones

    qa_ref[0] = (pm[:, _C_AQ:_C_AQ + 256] * (HEAD_DIM ** -0.5 * LOG2E)).astype(BF16)
    ka_ref[0] = pm[:, _C_AK:_C_AK + 128].astype(BF16)
    for h in range(WIN_KV_HEADS):
        put_vt(vta_ref, h, pvt[_R_AV + 64 * h:_R_AV + 64 * (h + 1)])

    cosc = cosc_ref[...]
    sinc = sinc_ref[...]
    gavg = gavg_ref[...]

    def head_rms_scale(v):
        hi, lo = _split_hi_lo(v * v)
        ms = (jnp.dot(hi, gavg, preferred_element_type=F32)
              + jnp.dot(lo, gavg, preferred_element_type=F32))
        return lax.rsqrt(ms + NORM_EPS)

    def axial(off, off_rot, g_ref_, gr_ref_, scale):
        v = pm[:, off:off + LANES]
        r = head_rms_scale(v)
        vn = v * r * g_ref_[...]
        vrn = pm[:, off_rot:off_rot + LANES] * r * gr_ref_[...]
        return ((vn * cosc + vrn * sinc) * scale).astype(BF16)

    for grp in range(2):
        qc_ref[0, :, grp * LANES:(grp + 1) * LANES] = axial(
            _C_CQ + grp * LANES, _C_CQR + grp * LANES, axq_ref, axqr_ref, HEAD_DIM ** -0.5 * LOG2E)
    kc_ref[0] = axial(_C_CK, _C_CKR, axk_ref, axkr_ref, 1.0)
    for h in range(AX_KV_HEADS):
        put_vt(vtc_ref, h, pvt[_R_CV + 64 * h:_R_CV + 64 * (h + 1)])

    qd_ref[0] = (pm[:, _C_DQ:_C_DQ + 512] * (DIFF_QK ** -0.5 * LOG2E)).astype(BF16)
    augk = augk_ref[...]
    for h in range(DIFF_HEADS):
        kd_ref[0, :, h * LANES:(h + 1) * LANES] = (
            pm[:, _C_DK + h * LANES:_C_DK + (h + 1) * LANES] + augk).astype(BF16)
        put_vt(vtd_ref, h, pvt[_R_DV + 64 * h:_R_DV + 64 * (h + 1)])

    cosb = cosb_ref[...]
    sinb = sinb_ref[...]
    cq = pm[:, _C_BQ:_C_BQ + MLA_Q_RANK]
    cqn = (cq * lax.rsqrt(jnp.mean(cq * cq, axis=-1, keepdims=True) + NORM_EPS) * gq_ref[...]).astype(BF16)
    qw = jnp.dot(cqn, wuq_ref[...], preferred_element_type=F32)
    qwr = jnp.dot(cqn, wuqr_ref[...], preferred_element_type=F32)
    ckv = pm[:, _C_BKV:_C_BKV + MLA_KV_RANK]
    ckvn = (ckv * lax.rsqrt(jnp.mean(ckv * ckv, axis=-1, keepdims=True) + NORM_EPS) * gkv_ref[...]).astype(BF16)
    kw = jnp.dot(ckvn, wukvk_ref[...], preferred_element_type=F32)
    vbt = lax.dot_general(wukvvt_ref[...], ckvn, nt, preferred_element_type=F32)
    k_rope = pm[:, _C_BKR:_C_BKR + LANES] * cosb + pm[:, _C_BKRR:_C_BKRR + LANES] * sinb
    q_scale = (MLA_NOPE + MLA_ROPE) ** -0.5 * LOG2E
    for h in range(MLA_HEADS):
        sl = slice(h * LANES, (h + 1) * LANES)
        qb_ref[0, :, sl] = ((qw[:, sl] * cosb + qwr[:, sl] * sinb) * q_scale).astype(BF16)
        kb_ref[0, :, sl] = (kw[:, sl] + k_rope).astype(BF16)
        put_vt(vtb_ref, h, vbt[64 * h:64 * (h + 1)])


def _prep(x, p, tabs):
    bsz, seq, d = x.shape
    tm = KV_CHUNK
    nch = seq // tm
    tok = lambda w: pl.BlockSpec((1, tm, w), lambda b, i: (b, i, 0))
    tab = pl.BlockSpec((tm, LANES), lambda b, i: (i, 0))
    vt = lambda nh, width=tm: pl.BlockSpec((1, nh, tm // width, V_ROWS, width), lambda b, i: (b, 0, i, 0, 0))
    vt_shape = lambda nh, width=tm: jax.ShapeDtypeStruct((bsz, nh, seq // width, V_ROWS, width), BF16)
    tok_shape = lambda w: jax.ShapeDtypeStruct((bsz, seq, w), BF16)
    weights = [p["w_tok"], p["w_vt"], p["w_uq"], p["w_uq_rot"], p["w_ukv_k"], p["w_ukv_vt"], p["g_avg"],
               p["g_q"], p["g_kv"], p["ax_q"], p["ax_q_rot"], p["ax_k"], p["ax_k_rot"]]
    return pl.pallas_call(
        _prep_kernel,
        grid=(bsz, nch),
        in_specs=[tok(d)] + [_resident(w.shape) for w in weights] + [tab] * 5,
        out_specs=[tok(256), tok(128), vt(2, WIN_TILE), tok(512), tok(512), vt(4),
                   tok(256), tok(128), vt(2), tok(512), tok(512), vt(4)],
        out_shape=[tok_shape(256), tok_shape(128), vt_shape(2, WIN_TILE), tok_shape(512), tok_shape(512), vt_shape(4),
                   tok_shape(256), tok_shape(128), vt_shape(2), tok_shape(512), tok_shape(512), vt_shape(4)],
        compiler_params=_cparams(("parallel", "parallel")),
    )(x, *weights, tabs["cos_b"], tabs["sin_b"], tabs["cos_c"], tabs["sin_c"], tabs["aug_k"])


_NT = (((1,), (1,)), ((), ()))
LOG2E = math.log2(math.e)


def _bf16_parts(x, n):
    parts = []
    for _ in range(n):
        part = float(np.float32(x).astype(jnp.bfloat16).astype(np.float32))
        parts.append(part)
        x -= part
    return tuple(parts)


LOG2E_PARTS = _bf16_parts(LOG2E, 3)


def _score_block(k_ref, q, k_slab, rb, row0, s_write, n, r, hooks):
    start = row0 + r * rb
    if not isinstance(start, int):
        start = pl.multiple_of(start, rb)
    kc = k_ref[0, pl.ds(start, rb), k_slab * LANES:(k_slab + 1) * LANES]
    if hooks is not None:
        kc = hooks[0](kc, start // rb)
    s = lax.dot_general(kc, q, _NT, preferred_element_type=F32)
    if hooks is not None:
        s = hooks[1](s, n, start // rb)
    s_write[n, r * rb:(r + 1) * rb, :] = s
    return jnp.max(s.reshape(rb // F32_SUBLANES, F32_SUBLANES, s.shape[-1]), axis=0)


def _first_scores(k_ref, chains, cps, rb, s_write, hooks):
    out = []
    for n, (q, k_slab, _) in enumerate(chains):
        parts = [_score_block(k_ref, q, k_slab, rb, 0, s_write, n, r, hooks) for r in range(cps * KV_CHUNK // rb)]
        out.append(functools.reduce(jnp.maximum, parts))
    return out


def _pipelined_step(k_ref, vt_ref, chains, exp_fn, cps, rb, hooks, j, carry, s_read, s_write):
    step_rows = cps * KV_CHUNK
    blocks_per_chunk = KV_CHUNK // rb
    next_row0 = (j + 1) * step_rows
    out = []
    for n, ((q, k_slab, v_head), (m, acc, smax)) in enumerate(zip(chains, carry)):
        m_new = jnp.maximum(m, jnp.max(smax, axis=0, keepdims=True))
        alpha = exp_fn(m - m_new)
        next_max, pv = smax, None
        for r in range(step_rows // rb):
            if s_write is not None:
                part = _score_block(k_ref, q, k_slab, rb, next_row0, s_write, n, r, hooks)
                next_max = part if r == 0 else jnp.maximum(next_max, part)
            p = exp_fn(s_read[n, r * rb:(r + 1) * rb, :] - m_new).astype(BF16)
            c, b = divmod(r, blocks_per_chunk)
            part = jnp.dot(vt_ref[0, v_head, j * cps + c, :, b * rb:(b + 1) * rb], p, preferred_element_type=F32)
            pv = part if pv is None else pv + part
        out.append((m_new, acc * alpha + pv, next_max))
    return tuple(out)


def _pipelined_attention(k_ref, vt_ref, chains, exp_fn, n_chunks, rb, s0_ref, s1_ref, hooks=None):
    nq = chains[0][0].shape[0]
    cps = s0_ref.shape[1] // KV_CHUNK
    n_steps = n_chunks // cps
    step = functools.partial(_pipelined_step, k_ref, vt_ref, chains, exp_fn, cps, rb, hooks)
    first_max = _first_scores(k_ref, chains, cps, rb, s0_ref, hooks)
    init = tuple((jnp.full((1, nq), NEG_INF, F32), jnp.zeros((V_ROWS, nq), F32), smax) for smax in first_max)

    def pair(t, carry):
        carry = step(2 * t, carry, s0_ref, s1_ref)
        return step(2 * t + 1, carry, s1_ref, s0_ref)

    carry = lax.fori_loop(0, n_steps // 2 - 1, pair, init)
    carry = step(n_steps - 2, carry, s0_ref, s1_ref)
    final = step(n_steps - 1, carry, s1_ref, None)
    return [acc for _, acc, _ in final]


def _normalised(acc):
    return acc[:HEAD_DIM] / acc[HEAD_DIM:HEAD_DIM + 1]


def _single_buffered(block, index_map):
    return pl.BlockSpec(block, index_map, pipeline_mode=pl.Buffered(1))


def _dense_kernel(q_ref, k_ref, vt_ref, o_ref, s0_ref, s1_ref, *, n_chunks, heads):
    tq = q_ref.shape[1]
    lane = lax.broadcasted_iota(jnp.int32, (tq, LANES), 1)
    chains = []
    for q_slab, k_slab, v_head, shared_key_slab in heads:
        q = q_ref[0, :, q_slab * LANES:(q_slab + 1) * LANES]
        if shared_key_slab:
            q = jnp.where((lane >> HEAD_SHIFT) == pl.program_id(1), q.astype(F32), 0.0).astype(BF16)
        chains.append((q, k_slab, v_head))
    accs = _pipelined_attention(k_ref, vt_ref, chains, jnp.exp2, n_chunks, DENSE_ROW_BLOCK, s0_ref, s1_ref)
    for n, acc in enumerate(accs):
        o_ref[0, n * HEAD_DIM:(n + 1) * HEAD_DIM, :] = _normalised(acc).astype(BF16)


def _dense_attention(q, k, vt, *, n_heads, kv_group, heads_per_step):
    bsz, seq, _ = q.shape
    nch = vt.shape[2]
    tq = min(Q_TILE, seq)
    hp = heads_per_step
    if kv_group:
        assert hp == kv_group and n_heads // kv_group == LANES // HEAD_DIM
        heads = tuple((s, 0, 0, True) for s in range(kv_group))
        q_spec = pl.BlockSpec((1, tq, kv_group * LANES), lambda b, g, i: (b, i, 0))
        k_spec = pl.BlockSpec((1, seq, LANES), lambda b, g, i: (b, 0, 0))
        vt_spec = pl.BlockSpec((1, 1, nch, V_ROWS, KV_CHUNK), lambda b, g, i: (b, g, 0, 0, 0))
    else:
        heads = tuple((n, n, n, False) for n in range(hp))
        q_spec = pl.BlockSpec((1, tq, hp * LANES), lambda b, g, i: (b, i, g))
        k_spec = pl.BlockSpec((1, seq, hp * LANES), lambda b, g, i: (b, 0, g))
        vt_spec = pl.BlockSpec((1, hp, nch, V_ROWS, KV_CHUNK), lambda b, g, i: (b, g, 0, 0, 0))
    assert nch % 2 == 0
    cps = math.gcd(CHUNKS_PER_STEP, nch // 2)
    s_buffer = pltpu.VMEM((hp, cps * KV_CHUNK, tq), F32)
    return pl.pallas_call(
        functools.partial(_dense_kernel, n_chunks=nch, heads=heads),
        grid=(bsz, n_heads // hp, seq // tq),
        in_specs=[q_spec, k_spec, vt_spec],
        out_specs=pl.BlockSpec((1, hp * HEAD_DIM, tq), lambda b, g, i: (b, g, i)),
        out_shape=jax.ShapeDtypeStruct((bsz, n_heads * HEAD_DIM, seq), BF16),
        scratch_shapes=[s_buffer, s_buffer],
        compiler_params=_cparams(("parallel", "parallel", "arbitrary"), ATTENTION_FLAGS),
    )(q, k, vt)


_AUG = 2 * DIFF_QK


def _diff_kernel(lam_ref, g_ref, q_ref, k_ref, vt_ref, o_ref, s0_ref, s1_ref, bias_ref, *,
                 n_chunks, hp, lam_init, slope_exp0):
    i = pl.program_id(2)
    tq = q_ref.shape[1]
    lane = lax.broadcasted_iota(jnp.int32, (tq, LANES), 1)
    digit = (lane - _AUG) & 3
    part = (lane - _AUG) >> 2
    is_bias_lane = (lane >= _AUG) & (lane < _AUG + 4 * len(LOG2E_PARTS))
    log2e_part = functools.reduce(lambda acc, x: jnp.where(part == x[0], x[1], acc),
                                  enumerate(LOG2E_PARTS), jnp.zeros((tq, LANES), F32))
    qpos = i * tq + lax.broadcasted_iota(jnp.int32, (tq, LANES), 0)
    q_hi = ((qpos >> 7) << 7).astype(F32)
    q_lo = (qpos & 127).astype(F32)
    own_dist = jnp.abs(lax.broadcasted_iota(jnp.int32, (ROW_BLOCK, tq), 1)
                       - lax.broadcasted_iota(jnp.int32, (ROW_BLOCK, tq), 0)).astype(F32)

    chains = []
    for n in range(hp):
        head = pl.program_id(1) * hp + n
        slope_bits = jnp.full((1, LANES), 127 - slope_exp0, jnp.int32) - head
        slope = lax.bitcast_convert_type(slope_bits << 23, F32)
        qf = q_ref[0, :, n * LANES:(n + 1) * LANES].astype(F32)
        aug = jnp.where(digit == 0, -slope * q_hi, jnp.where(digit == 1, -slope * q_lo, slope * log2e_part))
        aug = jnp.where(is_bias_lane, aug, 0.0)
        q1 = jnp.where(lane < DIFF_QK, qf, 0.0)
        q2 = jnp.where((lane >= DIFF_QK) & (lane < 2 * DIFF_QK), qf, 0.0)
        chains.append((jnp.concatenate([q1 + aug, q2 + aug], axis=0).astype(BF16), n, n))
        bias = (-LOG2E) * slope[:, :1] * own_dist
        bias_ref[n, 0] = jnp.zeros((ROW_BLOCK, 2 * tq), F32)
        bias_ref[n, 1] = jnp.concatenate([bias, bias], axis=1)

    key_lane = lax.broadcasted_iota(jnp.int32, (1, LANES), 1)
    key_bias_lane = (key_lane >= _AUG) & (key_lane < _AUG + 4 * len(LOG2E_PARTS))

    def edit_keys(kc, block):
        sign = jnp.where(block < i, 1.0, jnp.where(block > i, -1.0, 0.0))
        return kc * jnp.where(key_bias_lane, sign, 1.0).astype(BF16)

    def add_bias(s, n, block):
        return s + bias_ref[n, jnp.where(block == i, 1, 0)]

    accs = _pipelined_attention(k_ref, vt_ref, chains, jnp.exp2, n_chunks, ROW_BLOCK, s0_ref, s1_ref,
                                (edit_keys, add_bias))

    lp = lam_ref[...]
    lam = (jnp.exp(jnp.sum(lp[0:1] * lp[1:2], axis=-1, keepdims=True))
           - jnp.exp(jnp.sum(lp[2:3] * lp[3:4], axis=-1, keepdims=True)) + lam_init)
    for n, acc in enumerate(accs):
        o = _normalised(acc[:, :tq]) - lam * _normalised(acc[:, tq:])
        o = o * lax.rsqrt(jnp.mean(o * o, axis=0, keepdims=True) + NORM_EPS)
        o_ref[0, n * DIFF_V:(n + 1) * DIFF_V, :] = (o * g_ref[...] * (1.0 - lam_init)).astype(BF16)


def _diff_attention(q, k, vt, lam_params, subln_col, *, layer_idx, slope_exp0, heads_per_step):
    bsz, seq, _ = q.shape
    nch = vt.shape[2]
    tq = ROW_BLOCK
    hp = heads_per_step
    assert seq % tq == 0 and seq <= 256 * LANES and DIFF_HEADS % hp == 0 and nch % 2 == 0
    lam_init = 0.8 - 0.6 * math.exp(-0.3 * layer_idx)
    cps = math.gcd(CHUNKS_PER_STEP, nch // 2)
    s_buffer = pltpu.VMEM((hp, cps * KV_CHUNK, 2 * tq), F32)
    return pl.pallas_call(
        functools.partial(_diff_kernel, n_chunks=nch, hp=hp, lam_init=lam_init, slope_exp0=slope_exp0),
        grid=(bsz, DIFF_HEADS // hp, seq // tq),
        in_specs=[
            _resident(lam_params.shape),
            _resident(subln_col.shape),
            pl.BlockSpec((1, tq, hp * LANES), lambda b, g, i: (b, i, g)),
            pl.BlockSpec((1, seq, hp * LANES), lambda b, g, i: (b, 0, g)),
            pl.BlockSpec((1, hp, nch, V_ROWS, KV_CHUNK), lambda b, g, i: (b, g, 0, 0, 0)),
        ],
        out_specs=pl.BlockSpec((1, hp * DIFF_V, tq), lambda b, g, i: (b, g, i)),
        out_shape=jax.ShapeDtypeStruct((bsz, DIFF_HEADS * DIFF_V, seq), BF16),
        scratch_shapes=[s_buffer, s_buffer, pltpu.VMEM((hp, 2, ROW_BLOCK, 2 * tq), F32)],
        compiler_params=_cparams(("parallel", "parallel", "arbitrary"), ATTENTION_FLAGS),
    )(lam_params, subln_col, q, k, vt)


_WIN_OFFSETS = (-1, 0, 1)
_WIN_NO_BLOCK = len(_WIN_OFFSETS)


def _window_bias_tiles():
    key = np.arange(WIN_TILE)[:, None]
    query = np.arange(WIN_TILE)[None, :]
    tiles = np.full((WIN_HEADS, _WIN_NO_BLOCK + 1, WIN_TILE, WIN_TILE), NEG_INF, np.float32)
    for head in range(WIN_HEADS):
        for sel, offset in enumerate(_WIN_OFFSETS):
            dist = np.abs(query - key - WIN_TILE * offset)
            tiles[head, sel] = np.where(dist <= WINDOW, -(2.0 ** -(head + 1)) * LOG2E * dist, NEG_INF)
    return tiles


def _window_kernel(sink_ref, bias_ref, q_ref, k_ref, vt_ref, o_ref, *, n_blocks):
    i = pl.program_id(1)
    lane = lax.broadcasted_iota(jnp.int32, (WIN_TILE, LANES), 1)
    group = WIN_HEADS // WIN_KV_HEADS
    blocks = []
    for offset in _WIN_OFFSETS:
        block = i + offset
        exists = (block >= 0) & (block < n_blocks)
        block = jnp.clip(block, 0, n_blocks - 1)
        keys = k_ref[0, pl.ds(pl.multiple_of(block * WIN_TILE, WIN_TILE), WIN_TILE), :]
        blocks.append((jnp.where(exists, offset + 1, _WIN_NO_BLOCK), block, keys))
    scores = []
    for head in range(WIN_HEADS):
        kv_head, slab = head // group, head % group
        qf = q_ref[0, :, slab * LANES:(slab + 1) * LANES].astype(F32)
        q = jnp.where((lane >> HEAD_SHIFT) == kv_head, qf, 0.0).astype(BF16)
        scores.append([lax.dot_general(keys, q, _NT, preferred_element_type=F32) + bias_ref[head, sel]
                       for sel, _, keys in blocks])
    probs = []
    for head in range(WIN_HEADS):
        sink = jnp.full((1, WIN_TILE), sink_ref[head] * LOG2E, F32)
        m = functools.reduce(jnp.maximum, [jnp.max(s, axis=0, keepdims=True) for s in scores[head]], sink)
        probs.append(([jnp.exp2(s - m).astype(BF16) for s in scores[head]], jnp.exp2(sink - m)))
    for head in range(WIN_HEADS):
        acc = None
        for p, (_, block, _) in zip(probs[head][0], blocks):
            part = jnp.dot(vt_ref[0, head // group, block], p, preferred_element_type=F32)
            acc = part if acc is None else acc + part
        denominator = acc[HEAD_DIM:HEAD_DIM + 1] + probs[head][1]
        o_ref[0, head * HEAD_DIM:(head + 1) * HEAD_DIM, :] = (acc[:HEAD_DIM] / denominator).astype(BF16)


def _window_attention(q, k, vt, sink):
    bsz, seq, _ = q.shape
    n_blocks = vt.shape[2]
    bias = jnp.asarray(_window_bias_tiles())
    return pl.pallas_call(
        functools.partial(_window_kernel, n_blocks=n_blocks),
        grid=(bsz, n_blocks),
        in_specs=[
            pl.BlockSpec(memory_space=pltpu.SMEM),
            _resident(bias.shape),
            pl.BlockSpec((1, WIN_TILE, 2 * LANES), lambda b, i: (b, i, 0)),
            _single_buffered((1, seq, LANES), lambda b, i: (b, 0, 0)),
            _single_buffered((1, WIN_KV_HEADS, n_blocks, V_ROWS, WIN_TILE), lambda b, i: (b, 0, 0, 0, 0)),
        ],
        out_specs=pl.BlockSpec((1, WIN_HEADS * HEAD_DIM, WIN_TILE), lambda b, i: (b, 0, i)),
        out_shape=jax.ShapeDtypeStruct((bsz, WIN_HEADS * HEAD_DIM, seq), BF16),
        compiler_params=_cparams(("parallel", "arbitrary")),
    )(sink, bias, q, k, vt)


def _outproj_kernel(x_ref, oa_ref, ob_ref, oc_ref, od_ref, wt_ref, g_ref, b_ref, o_ref, *, alpha):
    mix_t = None
    for m, ref in enumerate((oa_ref, ob_ref, oc_ref, od_ref)):
        part = jnp.dot(wt_ref[:, m * 256:(m + 1) * 256], ref[0], preferred_element_type=F32)
        mix_t = part if mix_t is None else mix_t + part
    y = alpha * x_ref[0] + mix_t.T
    o_ref[0] = _layer_norm_rows(y, g_ref[...], b_ref[...])


def _outproj(x, o_a, o_b, o_c, o_d, w_out_t, g, b, alpha):
    bsz, seq, d = x.shape
    tm = min(TOK_TILE, seq)
    ot = pl.BlockSpec((1, 256, tm), lambda bb, i: (bb, 0, i))
    tok = pl.BlockSpec((1, tm, d), lambda bb, i: (bb, i, 0))
    return pl.pallas_call(
        functools.partial(_outproj_kernel, alpha=alpha),
        grid=(bsz, seq // tm),
        in_specs=[tok, ot, ot, ot, ot, _resident(w_out_t.shape), _resident(g.shape), _resident(b.shape)],
        out_specs=tok,
        out_shape=jax.ShapeDtypeStruct((bsz, seq, d), F32),
        compiler_params=_cparams(("parallel", "parallel")),
    )(x, o_a, o_b, o_c, o_d, w_out_t, g, b)


def _rot_half_cols(start, width):
    half = width // 2
    src = np.concatenate([np.arange(start + half, start + width), np.arange(start, start + half)])
    sgn = np.concatenate([-np.ones(half), np.ones(half)])
    return src, sgn


def _gqa_slab_order(n_heads, n_kv):
    group = n_heads // n_kv
    return [kv * group + s for s in range(group) for kv in range(n_kv)]


def _layer_params(w_in, mla_q_norm, mla_w_uq, mla_kv_norm, mla_w_ukv, ax_q_norm, ax_k_norm):
    d = w_in.shape[0]
    zeros = lambda n: jnp.zeros((d, n), w_in.dtype)
    a0, b0, c0, d0 = 0, A_COLS, A_COLS + B_COLS, A_COLS + B_COLS + C_COLS
    head_cols = lambda base, h: np.arange(base + h * HEAD_DIM, base + (h + 1) * HEAD_DIM)

    def axial_rot(cols):
        src, sgn = [], []
        for blk in range(0, HEAD_DIM, HEAD_DIM // 2):
            s_, g_ = _rot_half_cols(blk, HEAD_DIM // 2)
            src.append(cols[s_])
            sgn.append(g_)
        return np.concatenate(src), np.concatenate(sgn)

    order = _gqa_slab_order(AX_HEADS, AX_KV_HEADS)
    aq = np.concatenate([head_cols(a0, h) for h in order])
    ak = np.arange(a0 + 256, a0 + 384)
    cq = np.concatenate([head_cols(c0, h) for h in order])
    ck = np.arange(c0 + 256, c0 + 384)
    cq_rot = [axial_rot(head_cols(c0, h)) for h in order]
    ck_rot = [axial_rot(head_cols(c0 + 256, h)) for h in range(AX_KV_HEADS)]

    def gather(src, sgn=None):
        w = w_in[:, np.asarray(src)]
        return w if sgn is None else w * jnp.asarray(sgn, w.dtype)[None, :]

    def widen(cols):
        return jnp.concatenate([gather(cols), zeros(LANES - len(cols))], axis=1)

    kr = np.arange(b0 + MLA_Q_RANK + MLA_KV_RANK, b0 + B_COLS)
    kr_src, kr_sgn = _rot_half_cols(kr[0], MLA_ROPE)
    place_rope = lambda w: jnp.concatenate([zeros(MLA_NOPE), w, zeros(LANES - MLA_NOPE - MLA_ROPE)], axis=1)

    w_tok = jnp.concatenate(
        [gather(aq), gather(ak), gather(cq), gather(ck)]
        + [gather(s, g) for s, g in cq_rot] + [gather(s, g) for s, g in ck_rot]
        + [widen(head_cols(d0, h)) for h in range(DIFF_HEADS)]
        + [widen(head_cols(d0 + 256, h)) for h in range(DIFF_HEADS)]
        + [gather(np.arange(b0, b0 + MLA_Q_RANK + MLA_KV_RANK)),
           place_rope(gather(kr)), place_rope(gather(kr_src, kr_sgn))], axis=1)
    assert w_tok.shape[1] == _C_END
    v_cols = np.concatenate([np.arange(a0 + 384, a0 + 512), np.arange(c0 + 384, c0 + 512),
                             np.arange(d0 + 512, d0 + 768)])
    w_vt = w_in[:, v_cols].T

    qd = MLA_NOPE + MLA_ROPE
    zq = lambda n: jnp.zeros((MLA_Q_RANK, n), mla_w_uq.dtype)
    uq, uq_rot = [], []
    for h in range(MLA_HEADS):
        blk = mla_w_uq[:, h * qd:(h + 1) * qd]
        src, sgn = _rot_half_cols(MLA_NOPE, MLA_ROPE)
        rot = blk[:, src] * jnp.asarray(sgn, blk.dtype)[None, :]
        uq += [blk, zq(LANES - qd)]
        uq_rot += [zq(MLA_NOPE), rot, zq(LANES - qd)]
    ukv = mla_w_ukv.reshape(MLA_KV_RANK, MLA_HEADS, MLA_NOPE + MLA_V)
    ukv_k = jnp.concatenate([ukv[:, :, :MLA_NOPE], jnp.zeros((MLA_KV_RANK, MLA_HEADS, LANES - MLA_NOPE), ukv.dtype)],
                            axis=2).reshape(MLA_KV_RANK, MLA_HEADS * LANES)
    ukv_vt = ukv[:, :, MLA_NOPE:].reshape(MLA_KV_RANK, MLA_HEADS * MLA_V).T

    def ax_gain(g):
        src, sgn = axial_rot(np.arange(HEAD_DIM))
        return jnp.tile(g, 2)[None, :].astype(F32), jnp.tile(g[src], 2)[None, :].astype(F32)

    ax_q, ax_q_rot = ax_gain(ax_q_norm)
    ax_k, ax_k_rot = ax_gain(ax_k_norm)
    lane = np.arange(LANES)
    g_avg = (lane[:, None] // HEAD_DIM == lane[None, :] // HEAD_DIM).astype(np.float32) / HEAD_DIM
    return {
        "w_tok": w_tok.astype(BF16), "w_vt": w_vt.astype(BF16),
        "w_uq": jnp.concatenate(uq, axis=1).astype(BF16), "w_uq_rot": jnp.concatenate(uq_rot, axis=1).astype(BF16),
        "w_ukv_k": ukv_k.astype(BF16), "w_ukv_vt": ukv_vt.astype(BF16),
        "g_avg": jnp.asarray(g_avg, BF16),
        "g_q": mla_q_norm[None, :].astype(F32), "g_kv": mla_kv_norm[None, :].astype(F32),
        "ax_q": ax_q, "ax_q_rot": ax_q_rot, "ax_k": ax_k, "ax_k_rot": ax_k_rot,
    }


def _position_tables(seq):
    pos = jnp.arange(seq, dtype=jnp.int32)
    half = HEAD_DIM // 2

    def angles(p, dim):
        inv = ROPE_BASE ** (-jnp.arange(0, dim, 2, dtype=F32) / dim)
        ang = p.astype(F32)[:, None] * inv[None, :]
        return jnp.cos(ang), jnp.sin(ang)

    cb, sb = angles(pos, MLA_ROPE)
    pad = LANES - MLA_NOPE - MLA_ROPE
    cos_b = jnp.concatenate([jnp.ones((seq, MLA_NOPE), F32), cb, cb, jnp.zeros((seq, pad), F32)], axis=1)
    sin_b = jnp.concatenate([jnp.zeros((seq, MLA_NOPE), F32), sb, sb, jnp.zeros((seq, pad), F32)], axis=1)
    cr, sr = angles(pos // GRID_W, half)
    cc, sc = angles(pos % GRID_W, half)
    cos_c = jnp.tile(jnp.concatenate([cr, cr, cc, cc], axis=1), (1, 2))
    sin_c = jnp.tile(jnp.concatenate([sr, sr, sc, sc], axis=1), (1, 2))
    rel = np.arange(LANES) - _AUG
    in_aug = (rel >= 0) & (rel < 4 * len(LOG2E_PARTS))
    part_row = np.where(in_aug, np.asarray(LOG2E_PARTS + (0.0,), np.float32)[np.clip(rel >> 2, 0, len(LOG2E_PARTS))], 0.0)
    digit = jnp.asarray((rel & 3)[None, :])
    pos_hi = ((pos >> 7) << 7).astype(F32)[:, None]
    pos_lo = (pos & 127).astype(F32)[:, None]
    aug = jnp.where(digit < 2, jnp.asarray(part_row, F32)[None, :], jnp.where(digit == 2, pos_hi, pos_lo))
    aug = jnp.where(jnp.asarray(in_aug[None, :]), aug, 0.0)
    return {"cos_b": cos_b, "sin_b": sin_b, "cos_c": cos_c, "sin_c": sin_c, "aug_k": aug}


def _ffn_weights(w_gu, w_down):
    d, two_ff = w_gu.shape
    d_ff = two_ff // 2
    n_chunks = d_ff // FF_CHUNK
    wgu = w_gu.astype(BF16).reshape(d, 2, n_chunks, FF_CHUNK).transpose(2, 0, 1, 3).reshape(n_chunks, d, 2 * FF_CHUNK)
    wd = w_down.astype(BF16).reshape(n_chunks, FF_CHUNK, d)
    return wgu, wd


def kernel(x, w_in, win_sink, mla_q_norm, mla_w_uq, mla_kv_norm, mla_w_ukv, ax_q_norm, ax_k_norm,
           diff_lambda, diff_subln, w_out, ffn_w_gu, ffn_w_down, ln_g, ln_b):
    depth = w_in.shape[0]
    bsz, seq, d = x.shape
    alpha = (2 * depth) ** 0.25
    tabs = _position_tables(seq)
    row = lambda v: v[None, :].astype(F32)
    for l in range(depth):
        wgu, wd = _ffn_weights(ffn_w_gu[l, 0], ffn_w_down[l, 0])
        x = _ffn(x.reshape(bsz * seq, d), wgu, wd, row(ln_g[l, 0]), row(ln_b[l, 0]), alpha).reshape(bsz, seq, d)

        p = _layer_params(w_in[l], mla_q_norm[l], mla_w_uq[l], mla_kv_norm[l], mla_w_ukv[l],
                          ax_q_norm[l], ax_k_norm[l])
        qa, ka, vta, qb, kb, vtb, qc, kc, vtc, qd, kd, vtd = _prep(x, p, tabs)
        o_a = _window_attention(qa, ka, vta, win_sink[l].astype(F32))
        o_b = _dense_attention(qb, kb, vtb, n_heads=MLA_HEADS, kv_group=0, heads_per_step=HEADS_PER_STEP)
        o_c = _dense_attention(qc, kc, vtc, n_heads=AX_HEADS, kv_group=AX_HEADS // AX_KV_HEADS,
                               heads_per_step=AX_HEADS // AX_KV_HEADS)
        o_d = _diff_attention(qd, kd, vtd, diff_lambda[l].astype(F32), diff_subln[l][:, None].astype(F32),
                              layer_idx=l, slope_exp0=WIN_HEADS + 1, heads_per_step=HEADS_PER_STEP)
        x = _outproj(x, o_a, o_b, o_c, o_d, w_out[l].T.astype(BF16), row(ln_g[l, 1]), row(ln_b[l, 1]), alpha)

        wgu, wd = _ffn_weights(ffn_w_gu[l, 1], ffn_w_down[l, 1])
        x = _ffn(x.reshape(bsz * seq, d), wgu, wd, row(ln_g[l, 2]), row(ln_b[l, 2]), alpha).reshape(bsz, seq, d)
    return x
```

```python
import functools
import math

import numpy as np
import jax
import jax.numpy as jnp
from jax import lax
from jax.experimental import pallas as pl
from jax.experimental.pallas import tpu as pltpu

F32 = jnp.float32
BF16 = jnp.bfloat16

HEAD_DIM = 64
HEAD_SHIFT = 6
GRID_W = 64
WIN_HEADS, WIN_KV_HEADS, WINDOW = 4, 2, 128
MLA_HEADS, MLA_Q_RANK, MLA_KV_RANK, MLA_NOPE, MLA_ROPE, MLA_V = 4, 256, 128, 64, 32, 64
AX_HEADS, AX_KV_HEADS = 4, 2
DIFF_HEADS, DIFF_QK, DIFF_V = 4, 32, 64
A_COLS = (WIN_HEADS + 2 * WIN_KV_HEADS) * HEAD_DIM
B_COLS = MLA_Q_RANK + MLA_KV_RANK + MLA_ROPE
C_COLS = (AX_HEADS + 2 * AX_KV_HEADS) * HEAD_DIM
D_COLS = DIFF_HEADS * (4 * DIFF_QK + DIFF_V)
ROPE_BASE = 10000.0
NORM_EPS = 1e-5
NEG_INF = -1e30

LANES = 128
F32_SUBLANES = 8
BF16_SUBLANES = 16
V_ROWS = HEAD_DIM + BF16_SUBLANES
VMEM_LIMIT = 56 * 1024 * 1024

TOK_TILE = 512
FFN_TOK_TILE = 1024
KV_CHUNK = 512
Q_TILE = 512
FF_CHUNK = 256
HEADS_PER_STEP = 2
CHUNKS_PER_STEP = 4
ROW_BLOCK = 256
DENSE_ROW_BLOCK = 256
WIN_TILE = 256
assert WIN_TILE >= WINDOW and KV_CHUNK % WIN_TILE == 0


def _cparams(sem, flags=None):
    return pltpu.CompilerParams(dimension_semantics=sem, vmem_limit_bytes=VMEM_LIMIT, flags=flags)


ATTENTION_FLAGS = None


def _resident(shape):
    nd = len(shape)
    return pl.BlockSpec(shape, lambda *_: (0,) * nd, pipeline_mode=pl.Buffered(1))


def _layer_norm_rows(y, g, b):
    mu = jnp.mean(y, axis=-1, keepdims=True)
    yc = y - mu
    var = jnp.mean(yc * yc, axis=-1, keepdims=True)
    return yc * lax.rsqrt(var + NORM_EPS) * g + b


def _ffn_kernel(x_ref, wgu_ref, wd_ref, g_ref, b_ref, o_ref, acc_ref, *, alpha, n_chunks):
    x = x_ref[...]
    xb = x.astype(BF16)
    acc_ref[...] = jnp.zeros_like(acc_ref)

    def body(c, carry):
        gu = jnp.dot(xb, wgu_ref[c], preferred_element_type=F32)
        gate = gu[:, :FF_CHUNK]
        up = gu[:, FF_CHUNK:]
        h = (gate * jax.nn.sigmoid(gate) * up).astype(BF16)
        acc_ref[...] += jnp.dot(h, wd_ref[c], preferred_element_type=F32)
        return carry

    lax.fori_loop(0, n_chunks, body, 0)
    y = alpha * x + 0.5 * acc_ref[...]
    o_ref[...] = _layer_norm_rows(y, g_ref[...], b_ref[...])


def _ffn(x2d, wgu, wd, g, b, alpha):
    n_tok, d = x2d.shape
    n_chunks = wgu.shape[0]
    tm = min(FFN_TOK_TILE, n_tok)
    return pl.pallas_call(
        functools.partial(_ffn_kernel, alpha=alpha, n_chunks=n_chunks),
        grid=(n_tok // tm,),
        in_specs=[
            pl.BlockSpec((tm, d), lambda i: (i, 0)),
            _resident(wgu.shape),
            _resident(wd.shape),
            _resident(g.shape),
            _resident(b.shape),
        ],
        out_specs=pl.BlockSpec((tm, d), lambda i: (i, 0)),
        out_shape=jax.ShapeDtypeStruct((n_tok, d), F32),
        scratch_shapes=[pltpu.VMEM((tm, d), F32)],
        compiler_params=_cparams(("parallel",)),
    )(x2d, wgu, wd, g, b)


_C_AQ, _C_AK, _C_CQ, _C_CK, _C_CQR, _C_CKR = 0, 256, 384, 640, 768, 1024
_C_DQ, _C_DK, _C_BQ, _C_BKV, _C_BKR, _C_BKRR, _C_END = 1152, 1664, 2176, 2432, 2560, 2688, 2816
_R_AV, _R_CV, _R_DV, _R_END = 0, 128, 256, 512


def _split_hi_lo(v):
    hi = v.astype(BF16)
    lo = (v - hi.astype(F32)).astype(BF16)
    return hi, lo


def _prep_kernel(x_ref, wtok_ref, wvt_ref, wuq_ref, wuqr_ref, wukvk_ref, wukvvt_ref, gavg_ref,
                 gq_ref, gkv_ref, axq_ref, axqr_ref, axk_ref, axkr_ref,
                 cosb_ref, sinb_ref, cosc_ref, sinc_ref, augk_ref,
                 qa_ref, ka_ref, vta_ref, qb_ref, kb_ref, vtb_ref,
                 qc_ref, kc_ref, vtc_ref, qd_ref, kd_ref, vtd_ref):
    xb = x_ref[0].astype(BF16)
    tm = xb.shape[0]
    pm = jnp.dot(xb, wtok_ref[...], preferred_element_type=F32)
    nt = (((1,), (1,)), ((), ()))
    pvt = lax.dot_general(wvt_ref[...], xb, nt, preferred_element_type=F32)

    def ones_row_tile(width):
        row = lax.broadcasted_iota(jnp.int32, (BF16_SUBLANES, width), 0)
        return jnp.where(row == 0, 1.0, 0.0).astype(BF16)

    ones_tile, ones_tile_narrow = ones_row_tile(tm), ones_row_tile(WIN_TILE)

    def put_vt(ref, head, vt):
        width = ref.shape[-1]
        ones = ones_tile if width == tm else ones_tile_narrow
        for piece in range(tm // width):
            ref[0, head, piece, 0:HEAD_DIM, :] = vt[:, piece * width:(piece + 1) * width].astype(BF16)
            ref[0, head, piece, HEAD_DIM:V_ROWS, :] = ones

    qa_ref[0] = (pm[:, _C_AQ:_C_AQ + 256] * (HEAD_DIM ** -0.5 * LOG2E)).astype(BF16)
    ka_ref[0] = pm[:, _C_AK:_C_AK + 128].astype(BF16)
    for h in range(WIN_KV_HEADS):
        put_vt(vta_ref, h, pvt[_R_AV + 64 * h:_R_AV + 64 * (h + 1)])

    cosc = cosc_ref[...]
    sinc = sinc_ref[...]
    gavg = gavg_ref[...]

    def head_rms_scale(v):
        hi, lo = _split_hi_lo(v * v)
        ms = (jnp.dot(hi, gavg, preferred_element_type=F32)
              + jnp.dot(lo, gavg, preferred_element_type=F32))
        return lax.rsqrt(ms + NORM_EPS)

    def axial(off, off_rot, g_ref_, gr_ref_, scale):
        v = pm[:, off:off + LANES]
        r = head_rms_scale(v)
        vn = v * r * g_ref_[...]
        vrn = pm[:, off_rot:off_rot + LANES] * r * gr_ref_[...]
        return ((vn * cosc + vrn * sinc) * scale).astype(BF16)

    for grp in range(2):
        qc_ref[0, :, grp * LANES:(grp + 1) * LANES] = axial(
            _C_CQ + grp * LANES, _C_CQR + grp * LANES, axq_ref, axqr_ref, HEAD_DIM ** -0.5 * LOG2E)
    kc_ref[0] = axial(_C_CK, _C_CKR, axk_ref, axkr_ref, 1.0)
    for h in range(AX_KV_HEADS):
        put_vt(vtc_ref, h, pvt[_R_CV + 64 * h:_R_CV + 64 * (h + 1)])

    qd_ref[0] = (pm[:, _C_DQ:_C_DQ + 512] * (DIFF_QK ** -0.5 * LOG2E)).astype(BF16)
    augk = augk_ref[...]
    for h in range(DIFF_HEADS):
        kd_ref[0, :, h * LANES:(h + 1) * LANES] = (
            pm[:, _C_DK + h * LANES:_C_DK + (h + 1) * LANES] + augk).astype(BF16)
        put_vt(vtd_ref, h, pvt[_R_DV + 64 * h:_R_DV + 64 * (h + 1)])

    cosb = cosb_ref[...]
    sinb = sinb_ref[...]
    cq = pm[:, _C_BQ:_C_BQ + MLA_Q_RANK]
    cqn = (cq * lax.rsqrt(jnp.mean(cq * cq, axis=-1, keepdims=True) + NORM_EPS) * gq_ref[...]).astype(BF16)
    qw = jnp.dot(cqn, wuq_ref[...], preferred_element_type=F32)
    qwr = jnp.dot(cqn, wuqr_ref[...], preferred_element_type=F32)
    ckv = pm[:, _C_BKV:_C_BKV + MLA_KV_RANK]
    ckvn = (ckv * lax.rsqrt(jnp.mean(ckv * ckv, axis=-1, keepdims=True) + NORM_EPS) * gkv_ref[...]).astype(BF16)
    kw = jnp.dot(ckvn, wukvk_ref[...], preferred_element_type=F32)
    vbt = lax.dot_general(wukvvt_ref[...], ckvn, nt, preferred_element_type=F32)
    k_rope = pm[:, _C_BKR:_C_BKR + LANES] * cosb + pm[:, _C_BKRR:_C_BKRR + LANES] * sinb
    q_scale = (MLA_NOPE + MLA_ROPE) ** -0.5 * LOG2E
    for h in range(MLA_HEADS):
        sl = slice(h * LANES, (h + 1) * LANES)
        qb_ref[0, :, sl] = ((qw[:, sl] * cosb + qwr[:, sl] * sinb) * q_scale).astype(BF16)
        kb_ref[0, :, sl] = (kw[:, sl] + k_rope).astype(BF16)
        put_vt(vtb_ref, h, vbt[64 * h:64 * (h + 1)])


def _prep(x, p, tabs):
    bsz, seq, d = x.shape
    tm = KV_CHUNK
    nch = seq // tm
    tok = lambda w: pl.BlockSpec((1, tm, w), lambda b, i: (b, i, 0))
    tab = pl.BlockSpec((tm, LANES), lambda b, i: (i, 0))
    vt = lambda nh, width=tm: pl.BlockSpec((1, nh, tm // width, V_ROWS, width), lambda b, i: (b, 0, i, 0, 0))
    vt_shape = lambda nh, width=tm: jax.ShapeDtypeStruct((bsz, nh, seq // width, V_ROWS, width), BF16)
    tok_shape = lambda w: jax.ShapeDtypeStruct((bsz, seq, w), BF16)
    weights = [p["w_tok"], p["w_vt"], p["w_uq"], p["w_uq_rot"], p["w_ukv_k"], p["w_ukv_vt"], p["g_avg"],
               p["g_q"], p["g_kv"], p["ax_q"], p["ax_q_rot"], p["ax_k"], p["ax_k_rot"]]
    return pl.pallas_call(
        _prep_kernel,
        grid=(bsz, nch),
        in_specs=[tok(d)] + [_resident(w.shape) for w in weights] + [tab] * 5,
        out_specs=[tok(256), tok(128), vt(2, WIN_TILE), tok(512), tok(512), vt(4),
                   tok(256), tok(128), vt(2), tok(512), tok(512), vt(4)],
        out_shape=[tok_shape(256), tok_shape(128), vt_shape(2, WIN_TILE), tok_shape(512), tok_shape(512), vt_shape(4),
                   tok_shape(256), tok_shape(128), vt_shape(2), tok_shape(512), tok_shape(512), vt_shape(4)],
        compiler_params=_cparams(("parallel", "parallel")),
    )(x, *weights, tabs["cos_b"], tabs["sin_b"], tabs["cos_c"], tabs["sin_c"], tabs["aug_k"])


LOG2E = math.log2(math.e)


def _bf16_parts(x, n):
    parts = []
    for _ in range(n):
        part = float(np.float32(x).astype(jnp.bfloat16).astype(np.float32))
        parts.append(part)
        x -= part
    return tuple(parts)


LOG2E_PARTS = _bf16_parts(LOG2E, 3)


def _score_block(k_ref, q, k_slab, rb, row0, s_write, n, r, hooks):
    start = row0 + r * rb
    if not isinstance(start, int):
        start = pl.multiple_of(start, rb)
    kc = k_ref[0, pl.ds(start, rb), k_slab * LANES:(k_slab + 1) * LANES]
    if hooks is not None:
        kc = hooks[0](kc, start // rb)
    s = jnp.dot(kc, q, preferred_element_type=F32)
    if hooks is not None:
        s = hooks[1](s, n, start // rb)
    s_write[n, r * rb:(r + 1) * rb, :] = s
    return jnp.max(s.reshape(rb // F32_SUBLANES, F32_SUBLANES, s.shape[-1]), axis=0)


def _first_scores(k_ref, chains, cps, rb, s_write, hooks):
    out = []
    for n, (q, k_slab, _) in enumerate(chains):
        parts = [_score_block(k_ref, q, k_slab, rb, 0, s_write, n, r, hooks) for r in range(cps * KV_CHUNK // rb)]
        out.append(functools.reduce(jnp.maximum, parts))
    return out


def _pipelined_step(k_ref, vt_ref, chains, exp_fn, cps, rb, hooks, j, carry, s_read, s_write):
    step_rows = cps * KV_CHUNK
    blocks_per_chunk = KV_CHUNK // rb
    next_row0 = (j + 1) * step_rows
    out = []
    for n, ((q, k_slab, v_head), (m, acc, smax)) in enumerate(zip(chains, carry)):
        m_new = jnp.maximum(m, jnp.max(smax, axis=0, keepdims=True))
        alpha = exp_fn(m - m_new)
        next_max, pv = smax, None
        for r in range(step_rows // rb):
            if s_write is not None:
                part = _score_block(k_ref, q, k_slab, rb, next_row0, s_write, n, r, hooks)
                next_max = part if r == 0 else jnp.maximum(next_max, part)
            p = exp_fn(s_read[n, r * rb:(r + 1) * rb, :] - m_new).astype(BF16)
            c, b = divmod(r, blocks_per_chunk)
            part = jnp.dot(vt_ref[0, v_head, j * cps + c, :, b * rb:(b + 1) * rb], p, preferred_element_type=F32)
            pv = part if pv is None else pv + part
        out.append((m_new, acc * alpha + pv, next_max))
    return tuple(out)


def _pipelined_attention(k_ref, vt_ref, chains, exp_fn, n_chunks, rb, s0_ref, s1_ref, hooks=None):
    nq = chains[0][0].shape[1]
    cps = s0_ref.shape[1] // KV_CHUNK
    n_steps = n_chunks // cps
    step = functools.partial(_pipelined_step, k_ref, vt_ref, chains, exp_fn, cps, rb, hooks)
    first_max = _first_scores(k_ref, chains, cps, rb, s0_ref, hooks)
    init = tuple((jnp.full((1, nq), NEG_INF, F32), jnp.zeros((V_ROWS, nq), F32), smax) for smax in first_max)

    def pair(t, carry):
        carry = step(2 * t, carry, s0_ref, s1_ref)
        return step(2 * t + 1, carry, s1_ref, s0_ref)

    carry = lax.fori_loop(0, n_steps // 2 - 1, pair, init)
    carry = step(n_steps - 2, carry, s0_ref, s1_ref)
    final = step(n_steps - 1, carry, s1_ref, None)
    return [acc for _, acc, _ in final]


def _normalised(acc):
    return acc[:HEAD_DIM] / acc[HEAD_DIM:HEAD_DIM + 1]


def _single_buffered(block, index_map):
    return pl.BlockSpec(block, index_map, pipeline_mode=pl.Buffered(1))


def _dense_kernel(q_ref, k_ref, vt_ref, o_ref, s0_ref, s1_ref, *, n_chunks, heads):
    tq = q_ref.shape[1]
    lane = lax.broadcasted_iota(jnp.int32, (tq, LANES), 1)
    chains = []
    for q_slab, k_slab, v_head, shared_key_slab in heads:
        q = q_ref[0, :, q_slab * LANES:(q_slab + 1) * LANES].astype(F32)
        if shared_key_slab:
            q = jnp.where((lane >> HEAD_SHIFT) == pl.program_id(1), q, 0.0)
        chains.append((q.T.astype(BF16), k_slab, v_head))
    accs = _pipelined_attention(k_ref, vt_ref, chains, jnp.exp2, n_chunks, DENSE_ROW_BLOCK, s0_ref, s1_ref)
    for n, acc in enumerate(accs):
        o_ref[0, n * HEAD_DIM:(n + 1) * HEAD_DIM, :] = _normalised(acc).astype(BF16)


def _dense_attention(q, k, vt, *, n_heads, kv_group, heads_per_step):
    bsz, seq, _ = q.shape
    nch = vt.shape[2]
    tq = min(Q_TILE, seq)
    hp = heads_per_step
    if kv_group:
        assert hp == kv_group and n_heads // kv_group == LANES // HEAD_DIM
        heads = tuple((s, 0, 0, True) for s in range(kv_group))
        q_spec = pl.BlockSpec((1, tq, kv_group * LANES), lambda b, g, i: (b, i, 0))
        k_spec = pl.BlockSpec((1, seq, LANES), lambda b, g, i: (b, 0, 0))
        vt_spec = pl.BlockSpec((1, 1, nch, V_ROWS, KV_CHUNK), lambda b, g, i: (b, g, 0, 0, 0))
    else:
        heads = tuple((n, n, n, False) for n in range(hp))
        q_spec = pl.BlockSpec((1, tq, hp * LANES), lambda b, g, i: (b, i, g))
        k_spec = pl.BlockSpec((1, seq, hp * LANES), lambda b, g, i: (b, 0, g))
        vt_spec = pl.BlockSpec((1, hp, nch, V_ROWS, KV_CHUNK), lambda b, g, i: (b, g, 0, 0, 0))
    assert nch % 2 == 0
    cps = math.gcd(CHUNKS_PER_STEP, nch // 2)
    s_buffer = pltpu.VMEM((hp, cps * KV_CHUNK, tq), F32)
    return pl.pallas_call(
        functools.partial(_dense_kernel, n_chunks=nch, heads=heads),
        grid=(bsz, n_heads // hp, seq // tq),
        in_specs=[q_spec, k_spec, vt_spec],
        out_specs=pl.BlockSpec((1, hp * HEAD_DIM, tq), lambda b, g, i: (b, g, i)),
        out_shape=jax.ShapeDtypeStruct((bsz, n_heads * HEAD_DIM, seq), BF16),
        scratch_shapes=[s_buffer, s_buffer],
        compiler_params=_cparams(("parallel", "parallel", "arbitrary"), ATTENTION_FLAGS),
    )(q, k, vt)


_AUG = 2 * DIFF_QK


def _diff_kernel(lam_ref, g_ref, q_ref, k_ref, vt_ref, o_ref, s0_ref, s1_ref, bias_ref, *,
                 n_chunks, hp, lam_init, slope_exp0):
    i = pl.program_id(2)
    tq = q_ref.shape[1]
    lane = lax.broadcasted_iota(jnp.int32, (tq, LANES), 1)
    digit = (lane - _AUG) & 3
    part = (lane - _AUG) >> 2
    is_bias_lane = (lane >= _AUG) & (lane < _AUG + 4 * len(LOG2E_PARTS))
    log2e_part = functools.reduce(lambda acc, x: jnp.where(part == x[0], x[1], acc),
                                  enumerate(LOG2E_PARTS), jnp.zeros((tq, LANES), F32))
    qpos = i * tq + lax.broadcasted_iota(jnp.int32, (tq, LANES), 0)
    q_hi = ((qpos >> 7) << 7).astype(F32)
    q_lo = (qpos & 127).astype(F32)
    own_dist = jnp.abs(lax.broadcasted_iota(jnp.int32, (ROW_BLOCK, tq), 1)
                       - lax.broadcasted_iota(jnp.int32, (ROW_BLOCK, tq), 0)).astype(F32)

    chains = []
    for n in range(hp):
        head = pl.program_id(1) * hp + n
        slope_bits = jnp.full((1, LANES), 127 - slope_exp0, jnp.int32) - head
        slope = lax.bitcast_convert_type(slope_bits << 23, F32)
        qf = q_ref[0, :, n * LANES:(n + 1) * LANES].astype(F32)
        aug = jnp.where(digit == 0, -slope * q_hi, jnp.where(digit == 1, -slope * q_lo, slope * log2e_part))
        aug = jnp.where(is_bias_lane, aug, 0.0)
        q1 = jnp.where(lane < DIFF_QK, qf, 0.0)
        q2 = jnp.where((lane >= DIFF_QK) & (lane < 2 * DIFF_QK), qf, 0.0)
        chains.append((jnp.concatenate([q1 + aug, q2 + aug], axis=0).T.astype(BF16), n, n))
        bias = (-LOG2E) * slope[:, :1] * own_dist
        bias_ref[n, 0] = jnp.zeros((ROW_BLOCK, 2 * tq), F32)
        bias_ref[n, 1] = jnp.concatenate([bias, bias], axis=1)

    key_lane = lax.broadcasted_iota(jnp.int32, (1, LANES), 1)
    key_bias_lane = (key_lane >= _AUG) & (key_lane < _AUG + 4 * len(LOG2E_PARTS))

    def edit_keys(kc, block):
        sign = jnp.where(block < i, 1.0, jnp.where(block > i, -1.0, 0.0))
        return kc * jnp.where(key_bias_lane, sign, 1.0).astype(BF16)

    def add_bias(s, n, block):
        return s + bias_ref[n, jnp.where(block == i, 1, 0)]

    accs = _pipelined_attention(k_ref, vt_ref, chains, jnp.exp2, n_chunks, ROW_BLOCK, s0_ref, s1_ref,
                                (edit_keys, add_bias))

    lp = lam_ref[...]
    lam = (jnp.exp(jnp.sum(lp[0:1] * lp[1:2], axis=-1, keepdims=True))
           - jnp.exp(jnp.sum(lp[2:3] * lp[3:4], axis=-1, keepdims=True)) + lam_init)
    for n, acc in enumerate(accs):
        o = _normalised(acc[:, :tq]) - lam * _normalised(acc[:, tq:])
        o = o * lax.rsqrt(jnp.mean(o * o, axis=0, keepdims=True) + NORM_EPS)
        o_ref[0, n * DIFF_V:(n + 1) * DIFF_V, :] = (o * g_ref[...] * (1.0 - lam_init)).astype(BF16)


def _diff_attention(q, k, vt, lam_params, subln_col, *, layer_idx, slope_exp0, heads_per_step):
    bsz, seq, _ = q.shape
    nch = vt.shape[2]
    tq = ROW_BLOCK
    hp = heads_per_step
    assert seq % tq == 0 and seq <= 256 * LANES and DIFF_HEADS % hp == 0 and nch % 2 == 0
    lam_init = 0.8 - 0.6 * math.exp(-0.3 * layer_idx)
    cps = math.gcd(CHUNKS_PER_STEP, nch // 2)
    s_buffer = pltpu.VMEM((hp, cps * KV_CHUNK, 2 * tq), F32)
    return pl.pallas_call(
        functools.partial(_diff_kernel, n_chunks=nch, hp=hp, lam_init=lam_init, slope_exp0=slope_exp0),
        grid=(bsz, DIFF_HEADS // hp, seq // tq),
        in_specs=[
            _resident(lam_params.shape),
            _resident(subln_col.shape),
            pl.BlockSpec((1, tq, hp * LANES), lambda b, g, i: (b, i, g)),
            pl.BlockSpec((1, seq, hp * LANES), lambda b, g, i: (b, 0, g)),
            pl.BlockSpec((1, hp, nch, V_ROWS, KV_CHUNK), lambda b, g, i: (b, g, 0, 0, 0)),
        ],
        out_specs=pl.BlockSpec((1, hp * DIFF_V, tq), lambda b, g, i: (b, g, i)),
        out_shape=jax.ShapeDtypeStruct((bsz, DIFF_HEADS * DIFF_V, seq), BF16),
        scratch_shapes=[s_buffer, s_buffer, pltpu.VMEM((hp, 2, ROW_BLOCK, 2 * tq), F32)],
        compiler_params=_cparams(("parallel", "parallel", "arbitrary"), ATTENTION_FLAGS),
    )(lam_params, subln_col, q, k, vt)


_WIN_OFFSETS = (-1, 0, 1)
_WIN_NO_BLOCK = len(_WIN_OFFSETS)


def _window_bias_tiles():
    key = np.arange(WIN_TILE)[:, None]
    query = np.arange(WIN_TILE)[None, :]
    tiles = np.full((WIN_HEADS, _WIN_NO_BLOCK + 1, WIN_TILE, WIN_TILE), NEG_INF, np.float32)
    for head in range(WIN_HEADS):
        for sel, offset in enumerate(_WIN_OFFSETS):
            dist = np.abs(query - key - WIN_TILE * offset)
            tiles[head, sel] = np.where(dist <= WINDOW, -(2.0 ** -(head + 1)) * LOG2E * dist, NEG_INF)
    return tiles


def _window_kernel(sink_ref, bias_ref, q_ref, k_ref, vt_ref, o_ref, *, n_blocks):
    i = pl.program_id(1)
    lane = lax.broadcasted_iota(jnp.int32, (WIN_TILE, LANES), 1)
    group = WIN_HEADS // WIN_KV_HEADS
    blocks = []
    for offset in _WIN_OFFSETS:
        block = i + offset
        exists = (block >= 0) & (block < n_blocks)
        block = jnp.clip(block, 0, n_blocks - 1)
        keys = k_ref[0, pl.ds(pl.multiple_of(block * WIN_TILE, WIN_TILE), WIN_TILE), :]
        blocks.append((jnp.where(exists, offset + 1, _WIN_NO_BLOCK), block, keys))
    scores = []
    for head in range(WIN_HEADS):
        kv_head, slab = head // group, head % group
        qf = q_ref[0, :, slab * LANES:(slab + 1) * LANES].astype(F32)
        qt = jnp.where((lane >> HEAD_SHIFT) == kv_head, qf, 0.0).T.astype(BF16)
        scores.append([jnp.dot(keys, qt, preferred_element_type=F32) + bias_ref[head, sel]
                       for sel, _, keys in blocks])
    probs = []
    for head in range(WIN_HEADS):
        sink = jnp.full((1, WIN_TILE), sink_ref[head] * LOG2E, F32)
        m = functools.reduce(jnp.maximum, [jnp.max(s, axis=0, keepdims=True) for s in scores[head]], sink)
        probs.append(([jnp.exp2(s - m).astype(BF16) for s in scores[head]], jnp.exp2(sink - m)))
    for head in range(WIN_HEADS):
        acc = None
        for p, (_, block, _) in zip(probs[head][0], blocks):
            part = jnp.dot(vt_ref[0, head // group, block], p, preferred_element_type=F32)
            acc = part if acc is None else acc + part
        denominator = acc[HEAD_DIM:HEAD_DIM + 1] + probs[head][1]
        o_ref[0, head * HEAD_DIM:(head + 1) * HEAD_DIM, :] = (acc[:HEAD_DIM] / denominator).astype(BF16)


def _window_attention(q, k, vt, sink):
    bsz, seq, _ = q.shape
    n_blocks = vt.shape[2]
    bias = jnp.asarray(_window_bias_tiles())
    return pl.pallas_call(
        functools.partial(_window_kernel, n_blocks=n_blocks),
        grid=(bsz, n_blocks),
        in_specs=[
            pl.BlockSpec(memory_space=pltpu.SMEM),
            _resident(bias.shape),
            pl.BlockSpec((1, WIN_TILE, 2 * LANES), lambda b, i: (b, i, 0)),
            _single_buffered((1, seq, LANES), lambda b, i: (b, 0, 0)),
            _single_buffered((1, WIN_KV_HEADS, n_blocks, V_ROWS, WIN_TILE), lambda b, i: (b, 0, 0, 0, 0)),
        ],
        out_specs=pl.BlockSpec((1, WIN_HEADS * HEAD_DIM, WIN_TILE), lambda b, i: (b, 0, i)),
        out_shape=jax.ShapeDtypeStruct((bsz, WIN_HEADS * HEAD_DIM, seq), BF16),
        compiler_params=_cparams(("parallel", "arbitrary")),
    )(sink, bias, q, k, vt)


def _outproj_kernel(x_ref, oa_ref, ob_ref, oc_ref, od_ref, wt_ref, g_ref, b_ref, o_ref, *, alpha):
    mix_t = None
    for m, ref in enumerate((oa_ref, ob_ref, oc_ref, od_ref)):
        part = jnp.dot(wt_ref[:, m * 256:(m + 1) * 256], ref[0], preferred_element_type=F32)
        mix_t = part if mix_t is None else mix_t + part
    y = alpha * x_ref[0] + mix_t.T
    o_ref[0] = _layer_norm_rows(y, g_ref[...], b_ref[...])


def _outproj(x, o_a, o_b, o_c, o_d, w_out_t, g, b, alpha):
    bsz, seq, d = x.shape
    tm = min(TOK_TILE, seq)
    ot = pl.BlockSpec((1, 256, tm), lambda bb, i: (bb, 0, i))
    tok = pl.BlockSpec((1, tm, d), lambda bb, i: (bb, i, 0))
    return pl.pallas_call(
        functools.partial(_outproj_kernel, alpha=alpha),
        grid=(bsz, seq // tm),
        in_specs=[tok, ot, ot, ot, ot, _resident(w_out_t.shape), _resident(g.shape), _resident(b.shape)],
        out_specs=tok,
        out_shape=jax.ShapeDtypeStruct((bsz, seq, d), F32),
        compiler_params=_cparams(("parallel", "parallel")),
    )(x, o_a, o_b, o_c, o_d, w_out_t, g, b)


def _rot_half_cols(start, width):
    half = width // 2
    src = np.concatenate([np.arange(start + half, start + width), np.arange(start, start + half)])
    sgn = np.concatenate([-np.ones(half), np.ones(half)])
    return src, sgn


def _gqa_slab_order(n_heads, n_kv):
    group = n_heads // n_kv
    return [kv * group + s for s in range(group) for kv in range(n_kv)]


def _layer_params(w_in, mla_q_norm, mla_w_uq, mla_kv_norm, mla_w_ukv, ax_q_norm, ax_k_norm):
    d = w_in.shape[0]
    zeros = lambda n: jnp.zeros((d, n), w_in.dtype)
    a0, b0, c0, d0 = 0, A_COLS, A_COLS + B_COLS, A_COLS + B_COLS + C_COLS
    head_cols = lambda base, h: np.arange(base + h * HEAD_DIM, base + (h + 1) * HEAD_DIM)

    def axial_rot(cols):
        src, sgn = [], []
        for blk in range(0, HEAD_DIM, HEAD_DIM // 2):
            s_, g_ = _rot_half_cols(blk, HEAD_DIM // 2)
            src.append(cols[s_])
            sgn.append(g_)
        return np.concatenate(src), np.concatenate(sgn)

    order = _gqa_slab_order(AX_HEADS, AX_KV_HEADS)
    aq = np.concatenate([head_cols(a0, h) for h in order])
    ak = np.arange(a0 + 256, a0 + 384)
    cq = np.concatenate([head_cols(c0, h) for h in order])
    ck = np.arange(c0 + 256, c0 + 384)
    cq_rot = [axial_rot(head_cols(c0, h)) for h in order]
    ck_rot = [axial_rot(head_cols(c0 + 256, h)) for h in range(AX_KV_HEADS)]

    def gather(src, sgn=None):
        w = w_in[:, np.asarray(src)]
        return w if sgn is None else w * jnp.asarray(sgn, w.dtype)[None, :]

    def widen(cols):
        return jnp.concatenate([gather(cols), zeros(LANES - len(cols))], axis=1)

    kr = np.arange(b0 + MLA_Q_RANK + MLA_KV_RANK, b0 + B_COLS)
    kr_src, kr_sgn = _rot_half_cols(kr[0], MLA_ROPE)
    place_rope = lambda w: jnp.concatenate([zeros(MLA_NOPE), w, zeros(LANES - MLA_NOPE - MLA_ROPE)], axis=1)

    w_tok = jnp.concatenate(
        [gather(aq), gather(ak), gather(cq), gather(ck)]
        + [gather(s, g) for s, g in cq_rot] + [gather(s, g) for s, g in ck_rot]
        + [widen(head_cols(d0, h)) for h in range(DIFF_HEADS)]
        + [widen(head_cols(d0 + 256, h)) for h in range(DIFF_HEADS)]
        + [gather(np.arange(b0, b0 + MLA_Q_RANK + MLA_KV_RANK)),
           place_rope(gather(kr)), place_rope(gather(kr_src, kr_sgn))], axis=1)
    assert w_tok.shape[1] == _C_END
    v_cols = np.concatenate([np.arange(a0 + 384, a0 + 512), np.arange(c0 + 384, c0 + 512),
                             np.arange(d0 + 512, d0 + 768)])
    w_vt = w_in[:, v_cols].T

    qd = MLA_NOPE + MLA_ROPE
    zq = lambda n: jnp.zeros((MLA_Q_RANK, n), mla_w_uq.dtype)
    uq, uq_rot = [], []
    for h in range(MLA_HEADS):
        blk = mla_w_uq[:, h * qd:(h + 1) * qd]
        src, sgn = _rot_half_cols(MLA_NOPE, MLA_ROPE)
        rot = blk[:, src] * jnp.asarray(sgn, blk.dtype)[None, :]
        uq += [blk, zq(LANES - qd)]
        uq_rot += [zq(MLA_NOPE), rot, zq(LANES - qd)]
    ukv = mla_w_ukv.reshape(MLA_KV_RANK, MLA_HEADS, MLA_NOPE + MLA_V)
    ukv_k = jnp.concatenate([ukv[:, :, :MLA_NOPE], jnp.zeros((MLA_KV_RANK, MLA_HEADS, LANES - MLA_NOPE), ukv.dtype)],
                            axis=2).reshape(MLA_KV_RANK, MLA_HEADS * LANES)
    ukv_vt = ukv[:, :, MLA_NOPE:].reshape(MLA_KV_RANK, MLA_HEADS * MLA_V).T

    def ax_gain(g):
        src, sgn = axial_rot(np.arange(HEAD_DIM))
        return jnp.tile(g, 2)[None, :].astype(F32), jnp.tile(g[src], 2)[None, :].astype(F32)

    ax_q, ax_q_rot = ax_gain(ax_q_norm)
    ax_k, ax_k_rot = ax_gain(ax_k_norm)
    lane = np.arange(LANES)
    g_avg = (lane[:, None] // HEAD_DIM == lane[None, :] // HEAD_DIM).astype(np.float32) / HEAD_DIM
    return {
        "w_tok": w_tok.astype(BF16), "w_vt": w_vt.astype(BF16),
        "w_uq": jnp.concatenate(uq, axis=1).astype(BF16), "w_uq_rot": jnp.concatenate(uq_rot, axis=1).astype(BF16),
        "w_ukv_k": ukv_k.astype(BF16), "w_ukv_vt": ukv_vt.astype(BF16),
        "g_avg": jnp.asarray(g_avg, BF16),
        "g_q": mla_q_norm[None, :].astype(F32), "g_kv": mla_kv_norm[None, :].astype(F32),
        "ax_q": ax_q, "ax_q_rot": ax_q_rot, "ax_k": ax_k, "ax_k_rot": ax_k_rot,
    }


def _position_tables(seq):
    pos = jnp.arange(seq, dtype=jnp.int32)
    half = HEAD_DIM // 2

    def angles(p, dim):
        inv = ROPE_BASE ** (-jnp.arange(0, dim, 2, dtype=F32) / dim)
        ang = p.astype(F32)[:, None] * inv[None, :]
        return jnp.cos(ang), jnp.sin(ang)

    cb, sb = angles(pos, MLA_ROPE)
    pad = LANES - MLA_NOPE - MLA_ROPE
    cos_b = jnp.concatenate([jnp.ones((seq, MLA_NOPE), F32), cb, cb, jnp.zeros((seq, pad), F32)], axis=1)
    sin_b = jnp.concatenate([jnp.zeros((seq, MLA_NOPE), F32), sb, sb, jnp.zeros((seq, pad), F32)], axis=1)
    cr, sr = angles(pos // GRID_W, half)
    cc, sc = angles(pos % GRID_W, half)
    cos_c = jnp.tile(jnp.concatenate([cr, cr, cc, cc], axis=1), (1, 2))
    sin_c = jnp.tile(jnp.concatenate([sr, sr, sc, sc], axis=1), (1, 2))
    rel = np.arange(LANES) - _AUG
    in_aug = (rel >= 0) & (rel < 4 * len(LOG2E_PARTS))
    part_row = np.where(in_aug, np.asarray(LOG2E_PARTS + (0.0,), np.float32)[np.clip(rel >> 2, 0, len(LOG2E_PARTS))], 0.0)
    digit = jnp.asarray((rel & 3)[None, :])
    pos_hi = ((pos >> 7) << 7).astype(F32)[:, None]
    pos_lo = (pos & 127).astype(F32)[:, None]
    aug = jnp.where(digit < 2, jnp.asarray(part_row, F32)[None, :], jnp.where(digit == 2, pos_hi, pos_lo))
    aug = jnp.where(jnp.asarray(in_aug[None, :]), aug, 0.0)
    return {"cos_b": cos_b, "sin_b": sin_b, "cos_c": cos_c, "sin_c": sin_c, "aug_k": aug}


def _ffn_weights(w_gu, w_down):
    d, two_ff = w_gu.shape
    d_ff = two_ff // 2
    n_chunks = d_ff // FF_CHUNK
    wgu = w_gu.astype(BF16).reshape(d, 2, n_chunks, FF_CHUNK).transpose(2, 0, 1, 3).reshape(n_chunks, d, 2 * FF_CHUNK)
    wd = w_down.astype(BF16).reshape(n_chunks, FF_CHUNK, d)
    return wgu, wd


def kernel(x, w_in, win_sink, mla_q_norm, mla_w_uq, mla_kv_norm, mla_w_ukv, ax_q_norm, ax_k_norm,
           diff_lambda, diff_subln, w_out, ffn_w_gu, ffn_w_down, ln_g, ln_b):
    depth = w_in.shape[0]
    bsz, seq, d = x.shape
    alpha = (2 * depth) ** 0.25
    tabs = _position_tables(seq)
    row = lambda v: v[None, :].astype(F32)
    for l in range(depth):
        wgu, wd = _ffn_weights(ffn_w_gu[l, 0], ffn_w_down[l, 0])
        x = _ffn(x.reshape(bsz * seq, d), wgu, wd, row(ln_g[l, 0]), row(ln_b[l, 0]), alpha).reshape(bsz, seq, d)

        p = _layer_params(w_in[l], mla_q_norm[l], mla_w_uq[l], mla_kv_norm[l], mla_w_ukv[l],
                          ax_q_norm[l], ax_k_norm[l])
        qa, ka, vta, qb, kb, vtb, qc, kc, vtc, qd, kd, vtd = _prep(x, p, tabs)
        o_a = _window_attention(qa, ka, vta, win_sink[l].astype(F32))
        o_b = _dense_attention(qb, kb, vtb, n_heads=MLA_HEADS, kv_group=0, heads_per_step=HEADS_PER_STEP)
        o_c = _dense_attention(qc, kc, vtc, n_heads=AX_HEADS, kv_group=AX_HEADS // AX_KV_HEADS,
                               heads_per_step=AX_HEADS // AX_KV_HEADS)
        o_d = _diff_attention(qd, kd, vtd, diff_lambda[l].astype(F32), diff_subln[l][:, None].astype(F32),
                              layer_idx=l, slope_exp0=WIN_HEADS + 1, heads_per_step=HEADS_PER_STEP)
        x = _outproj(x, o_a, o_b, o_c, o_d, w_out[l].T.astype(BF16), row(ln_g[l, 1]), row(ln_b[l, 1]), alpha)

        wgu, wd = _ffn_weights(ffn_w_gu[l, 1], ffn_w_down[l, 1])
        x = _ffn(x.reshape(bsz * seq, d), wgu, wd, row(ln_g[l, 2]), row(ln_b[l, 2]), alpha).reshape(bsz, seq, d)
    return x
```

```python
import functools
import math

import numpy as np
import jax
import jax.numpy as jnp
from jax import lax
from jax.experimental import pallas as pl
from jax.experimental.pallas import tpu as pltpu

F32 = jnp.float32
BF16 = jnp.bfloat16

HEAD_DIM = 64
HEAD_SHIFT = 6
GRID_W = 64
WIN_HEADS, WIN_KV_HEADS, WINDOW = 4, 2, 128
MLA_HEADS, MLA_Q_RANK, MLA_KV_RANK, MLA_NOPE, MLA_ROPE, MLA_V = 4, 256, 128, 64, 32, 64
AX_HEADS, AX_KV_HEADS = 4, 2
DIFF_HEADS, DIFF_QK, DIFF_V = 4, 32, 64
A_COLS = (WIN_HEADS + 2 * WIN_KV_HEADS) * HEAD_DIM
B_COLS = MLA_Q_RANK + MLA_KV_RANK + MLA_ROPE
C_COLS = (AX_HEADS + 2 * AX_KV_HEADS) * HEAD_DIM
D_COLS = DIFF_HEADS * (4 * DIFF_QK + DIFF_V)
ROPE_BASE = 10000.0
NORM_EPS = 1e-5
NEG_INF = -1e30

LANES = 128
F32_SUBLANES = 8
BF16_SUBLANES = 16
V_ROWS = HEAD_DIM + BF16_SUBLANES
VMEM_LIMIT = 56 * 1024 * 1024

TOK_TILE = 512
FFN_TOK_TILE = 1024
KV_CHUNK = 512
Q_TILE = 512
FF_CHUNK = 256
HEADS_PER_STEP = 2
CHUNKS_PER_STEP = 4
ROW_BLOCK = 256
DENSE_ROW_BLOCK = 256
WIN_TILE = 256
assert WIN_TILE >= WINDOW and KV_CHUNK % WIN_TILE == 0


def _cparams(sem, flags=None):
    return pltpu.CompilerParams(dimension_semantics=sem, vmem_limit_bytes=VMEM_LIMIT, flags=flags)


ATTENTION_FLAGS = None


def _resident(shape):
    nd = len(shape)
    return pl.BlockSpec(shape, lambda *_: (0,) * nd, pipeline_mode=pl.Buffered(1))


def _layer_norm_rows(y, g, b):
    mu = jnp.mean(y, axis=-1, keepdims=True)
    yc = y - mu
    var = jnp.mean(yc * yc, axis=-1, keepdims=True)
    return yc * lax.rsqrt(var + NORM_EPS) * g + b


def _ffn_kernel(x_ref, wgu_ref, wd_ref, g_ref, b_ref, o_ref, acc_ref, *, alpha, n_chunks):
    x = x_ref[...]
    xb = x.astype(BF16)

    def chunk(c):
        gu = jnp.dot(xb, wgu_ref[c], preferred_element_type=F32)
        gate = gu[:, :FF_CHUNK]
        up = gu[:, FF_CHUNK:]
        h = (gate * jax.nn.sigmoid(gate) * up).astype(BF16)
        return jnp.dot(h, wd_ref[c], preferred_element_type=F32)

    def body(c, carry):
        acc_ref[...] += chunk(c)
        return carry

    acc_ref[...] = chunk(0)
    lax.fori_loop(1, n_chunks, body, 0)
    y = alpha * x + 0.5 * acc_ref[...]
    o_ref[...] = _layer_norm_rows(y, g_ref[...], b_ref[...])


def _ffn(x2d, wgu, wd, g, b, alpha):
    n_tok, d = x2d.shape
    n_chunks = wgu.shape[0]
    tm = min(FFN_TOK_TILE, n_tok)
    return pl.pallas_call(
        functools.partial(_ffn_kernel, alpha=alpha, n_chunks=n_chunks),
        grid=(n_tok // tm,),
        in_specs=[
            pl.BlockSpec((tm, d), lambda i: (i, 0)),
            _resident(wgu.shape),
            _resident(wd.shape),
            _resident(g.shape),
            _resident(b.shape),
        ],
        out_specs=pl.BlockSpec((tm, d), lambda i: (i, 0)),
        out_shape=jax.ShapeDtypeStruct((n_tok, d), F32),
        scratch_shapes=[pltpu.VMEM((tm, d), F32)],
        compiler_params=_cparams(("parallel",)),
    )(x2d, wgu, wd, g, b)


_C_AQ, _C_AK, _C_CQ, _C_CK, _C_CQR, _C_CKR = 0, 256, 384, 640, 768, 1024
_C_DQ, _C_DK, _C_BQ, _C_BKV, _C_BKR, _C_BKRR, _C_END = 1152, 1664, 2176, 2432, 2560, 2688, 2816
_R_AV, _R_CV, _R_DV, _R_END = 0, 128, 256, 512


def _split_hi_lo(v):
    hi = v.astype(BF16)
    lo = (v - hi.astype(F32)).astype(BF16)
    return hi, lo


def _prep_kernel(x_ref, wtok_ref, wvt_ref, wuq_ref, wuqr_ref, wukvk_ref, wukvvt_ref, gavg_ref,
                 gq_ref, gkv_ref, axq_ref, axqr_ref, axk_ref, axkr_ref,
                 cosb_ref, sinb_ref, cosc_ref, sinc_ref, augk_ref,
                 qa_ref, ka_ref, vta_ref, qb_ref, kb_ref, vtb_ref,
                 qc_ref, kc_ref, vtc_ref, qd_ref, kd_ref, vtd_ref):
    xb = x_ref[0].astype(BF16)
    tm = xb.shape[0]
    pm = jnp.dot(xb, wtok_ref[...], preferred_element_type=F32)
    nt = (((1,), (1,)), ((), ()))
    pvt = lax.dot_general(wvt_ref[...], xb, nt, preferred_element_type=F32)

    def ones_row_tile(width):
        row = lax.broadcasted_iota(jnp.int32, (BF16_SUBLANES, width), 0)
        return jnp.where(row == 0, 1.0, 0.0).astype(BF16)

    ones_tile, ones_tile_narrow = ones_row_tile(tm), ones_row_tile(WIN_TILE)

    def put_vt(ref, head, vt):
        width = ref.shape[-1]
        ones = ones_tile if width == tm else ones_tile_narrow
        for piece in range(tm // width):
            ref[0, head, piece, 0:HEAD_DIM, :] = vt[:, piece * width:(piece + 1) * width].astype(BF16)
            ref[0, head, piece, HEAD_DIM:V_ROWS, :] = ones

    qa_ref[0] = (pm[:, _C_AQ:_C_AQ + 256] * (HEAD_DIM ** -0.5 * LOG2E)).astype(BF16)
    ka_ref[0] = pm[:, _C_AK:_C_AK + 128].astype(BF16)
    for h in range(WIN_KV_HEADS):
        put_vt(vta_ref, h, pvt[_R_AV + 64 * h:_R_AV + 64 * (h + 1)])

    cosc = cosc_ref[...]
    sinc = sinc_ref[...]
    gavg = gavg_ref[...]

    def head_rms_scale(v):
        hi, lo = _split_hi_lo(v * v)
        ms = (jnp.dot(hi, gavg, preferred_element_type=F32)
              + jnp.dot(lo, gavg, preferred_element_type=F32))
        return lax.rsqrt(ms + NORM_EPS)

    def axial(off, off_rot, g_ref_, gr_ref_, scale):
        v = pm[:, off:off + LANES]
        r = head_rms_scale(v)
        vn = v * r * g_ref_[...]
        vrn = pm[:, off_rot:off_rot + LANES] * r * gr_ref_[...]
        return ((vn * cosc + vrn * sinc) * scale).astype(BF16)

    for grp in range(2):
        qc_ref[0, :, grp * LANES:(grp + 1) * LANES] = axial(
            _C_CQ + grp * LANES, _C_CQR + grp * LANES, axq_ref, axqr_ref, HEAD_DIM ** -0.5 * LOG2E)
    kc_ref[0] = axial(_C_CK, _C_CKR, axk_ref, axkr_ref, 1.0)
    for h in range(AX_KV_HEADS):
        put_vt(vtc_ref, h, pvt[_R_CV + 64 * h:_R_CV + 64 * (h + 1)])

    qd_ref[0] = (pm[:, _C_DQ:_C_DQ + 512] * (DIFF_QK ** -0.5 * LOG2E)).astype(BF16)
    augk = augk_ref[...]
    for h in range(DIFF_HEADS):
        kd_ref[0, :, h * LANES:(h + 1) * LANES] = (
            pm[:, _C_DK + h * LANES:_C_DK + (h + 1) * LANES] + augk).astype(BF16)
        put_vt(vtd_ref, h, pvt[_R_DV + 64 * h:_R_DV + 64 * (h + 1)])

    cosb = cosb_ref[...]
    sinb = sinb_ref[...]
    cq = pm[:, _C_BQ:_C_BQ + MLA_Q_RANK]
    cqn = (cq * lax.rsqrt(jnp.mean(cq * cq, axis=-1, keepdims=True) + NORM_EPS) * gq_ref[...]).astype(BF16)
    qw = jnp.dot(cqn, wuq_ref[...], preferred_element_type=F32)
    qwr = jnp.dot(cqn, wuqr_ref[...], preferred_element_type=F32)
    ckv = pm[:, _C_BKV:_C_BKV + MLA_KV_RANK]
    ckvn = (ckv * lax.rsqrt(jnp.mean(ckv * ckv, axis=-1, keepdims=True) + NORM_EPS) * gkv_ref[...]).astype(BF16)
    kw = jnp.dot(ckvn, wukvk_ref[...], preferred_element_type=F32)
    vbt = lax.dot_general(wukvvt_ref[...], ckvn, nt, preferred_element_type=F32)
    k_rope = pm[:, _C_BKR:_C_BKR + LANES] * cosb + pm[:, _C_BKRR:_C_BKRR + LANES] * sinb
    q_scale = (MLA_NOPE + MLA_ROPE) ** -0.5 * LOG2E
    for h in range(MLA_HEADS):
        sl = slice(h * LANES, (h + 1) * LANES)
        qb_ref[0, :, sl] = ((qw[:, sl] * cosb + qwr[:, sl] * sinb) * q_scale).astype(BF16)
        kb_ref[0, :, sl] = (kw[:, sl] + k_rope).astype(BF16)
        put_vt(vtb_ref, h, vbt[64 * h:64 * (h + 1)])


def _prep(x, p, tabs):
    bsz, seq, d = x.shape
    tm = KV_CHUNK
    nch = seq // tm
    tok = lambda w: pl.BlockSpec((1, tm, w), lambda b, i: (b, i, 0))
    tab = pl.BlockSpec((tm, LANES), lambda b, i: (i, 0))
    vt = lambda nh, width=tm: pl.BlockSpec((1, nh, tm // width, V_ROWS, width), lambda b, i: (b, 0, i, 0, 0))
    vt_shape = lambda nh, width=tm: jax.ShapeDtypeStruct((bsz, nh, seq // width, V_ROWS, width), BF16)
    tok_shape = lambda w: jax.ShapeDtypeStruct((bsz, seq, w), BF16)
    weights = [p["w_tok"], p["w_vt"], p["w_uq"], p["w_uq_rot"], p["w_ukv_k"], p["w_ukv_vt"], p["g_avg"],
               p["g_q"], p["g_kv"], p["ax_q"], p["ax_q_rot"], p["ax_k"], p["ax_k_rot"]]
    return pl.pallas_call(
        _prep_kernel,
        grid=(bsz, nch),
        in_specs=[tok(d)] + [_resident(w.shape) for w in weights] + [tab] * 5,
        out_specs=[tok(256), tok(128), vt(2, WIN_TILE), tok(512), tok(512), vt(4),
                   tok(256), tok(128), vt(2), tok(512), tok(512), vt(4)],
        out_shape=[tok_shape(256), tok_shape(128), vt_shape(2, WIN_TILE), tok_shape(512), tok_shape(512), vt_shape(4),
                   tok_shape(256), tok_shape(128), vt_shape(2), tok_shape(512), tok_shape(512), vt_shape(4)],
        compiler_params=_cparams(("parallel", "parallel")),
    )(x, *weights, tabs["cos_b"], tabs["sin_b"], tabs["cos_c"], tabs["sin_c"], tabs["aug_k"])


LOG2E = math.log2(math.e)


def _bf16_parts(x, n):
    parts = []
    for _ in range(n):
        part = float(np.float32(x).astype(jnp.bfloat16).astype(np.float32))
        parts.append(part)
        x -= part
    return tuple(parts)


LOG2E_PARTS = _bf16_parts(LOG2E, 3)


def _score_block(k_ref, q, k_slab, rb, row0, s_write, n, r, hooks):
    start = row0 + r * rb
    if not isinstance(start, int):
        start = pl.multiple_of(start, rb)
    kc = k_ref[0, pl.ds(start, rb), k_slab * LANES:k_slab * LANES + q.shape[0]]
    if hooks is not None:
        kc = hooks[0](kc, start // rb)
    s = jnp.dot(kc, q, preferred_element_type=F32)
    if hooks is not None:
        s = hooks[1](s, n, start // rb)
    s_write[n, r * rb:(r + 1) * rb, :] = s
    return jnp.max(s.reshape(rb // F32_SUBLANES, F32_SUBLANES, s.shape[-1]), axis=0)


def _first_scores(k_ref, chains, cps, rb, s_write, hooks):
    out = []
    for n, (q, k_slab, _) in enumerate(chains):
        parts = [_score_block(k_ref, q, k_slab, rb, 0, s_write, n, r, hooks) for r in range(cps * KV_CHUNK // rb)]
        out.append(functools.reduce(jnp.maximum, parts))
    return out


def _pipelined_step(k_ref, vt_ref, chains, exp_fn, cps, rb, hooks, j, carry, s_read, s_write):
    step_rows = cps * KV_CHUNK
    blocks_per_chunk = KV_CHUNK // rb
    next_row0 = (j + 1) * step_rows
    out = []
    for n, ((q, k_slab, v_head), (m, acc, smax)) in enumerate(zip(chains, carry)):
        m_new = jnp.maximum(m, jnp.max(smax, axis=0, keepdims=True))
        alpha = exp_fn(m - m_new)
        next_max, pv = smax, None
        for r in range(step_rows // rb):
            if s_write is not None:
                part = _score_block(k_ref, q, k_slab, rb, next_row0, s_write, n, r, hooks)
                next_max = part if r == 0 else jnp.maximum(next_max, part)
            p = exp_fn(s_read[n, r * rb:(r + 1) * rb, :] - m_new).astype(BF16)
            c, b = divmod(r, blocks_per_chunk)
            part = jnp.dot(vt_ref[0, v_head, j * cps + c, :, b * rb:(b + 1) * rb], p, preferred_element_type=F32)
            pv = part if pv is None else pv + part
        out.append((m_new, acc * alpha + pv, next_max))
    return tuple(out)


def _pipelined_attention(k_ref, vt_ref, chains, exp_fn, n_chunks, rb, s0_ref, s1_ref, hooks=None):
    nq = chains[0][0].shape[1]
    cps = s0_ref.shape[1] // KV_CHUNK
    n_steps = n_chunks // cps
    step = functools.partial(_pipelined_step, k_ref, vt_ref, chains, exp_fn, cps, rb, hooks)
    first_max = _first_scores(k_ref, chains, cps, rb, s0_ref, hooks)
    init = tuple((jnp.full((1, nq), NEG_INF, F32), jnp.zeros((V_ROWS, nq), F32), smax) for smax in first_max)

    def pair(t, carry):
        carry = step(2 * t, carry, s0_ref, s1_ref)
        return step(2 * t + 1, carry, s1_ref, s0_ref)

    carry = lax.fori_loop(0, n_steps // 2 - 1, pair, init)
    carry = step(n_steps - 2, carry, s0_ref, s1_ref)
    final = step(n_steps - 1, carry, s1_ref, None)
    return [acc for _, acc, _ in final]


def _normalised(acc):
    return acc[:HEAD_DIM] / acc[HEAD_DIM:HEAD_DIM + 1]


def _single_buffered(block, index_map):
    return pl.BlockSpec(block, index_map, pipeline_mode=pl.Buffered(1))


def _dense_kernel(q_ref, k_ref, vt_ref, o_ref, s0_ref, s1_ref, *, n_chunks, heads, key_lanes):
    tq = q_ref.shape[1]
    lane = lax.broadcasted_iota(jnp.int32, (tq, LANES), 1)
    chains = []
    for q_slab, k_slab, v_head, shared_key_slab in heads:
        q = q_ref[0, :, q_slab * LANES:(q_slab + 1) * LANES].astype(F32)
        if shared_key_slab:
            q = jnp.where((lane >> HEAD_SHIFT) == pl.program_id(1), q, 0.0)
        chains.append((q.T[:key_lanes].astype(BF16), k_slab, v_head))
    accs = _pipelined_attention(k_ref, vt_ref, chains, jnp.exp2, n_chunks, DENSE_ROW_BLOCK, s0_ref, s1_ref)
    for n, acc in enumerate(accs):
        o_ref[0, n * HEAD_DIM:(n + 1) * HEAD_DIM, :] = _normalised(acc).astype(BF16)


def _dense_attention(q, k, vt, *, n_heads, kv_group, heads_per_step, key_lanes):
    bsz, seq, _ = q.shape
    nch = vt.shape[2]
    tq = min(Q_TILE, seq)
    hp = heads_per_step
    if kv_group:
        assert hp == kv_group and n_heads // kv_group == LANES // HEAD_DIM
        heads = tuple((s, 0, 0, True) for s in range(kv_group))
        q_spec = pl.BlockSpec((1, tq, kv_group * LANES), lambda b, g, i: (b, i, 0))
        k_spec = pl.BlockSpec((1, seq, LANES), lambda b, g, i: (b, 0, 0))
        vt_spec = pl.BlockSpec((1, 1, nch, V_ROWS, KV_CHUNK), lambda b, g, i: (b, g, 0, 0, 0))
    else:
        heads = tuple((n, n, n, False) for n in range(hp))
        q_spec = pl.BlockSpec((1, tq, hp * LANES), lambda b, g, i: (b, i, g))
        k_spec = pl.BlockSpec((1, seq, hp * LANES), lambda b, g, i: (b, 0, g))
        vt_spec = pl.BlockSpec((1, hp, nch, V_ROWS, KV_CHUNK), lambda b, g, i: (b, g, 0, 0, 0))
    assert nch % 2 == 0
    cps = math.gcd(CHUNKS_PER_STEP, nch // 2)
    s_buffer = pltpu.VMEM((hp, cps * KV_CHUNK, tq), F32)
    return pl.pallas_call(
        functools.partial(_dense_kernel, n_chunks=nch, heads=heads, key_lanes=key_lanes),
        grid=(bsz, n_heads // hp, seq // tq),
        in_specs=[q_spec, k_spec, vt_spec],
        out_specs=pl.BlockSpec((1, hp * HEAD_DIM, tq), lambda b, g, i: (b, g, i)),
        out_shape=jax.ShapeDtypeStruct((bsz, n_heads * HEAD_DIM, seq), BF16),
        scratch_shapes=[s_buffer, s_buffer],
        compiler_params=_cparams(("parallel", "parallel", "arbitrary"), ATTENTION_FLAGS),
    )(q, k, vt)


_AUG = 2 * DIFF_QK
DIFF_KEY_LANES = -(-(_AUG + 4 * len(LOG2E_PARTS)) // BF16_SUBLANES) * BF16_SUBLANES


def _diff_kernel(lam_ref, g_ref, q_ref, k_ref, vt_ref, o_ref, s0_ref, s1_ref, bias_ref, *,
                 n_chunks, hp, lam_init, slope_exp0):
    i = pl.program_id(2)
    tq = q_ref.shape[1]
    lane = lax.broadcasted_iota(jnp.int32, (tq, LANES), 1)
    digit = (lane - _AUG) & 3
    part = (lane - _AUG) >> 2
    is_bias_lane = (lane >= _AUG) & (lane < _AUG + 4 * len(LOG2E_PARTS))
    log2e_part = functools.reduce(lambda acc, x: jnp.where(part == x[0], x[1], acc),
                                  enumerate(LOG2E_PARTS), jnp.zeros((tq, LANES), F32))
    qpos = i * tq + lax.broadcasted_iota(jnp.int32, (tq, LANES), 0)
    q_hi = ((qpos >> 7) << 7).astype(F32)
    q_lo = (qpos & 127).astype(F32)
    own_dist = jnp.abs(lax.broadcasted_iota(jnp.int32, (ROW_BLOCK, tq), 1)
                       - lax.broadcasted_iota(jnp.int32, (ROW_BLOCK, tq), 0)).astype(F32)

    chains = []
    for n in range(hp):
        head = pl.program_id(1) * hp + n
        slope_bits = jnp.full((1, LANES), 127 - slope_exp0, jnp.int32) - head
        slope = lax.bitcast_convert_type(slope_bits << 23, F32)
        qf = q_ref[0, :, n * LANES:(n + 1) * LANES].astype(F32)
        aug = jnp.where(digit == 0, -slope * q_hi, jnp.where(digit == 1, -slope * q_lo, slope * log2e_part))
        aug = jnp.where(is_bias_lane, aug, 0.0)
        q1 = jnp.where(lane < DIFF_QK, qf, 0.0)
        q2 = jnp.where((lane >= DIFF_QK) & (lane < 2 * DIFF_QK), qf, 0.0)
        qt = jnp.concatenate([q1 + aug, q2 + aug], axis=0).T
        chains.append((qt[:DIFF_KEY_LANES].astype(BF16), n, n))
        bias = (-LOG2E) * slope[:, :1] * own_dist
        bias_ref[n, 0] = jnp.zeros((ROW_BLOCK, 2 * tq), F32)
        bias_ref[n, 1] = jnp.concatenate([bias, bias], axis=1)

    key_lane = lax.broadcasted_iota(jnp.int32, (1, DIFF_KEY_LANES), 1)
    key_bias_lane = (key_lane >= _AUG) & (key_lane < _AUG + 4 * len(LOG2E_PARTS))

    def edit_keys(kc, block):
        sign = jnp.where(block < i, 1.0, jnp.where(block > i, -1.0, 0.0))
        return kc * jnp.where(key_bias_lane, sign, 1.0).astype(BF16)

    def add_bias(s, n, block):
        return s + bias_ref[n, jnp.where(block == i, 1, 0)]

    accs = _pipelined_attention(k_ref, vt_ref, chains, jnp.exp2, n_chunks, ROW_BLOCK, s0_ref, s1_ref,
                                (edit_keys, add_bias))

    lp = lam_ref[...]
    lam = (jnp.exp(jnp.sum(lp[0:1] * lp[1:2], axis=-1, keepdims=True))
           - jnp.exp(jnp.sum(lp[2:3] * lp[3:4], axis=-1, keepdims=True)) + lam_init)
    for n, acc in enumerate(accs):
        o = _normalised(acc[:, :tq]) - lam * _normalised(acc[:, tq:])
        o = o * lax.rsqrt(jnp.mean(o * o, axis=0, keepdims=True) + NORM_EPS)
        o_ref[0, n * DIFF_V:(n + 1) * DIFF_V, :] = (o * g_ref[...] * (1.0 - lam_init)).astype(BF16)


def _diff_attention(q, k, vt, lam_params, subln_col, *, layer_idx, slope_exp0, heads_per_step):
    bsz, seq, _ = q.shape
    nch = vt.shape[2]
    tq = ROW_BLOCK
    hp = heads_per_step
    assert seq % tq == 0 and seq <= 256 * LANES and DIFF_HEADS % hp == 0 and nch % 2 == 0
    lam_init = 0.8 - 0.6 * math.exp(-0.3 * layer_idx)
    cps = math.gcd(CHUNKS_PER_STEP, nch // 2)
    s_buffer = pltpu.VMEM((hp, cps * KV_CHUNK, 2 * tq), F32)
    return pl.pallas_call(
        functools.partial(_diff_kernel, n_chunks=nch, hp=hp, lam_init=lam_init, slope_exp0=slope_exp0),
        grid=(bsz, DIFF_HEADS // hp, seq // tq),
        in_specs=[
            _resident(lam_params.shape),
            _resident(subln_col.shape),
            pl.BlockSpec((1, tq, hp * LANES), lambda b, g, i: (b, i, g)),
            pl.BlockSpec((1, seq, hp * LANES), lambda b, g, i: (b, 0, g)),
            pl.BlockSpec((1, hp, nch, V_ROWS, KV_CHUNK), lambda b, g, i: (b, g, 0, 0, 0)),
        ],
        out_specs=pl.BlockSpec((1, hp * DIFF_V, tq), lambda b, g, i: (b, g, i)),
        out_shape=jax.ShapeDtypeStruct((bsz, DIFF_HEADS * DIFF_V, seq), BF16),
        scratch_shapes=[s_buffer, s_buffer, pltpu.VMEM((hp, 2, ROW_BLOCK, 2 * tq), F32)],
        compiler_params=_cparams(("parallel", "parallel", "arbitrary"), ATTENTION_FLAGS),
    )(lam_params, subln_col, q, k, vt)


_WIN_OFFSETS = (-1, 0, 1)
_WIN_NO_BLOCK = len(_WIN_OFFSETS)


def _window_bias_tiles():
    key = np.arange(WIN_TILE)[:, None]
    query = np.arange(WIN_TILE)[None, :]
    tiles = np.full((WIN_HEADS, _WIN_NO_BLOCK + 1, WIN_TILE, WIN_TILE), NEG_INF, np.float32)
    for head in range(WIN_HEADS):
        for sel, offset in enumerate(_WIN_OFFSETS):
            dist = np.abs(query - key - WIN_TILE * offset)
            tiles[head, sel] = np.where(dist <= WINDOW, -(2.0 ** -(head + 1)) * LOG2E * dist, NEG_INF)
    return tiles


def _window_kernel(sink_ref, bias_ref, q_ref, k_ref, vt_ref, o_ref, *, n_blocks):
    i = pl.program_id(1)
    lane = lax.broadcasted_iota(jnp.int32, (WIN_TILE, LANES), 1)
    group = WIN_HEADS // WIN_KV_HEADS
    blocks = []
    for offset in _WIN_OFFSETS:
        block = i + offset
        exists = (block >= 0) & (block < n_blocks)
        block = jnp.clip(block, 0, n_blocks - 1)
        keys = k_ref[0, pl.ds(pl.multiple_of(block * WIN_TILE, WIN_TILE), WIN_TILE), :]
        blocks.append((jnp.where(exists, offset + 1, _WIN_NO_BLOCK), block, keys))
    scores = []
    for head in range(WIN_HEADS):
        kv_head, slab = head // group, head % group
        qf = q_ref[0, :, slab * LANES:(slab + 1) * LANES].astype(F32)
        qt = jnp.where((lane >> HEAD_SHIFT) == kv_head, qf, 0.0).T.astype(BF16)
        scores.append([jnp.dot(keys, qt, preferred_element_type=F32) + bias_ref[head, sel]
                       for sel, _, keys in blocks])
    probs = []
    for head in range(WIN_HEADS):
        sink = jnp.full((1, WIN_TILE), sink_ref[head] * LOG2E, F32)
        m = functools.reduce(jnp.maximum, [jnp.max(s, axis=0, keepdims=True) for s in scores[head]], sink)
        probs.append(([jnp.exp2(s - m).astype(BF16) for s in scores[head]], jnp.exp2(sink - m)))
    for head in range(WIN_HEADS):
        acc = None
        for p, (_, block, _) in zip(probs[head][0], blocks):
            part = jnp.dot(vt_ref[0, head // group, block], p, preferred_element_type=F32)
            acc = part if acc is None else acc + part
        denominator = acc[HEAD_DIM:HEAD_DIM + 1] + probs[head][1]
        o_ref[0, head * HEAD_DIM:(head + 1) * HEAD_DIM, :] = (acc[:HEAD_DIM] / denominator).astype(BF16)


def _window_attention(q, k, vt, sink):
    bsz, seq, _ = q.shape
    n_blocks = vt.shape[2]
    bias = jnp.asarray(_window_bias_tiles())
    return pl.pallas_call(
        functools.partial(_window_kernel, n_blocks=n_blocks),
        grid=(bsz, n_blocks),
        in_specs=[
            pl.BlockSpec(memory_space=pltpu.SMEM),
            _resident(bias.shape),
            pl.BlockSpec((1, WIN_TILE, 2 * LANES), lambda b, i: (b, i, 0)),
            _single_buffered((1, seq, LANES), lambda b, i: (b, 0, 0)),
            _single_buffered((1, WIN_KV_HEADS, n_blocks, V_ROWS, WIN_TILE), lambda b, i: (b, 0, 0, 0, 0)),
        ],
        out_specs=pl.BlockSpec((1, WIN_HEADS * HEAD_DIM, WIN_TILE), lambda b, i: (b, 0, i)),
        out_shape=jax.ShapeDtypeStruct((bsz, WIN_HEADS * HEAD_DIM, seq), BF16),
        compiler_params=_cparams(("parallel", "arbitrary")),
    )(sink, bias, q, k, vt)


def _outproj_kernel(x_ref, oa_ref, ob_ref, oc_ref, od_ref, wt_ref, g_ref, b_ref, o_ref, *, alpha):
    mix_t = None
    for m, ref in enumerate((oa_ref, ob_ref, oc_ref, od_ref)):
        part = jnp.dot(wt_ref[:, m * 256:(m + 1) * 256], ref[0], preferred_element_type=F32)
        mix_t = part if mix_t is None else mix_t + part
    y = alpha * x_ref[0] + mix_t.T
    o_ref[0] = _layer_norm_rows(y, g_ref[...], b_ref[...])


def _outproj(x, o_a, o_b, o_c, o_d, w_out_t, g, b, alpha):
    bsz, seq, d = x.shape
    tm = min(TOK_TILE, seq)
    ot = pl.BlockSpec((1, 256, tm), lambda bb, i: (bb, 0, i))
    tok = pl.BlockSpec((1, tm, d), lambda bb, i: (bb, i, 0))
    return pl.pallas_call(
        functools.partial(_outproj_kernel, alpha=alpha),
        grid=(bsz, seq // tm),
        in_specs=[tok, ot, ot, ot, ot, _resident(w_out_t.shape), _resident(g.shape), _resident(b.shape)],
        out_specs=tok,
        out_shape=jax.ShapeDtypeStruct((bsz, seq, d), F32),
        compiler_params=_cparams(("parallel", "parallel")),
    )(x, o_a, o_b, o_c, o_d, w_out_t, g, b)


def _rot_half_cols(start, width):
    half = width // 2
    src = np.concatenate([np.arange(start + half, start + width), np.arange(start, start + half)])
    sgn = np.concatenate([-np.ones(half), np.ones(half)])
    return src, sgn


def _gqa_slab_order(n_heads, n_kv):
    group = n_heads // n_kv
    return [kv * group + s for s in range(group) for kv in range(n_kv)]


def _layer_params(w_in, mla_q_norm, mla_w_uq, mla_kv_norm, mla_w_ukv, ax_q_norm, ax_k_norm):
    d = w_in.shape[0]
    zeros = lambda n: jnp.zeros((d, n), w_in.dtype)
    a0, b0, c0, d0 = 0, A_COLS, A_COLS + B_COLS, A_COLS + B_COLS + C_COLS
    head_cols = lambda base, h: np.arange(base + h * HEAD_DIM, base + (h + 1) * HEAD_DIM)

    def axial_rot(cols):
        src, sgn = [], []
        for blk in range(0, HEAD_DIM, HEAD_DIM // 2):
            s_, g_ = _rot_half_cols(blk, HEAD_DIM // 2)
            src.append(cols[s_])
            sgn.append(g_)
        return np.concatenate(src), np.concatenate(sgn)

    order = _gqa_slab_order(AX_HEADS, AX_KV_HEADS)
    aq = np.concatenate([head_cols(a0, h) for h in order])
    ak = np.arange(a0 + 256, a0 + 384)
    cq = np.concatenate([head_cols(c0, h) for h in order])
    ck = np.arange(c0 + 256, c0 + 384)
    cq_rot = [axial_rot(head_cols(c0, h)) for h in order]
    ck_rot = [axial_rot(head_cols(c0 + 256, h)) for h in range(AX_KV_HEADS)]

    def gather(src, sgn=None):
        w = w_in[:, np.asarray(src)]
        return w if sgn is None else w * jnp.asarray(sgn, w.dtype)[None, :]

    def widen(cols):
        return jnp.concatenate([gather(cols), zeros(LANES - len(cols))], axis=1)

    kr = np.arange(b0 + MLA_Q_RANK + MLA_KV_RANK, b0 + B_COLS)
    kr_src, kr_sgn = _rot_half_cols(kr[0], MLA_ROPE)
    place_rope = lambda w: jnp.concatenate([zeros(MLA_NOPE), w, zeros(LANES - MLA_NOPE - MLA_ROPE)], axis=1)

    w_tok = jnp.concatenate(
        [gather(aq), gather(ak), gather(cq), gather(ck)]
        + [gather(s, g) for s, g in cq_rot] + [gather(s, g) for s, g in ck_rot]
        + [widen(head_cols(d0, h)) for h in range(DIFF_HEADS)]
        + [widen(head_cols(d0 + 256, h)) for h in range(DIFF_HEADS)]
        + [gather(np.arange(b0, b0 + MLA_Q_RANK + MLA_KV_RANK)),
           place_rope(gather(kr)), place_rope(gather(kr_src, kr_sgn))], axis=1)
    assert w_tok.shape[1] == _C_END
    v_cols = np.concatenate([np.arange(a0 + 384, a0 + 512), np.arange(c0 + 384, c0 + 512),
                             np.arange(d0 + 512, d0 + 768)])
    w_vt = w_in[:, v_cols].T

    qd = MLA_NOPE + MLA_ROPE
    zq = lambda n: jnp.zeros((MLA_Q_RANK, n), mla_w_uq.dtype)
    uq, uq_rot = [], []
    for h in range(MLA_HEADS):
        blk = mla_w_uq[:, h * qd:(h + 1) * qd]
        src, sgn = _rot_half_cols(MLA_NOPE, MLA_ROPE)
        rot = blk[:, src] * jnp.asarray(sgn, blk.dtype)[None, :]
        uq += [blk, zq(LANES - qd)]
        uq_rot += [zq(MLA_NOPE), rot, zq(LANES - qd)]
    ukv = mla_w_ukv.reshape(MLA_KV_RANK, MLA_HEADS, MLA_NOPE + MLA_V)
    ukv_k = jnp.concatenate([ukv[:, :, :MLA_NOPE], jnp.zeros((MLA_KV_RANK, MLA_HEADS, LANES - MLA_NOPE), ukv.dtype)],
                            axis=2).reshape(MLA_KV_RANK, MLA_HEADS * LANES)
    ukv_vt = ukv[:, :, MLA_NOPE:].reshape(MLA_KV_RANK, MLA_HEADS * MLA_V).T

    def ax_gain(g):
        src, sgn = axial_rot(np.arange(HEAD_DIM))
        return jnp.tile(g, 2)[None, :].astype(F32), jnp.tile(g[src], 2)[None, :].astype(F32)

    ax_q, ax_q_rot = ax_gain(ax_q_norm)
    ax_k, ax_k_rot = ax_gain(ax_k_norm)
    lane = np.arange(LANES)
    g_avg = (lane[:, None] // HEAD_DIM == lane[None, :] // HEAD_DIM).astype(np.float32) / HEAD_DIM
    return {
        "w_tok": w_tok.astype(BF16), "w_vt": w_vt.astype(BF16),
        "w_uq": jnp.concatenate(uq, axis=1).astype(BF16), "w_uq_rot": jnp.concatenate(uq_rot, axis=1).astype(BF16),
        "w_ukv_k": ukv_k.astype(BF16), "w_ukv_vt": ukv_vt.astype(BF16),
        "g_avg": jnp.asarray(g_avg, BF16),
        "g_q": mla_q_norm[None, :].astype(F32), "g_kv": mla_kv_norm[None, :].astype(F32),
        "ax_q": ax_q, "ax_q_rot": ax_q_rot, "ax_k": ax_k, "ax_k_rot": ax_k_rot,
    }


def _position_tables(seq):
    pos = jnp.arange(seq, dtype=jnp.int32)
    half = HEAD_DIM // 2

    def angles(p, dim):
        inv = ROPE_BASE ** (-jnp.arange(0, dim, 2, dtype=F32) / dim)
        ang = p.astype(F32)[:, None] * inv[None, :]
        return jnp.cos(ang), jnp.sin(ang)

    cb, sb = angles(pos, MLA_ROPE)
    pad = LANES - MLA_NOPE - MLA_ROPE
    cos_b = jnp.concatenate([jnp.ones((seq, MLA_NOPE), F32), cb, cb, jnp.zeros((seq, pad), F32)], axis=1)
    sin_b = jnp.concatenate([jnp.zeros((seq, MLA_NOPE), F32), sb, sb, jnp.zeros((seq, pad), F32)], axis=1)
    cr, sr = angles(pos // GRID_W, half)
    cc, sc = angles(pos % GRID_W, half)
    cos_c = jnp.tile(jnp.concatenate([cr, cr, cc, cc], axis=1), (1, 2))
    sin_c = jnp.tile(jnp.concatenate([sr, sr, sc, sc], axis=1), (1, 2))
    rel = np.arange(LANES) - _AUG
    in_aug = (rel >= 0) & (rel < 4 * len(LOG2E_PARTS))
    part_row = np.where(in_aug, np.asarray(LOG2E_PARTS + (0.0,), np.float32)[np.clip(rel >> 2, 0, len(LOG2E_PARTS))], 0.0)
    digit = jnp.asarray((rel & 3)[None, :])
    pos_hi = ((pos >> 7) << 7).astype(F32)[:, None]
    pos_lo = (pos & 127).astype(F32)[:, None]
    aug = jnp.where(digit < 2, jnp.asarray(part_row, F32)[None, :], jnp.where(digit == 2, pos_hi, pos_lo))
    aug = jnp.where(jnp.asarray(in_aug[None, :]), aug, 0.0)
    return {"cos_b": cos_b, "sin_b": sin_b, "cos_c": cos_c, "sin_c": sin_c, "aug_k": aug}


def _ffn_weights(w_gu, w_down):
    *lead, d, two_ff = w_gu.shape
    n_chunks = two_ff // 2 // FF_CHUNK
    nl = len(lead)
    wgu = w_gu.astype(BF16).reshape(*lead, d, 2, n_chunks, FF_CHUNK)
    wgu = wgu.transpose(*range(nl), nl + 2, nl, nl + 1, nl + 3).reshape(*lead, n_chunks, d, 2 * FF_CHUNK)
    wd = w_down.astype(BF16).reshape(*lead, n_chunks, FF_CHUNK, d)
    return wgu, wd


def kernel(x, w_in, win_sink, mla_q_norm, mla_w_uq, mla_kv_norm, mla_w_ukv, ax_q_norm, ax_k_norm,
           diff_lambda, diff_subln, w_out, ffn_w_gu, ffn_w_down, ln_g, ln_b):
    depth = w_in.shape[0]
    bsz, seq, d = x.shape
    alpha = (2 * depth) ** 0.25
    tabs = _position_tables(seq)
    wgu_all, wd_all = _ffn_weights(ffn_w_gu, ffn_w_down)
    w_out_t = jnp.swapaxes(w_out, 1, 2).astype(BF16)
    row = lambda v: v[None, :].astype(F32)
    for l in range(depth):
        x = _ffn(x.reshape(bsz * seq, d), wgu_all[l, 0], wd_all[l, 0], row(ln_g[l, 0]), row(ln_b[l, 0]), alpha).reshape(bsz, seq, d)

        p = _layer_params(w_in[l], mla_q_norm[l], mla_w_uq[l], mla_kv_norm[l], mla_w_ukv[l],
                          ax_q_norm[l], ax_k_norm[l])
        qa, ka, vta, qb, kb, vtb, qc, kc, vtc, qd, kd, vtd = _prep(x, p, tabs)
        o_a = _window_attention(qa, ka, vta, win_sink[l].astype(F32))
        o_b = _dense_attention(qb, kb, vtb, n_heads=MLA_HEADS, kv_group=0, heads_per_step=HEADS_PER_STEP,
                               key_lanes=MLA_NOPE + MLA_ROPE)
        o_c = _dense_attention(qc, kc, vtc, n_heads=AX_HEADS, kv_group=AX_HEADS // AX_KV_HEADS,
                               heads_per_step=AX_HEADS // AX_KV_HEADS, key_lanes=LANES)
        o_d = _diff_attention(qd, kd, vtd, diff_lambda[l].astype(F32), diff_subln[l][:, None].astype(F32),
                              layer_idx=l, slope_exp0=WIN_HEADS + 1, heads_per_step=HEADS_PER_STEP)
        x = _outproj(x, o_a, o_b, o_c, o_d, w_out_t[l], row(ln_g[l, 1]), row(ln_b[l, 1]), alpha)

        x = _ffn(x.reshape(bsz * seq, d), wgu_all[l, 1], wd_all[l, 1], row(ln_g[l, 2]), row(ln_b[l, 2]), alpha).reshape(bsz, seq, d)
    return x
```

```python
import functools
import math

import numpy as np
import jax
import jax.numpy as jnp
from jax import lax
from jax.experimental import pallas as pl
from jax.experimental.pallas import tpu as pltpu

F32 = jnp.float32
BF16 = jnp.bfloat16

HEAD_DIM = 64
HEAD_SHIFT = 6
GRID_W = 64
WIN_HEADS, WIN_KV_HEADS, WINDOW = 4, 2, 128
MLA_HEADS, MLA_Q_RANK, MLA_KV_RANK, MLA_NOPE, MLA_ROPE, MLA_V = 4, 256, 128, 64, 32, 64
AX_HEADS, AX_KV_HEADS = 4, 2
DIFF_HEADS, DIFF_QK, DIFF_V = 4, 32, 64
A_COLS = (WIN_HEADS + 2 * WIN_KV_HEADS) * HEAD_DIM
B_COLS = MLA_Q_RANK + MLA_KV_RANK + MLA_ROPE
C_COLS = (AX_HEADS + 2 * AX_KV_HEADS) * HEAD_DIM
D_COLS = DIFF_HEADS * (4 * DIFF_QK + DIFF_V)
ROPE_BASE = 10000.0
NORM_EPS = 1e-5
NEG_INF = -1e30

LANES = 128
F32_SUBLANES = 8
BF16_SUBLANES = 16
V_ROWS = HEAD_DIM + BF16_SUBLANES
VMEM_LIMIT = 56 * 1024 * 1024

TOK_TILE = 512
FFN_TOK_TILE = 1024
KV_CHUNK = 512
Q_TILE = 512
FF_CHUNK = 256
HEADS_PER_STEP = 2
CHUNKS_PER_STEP = 4
ROW_BLOCK = 256
DENSE_ROW_BLOCK = 256
WIN_TILE = 256
assert WIN_TILE >= WINDOW and KV_CHUNK % WIN_TILE == 0


def _cparams(sem, flags=None):
    return pltpu.CompilerParams(dimension_semantics=sem, vmem_limit_bytes=VMEM_LIMIT, flags=flags)


ATTENTION_FLAGS = None


def _resident(shape):
    nd = len(shape)
    return pl.BlockSpec(shape, lambda *_: (0,) * nd, pipeline_mode=pl.Buffered(1))


def _layer_norm_rows(y, g, b):
    mu = jnp.mean(y, axis=-1, keepdims=True)
    yc = y - mu
    var = jnp.mean(yc * yc, axis=-1, keepdims=True)
    return yc * lax.rsqrt(var + NORM_EPS) * g + b


def _ffn_kernel(x_ref, wgu_ref, wd_ref, g_ref, b_ref, o_ref, acc_ref, *, alpha, n_chunks):
    x = x_ref[...]
    xb = x.astype(BF16)

    def chunk(c):
        gu = jnp.dot(xb, wgu_ref[c], preferred_element_type=F32)
        gate = gu[:, :FF_CHUNK]
        up = gu[:, FF_CHUNK:]
        h = (gate * jax.nn.sigmoid(gate) * up).astype(BF16)
        return jnp.dot(h, wd_ref[c], preferred_element_type=F32)

    def body(c, carry):
        acc_ref[...] += chunk(c)
        return carry

    acc_ref[...] = chunk(0)
    lax.fori_loop(1, n_chunks, body, 0)
    y = alpha * x + 0.5 * acc_ref[...]
    o_ref[...] = _layer_norm_rows(y, g_ref[...], b_ref[...])


def _ffn(x2d, wgu, wd, g, b, alpha):
    n_tok, d = x2d.shape
    n_chunks = wgu.shape[0]
    tm = min(FFN_TOK_TILE, n_tok)
    return pl.pallas_call(
        functools.partial(_ffn_kernel, alpha=alpha, n_chunks=n_chunks),
        grid=(n_tok // tm,),
        in_specs=[
            pl.BlockSpec((tm, d), lambda i: (i, 0)),
            _resident(wgu.shape),
            _resident(wd.shape),
            _resident(g.shape),
            _resident(b.shape),
        ],
        out_specs=pl.BlockSpec((tm, d), lambda i: (i, 0)),
        out_shape=jax.ShapeDtypeStruct((n_tok, d), F32),
        scratch_shapes=[pltpu.VMEM((tm, d), F32)],
        compiler_params=_cparams(("parallel",)),
    )(x2d, wgu, wd, g, b)


_C_AQ, _C_AK, _C_CQ, _C_CK, _C_CQR, _C_CKR = 0, 256, 384, 640, 768, 1024
_C_DQ, _C_DK, _C_BQ, _C_BKV, _C_BKR, _C_BKRR, _C_END = 1152, 1664, 2176, 2432, 2560, 2688, 2816
_R_AV, _R_CV, _R_DV, _R_END = 0, 128, 256, 512


def _split_hi_lo(v):
    hi = v.astype(BF16)
    lo = (v - hi.astype(F32)).astype(BF16)
    return hi, lo


def _prep_kernel(x_ref, wtok_ref, wvt_ref, wuq_ref, wuqr_ref, wukvk_ref, wukvvt_ref, gavg_ref,
                 gq_ref, gkv_ref, axq_ref, axqr_ref, axk_ref, axkr_ref,
                 cosb_ref, sinb_ref, cosc_ref, sinc_ref, augk_ref,
                 qa_ref, ka_ref, vta_ref, qb_ref, kb_ref, vtb_ref,
                 qc_ref, kc_ref, vtc_ref, qd_ref, kd_ref, vtd_ref):
    xb = x_ref[0].astype(BF16)
    tm = xb.shape[0]
    pm = jnp.dot(xb, wtok_ref[...], preferred_element_type=F32)
    nt = (((1,), (1,)), ((), ()))
    pvt = lax.dot_general(wvt_ref[...], xb, nt, preferred_element_type=F32)

    def ones_row_tile(width):
        row = lax.broadcasted_iota(jnp.int32, (BF16_SUBLANES, width), 0)
        return jnp.where(row == 0, 1.0, 0.0).astype(BF16)

    ones_tile, ones_tile_narrow = ones_row_tile(tm), ones_row_tile(WIN_TILE)

    def put_vt(ref, head, vt):
        width = ref.shape[-1]
        ones = ones_tile if width == tm else ones_tile_narrow
        for piece in range(tm // width):
            ref[0, head, piece, 0:HEAD_DIM, :] = vt[:, piece * width:(piece + 1) * width].astype(BF16)
            ref[0, head, piece, HEAD_DIM:V_ROWS, :] = ones

    qa_ref[0] = (pm[:, _C_AQ:_C_AQ + 256] * (HEAD_DIM ** -0.5 * LOG2E)).astype(BF16)
    ka_ref[0] = pm[:, _C_AK:_C_AK + 128].astype(BF16)
    for h in range(WIN_KV_HEADS):
        put_vt(vta_ref, h, pvt[_R_AV + 64 * h:_R_AV + 64 * (h + 1)])

    cosc = cosc_ref[...]
    sinc = sinc_ref[...]
    gavg = gavg_ref[...]

    def head_rms_scale(v):
        hi, lo = _split_hi_lo(v * v)
        ms = (jnp.dot(hi, gavg, preferred_element_type=F32)
              + jnp.dot(lo, gavg, preferred_element_type=F32))
        return lax.rsqrt(ms + NORM_EPS)

    def axial(off, off_rot, g_ref_, gr_ref_, scale):
        v = pm[:, off:off + LANES]
        r = head_rms_scale(v)
        vn = v * r * g_ref_[...]
        vrn = pm[:, off_rot:off_rot + LANES] * r * gr_ref_[...]
        return ((vn * cosc + vrn * sinc) * scale).astype(BF16)

    for grp in range(2):
        qc_ref[0, :, grp * LANES:(grp + 1) * LANES] = axial(
            _C_CQ + grp * LANES, _C_CQR + grp * LANES, axq_ref, axqr_ref, HEAD_DIM ** -0.5 * LOG2E)
    kc_ref[0] = axial(_C_CK, _C_CKR, axk_ref, axkr_ref, 1.0)
    for h in range(AX_KV_HEADS):
        put_vt(vtc_ref, h, pvt[_R_CV + 64 * h:_R_CV + 64 * (h + 1)])

    qd_ref[0] = (pm[:, _C_DQ:_C_DQ + 512] * (DIFF_QK ** -0.5 * LOG2E)).astype(BF16)
    augk = augk_ref[...]
    for h in range(DIFF_HEADS):
        kd_ref[0, :, h * LANES:(h + 1) * LANES] = (
            pm[:, _C_DK + h * LANES:_C_DK + (h + 1) * LANES] + augk).astype(BF16)
        put_vt(vtd_ref, h, pvt[_R_DV + 64 * h:_R_DV + 64 * (h + 1)])

    cosb = cosb_ref[...]
    sinb = sinb_ref[...]
    cq = pm[:, _C_BQ:_C_BQ + MLA_Q_RANK]
    cqn = (cq * lax.rsqrt(jnp.mean(cq * cq, axis=-1, keepdims=True) + NORM_EPS) * gq_ref[...]).astype(BF16)
    qw = jnp.dot(cqn, wuq_ref[...], preferred_element_type=F32)
    qwr = jnp.dot(cqn, wuqr_ref[...], preferred_element_type=F32)
    ckv = pm[:, _C_BKV:_C_BKV + MLA_KV_RANK]
    ckvn = (ckv * lax.rsqrt(jnp.mean(ckv * ckv, axis=-1, keepdims=True) + NORM_EPS) * gkv_ref[...]).astype(BF16)
    kw = jnp.dot(ckvn, wukvk_ref[...], preferred_element_type=F32)
    vbt = lax.dot_general(wukvvt_ref[...], ckvn, nt, preferred_element_type=F32)
    k_rope = pm[:, _C_BKR:_C_BKR + LANES] * cosb + pm[:, _C_BKRR:_C_BKRR + LANES] * sinb
    q_scale = (MLA_NOPE + MLA_ROPE) ** -0.5 * LOG2E
    for h in range(MLA_HEADS):
        sl = slice(h * LANES, (h + 1) * LANES)
        qb_ref[0, :, sl] = ((qw[:, sl] * cosb + qwr[:, sl] * sinb) * q_scale).astype(BF16)
        kb_ref[0, :, sl] = (kw[:, sl] + k_rope).astype(BF16)
        put_vt(vtb_ref, h, vbt[64 * h:64 * (h + 1)])


def _prep(x, p, tabs):
    bsz, seq, d = x.shape
    tm = KV_CHUNK
    nch = seq // tm
    tok = lambda w: pl.BlockSpec((1, tm, w), lambda b, i: (b, i, 0))
    tab = pl.BlockSpec((tm, LANES), lambda b, i: (i, 0))
    vt = lambda nh, width=tm: pl.BlockSpec((1, nh, tm // width, V_ROWS, width), lambda b, i: (b, 0, i, 0, 0))
    vt_shape = lambda nh, width=tm: jax.ShapeDtypeStruct((bsz, nh, seq // width, V_ROWS, width), BF16)
    tok_shape = lambda w: jax.ShapeDtypeStruct((bsz, seq, w), BF16)
    weights = [p["w_tok"], p["w_vt"], p["w_uq"], p["w_uq_rot"], p["w_ukv_k"], p["w_ukv_vt"], p["g_avg"],
               p["g_q"], p["g_kv"], p["ax_q"], p["ax_q_rot"], p["ax_k"], p["ax_k_rot"]]
    return pl.pallas_call(
        _prep_kernel,
        grid=(bsz, nch),
        in_specs=[tok(d)] + [_resident(w.shape) for w in weights] + [tab] * 5,
        out_specs=[tok(256), tok(128), vt(2, WIN_TILE), tok(512), tok(512), vt(4),
                   tok(256), tok(128), vt(2), tok(512), tok(512), vt(4)],
        out_shape=[tok_shape(256), tok_shape(128), vt_shape(2, WIN_TILE), tok_shape(512), tok_shape(512), vt_shape(4),
                   tok_shape(256), tok_shape(128), vt_shape(2), tok_shape(512), tok_shape(512), vt_shape(4)],
        compiler_params=_cparams(("parallel", "parallel")),
    )(x, *weights, tabs["cos_b"], tabs["sin_b"], tabs["cos_c"], tabs["sin_c"], tabs["aug_k"])


LOG2E = math.log2(math.e)


def _bf16_parts(x, n):
    parts = []
    for _ in range(n):
        part = float(np.float32(x).astype(jnp.bfloat16).astype(np.float32))
        parts.append(part)
        x -= part
    return tuple(parts)


LOG2E_PARTS = _bf16_parts(LOG2E, 3)


def _score_block(k_ref, q, k_slab, rb, row0, s_write, n, r, hooks):
    start = row0 + r * rb
    if not isinstance(start, int):
        start = pl.multiple_of(start, rb)
    kc = k_ref[0, pl.ds(start, rb), k_slab * LANES:k_slab * LANES + q.shape[0]]
    if hooks is not None:
        kc = hooks[0](kc, start // rb)
    s = jnp.dot(kc, q, preferred_element_type=F32)
    if hooks is not None:
        s = hooks[1](s, n, start // rb)
    s_write[n, r * rb:(r + 1) * rb, :] = s
    return jnp.max(s.reshape(rb // F32_SUBLANES, F32_SUBLANES, s.shape[-1]), axis=0)


def _first_scores(k_ref, chains, cps, rb, s_write, hooks):
    out = []
    for n, (q, k_slab, _) in enumerate(chains):
        parts = [_score_block(k_ref, q, k_slab, rb, 0, s_write, n, r, hooks) for r in range(cps * KV_CHUNK // rb)]
        out.append(functools.reduce(jnp.maximum, parts))
    return out


def _pipelined_step(k_ref, vt_ref, chains, exp_fn, cps, rb, hooks, j, carry, s_read, s_write, ahead=None):
    step_rows = cps * KV_CHUNK
    blocks_per_chunk = KV_CHUNK // rb
    ahead_chains, ahead_hooks, next_row0 = (chains, hooks, (j + 1) * step_rows) if ahead is None else (*ahead, 0)
    out = []
    for n, ((_, _, v_head), (m, acc, smax)) in enumerate(zip(chains, carry)):
        q, k_slab, _ = ahead_chains[n]
        m_new = jnp.maximum(m, jnp.max(smax, axis=0, keepdims=True))
        alpha = exp_fn(m - m_new)
        next_max, pv = None, None
        for r in range(step_rows // rb):
            part = _score_block(k_ref, q, k_slab, rb, next_row0, s_write, n, r, ahead_hooks)
            next_max = part if r == 0 else jnp.maximum(next_max, part)
            p = exp_fn(s_read[n, r * rb:(r + 1) * rb, :] - m_new).astype(BF16)
            c, b = divmod(r, blocks_per_chunk)
            part = jnp.dot(vt_ref[0, v_head, j * cps + c, :, b * rb:(b + 1) * rb], p, preferred_element_type=F32)
            pv = part if pv is None else pv + part
        out.append((m_new, acc * alpha + pv, next_max))
    return tuple(out)


def _pipelined_attention(k_ref, vt_ref, chains, next_chains, exp_fn, n_chunks, rb, s0_ref, s1_ref, max_ref,
                         is_first_tile, hooks=None, next_hooks=None):
    nq = chains[0][0].shape[1]
    cps = s0_ref.shape[1] // KV_CHUNK
    n_steps = n_chunks // cps
    step = functools.partial(_pipelined_step, k_ref, vt_ref, chains, exp_fn, cps, rb, hooks)

    @pl.when(is_first_tile)
    def _():
        for n, first_max in enumerate(_first_scores(k_ref, chains, cps, rb, s0_ref, hooks)):
            max_ref[n] = first_max

    init = tuple((jnp.full((1, nq), NEG_INF, F32), jnp.zeros((V_ROWS, nq), F32), max_ref[n])
                 for n in range(len(chains)))

    def pair(t, carry):
        carry = step(2 * t, carry, s0_ref, s1_ref)
        return step(2 * t + 1, carry, s1_ref, s0_ref)

    carry = lax.fori_loop(0, n_steps // 2 - 1, pair, init)
    carry = step(n_steps - 2, carry, s0_ref, s1_ref)
    final = step(n_steps - 1, carry, s1_ref, s0_ref, ahead=(next_chains, next_hooks))
    for n, (_, _, next_max) in enumerate(final):
        max_ref[n] = next_max
    return [acc for _, acc, _ in final]


def _normalised(acc):
    return acc[:HEAD_DIM] / acc[HEAD_DIM:HEAD_DIM + 1]


def _single_buffered(block, index_map):
    return pl.BlockSpec(block, index_map, pipeline_mode=pl.Buffered(1))


def _dense_kernel(q_ref, q_next_ref, k_ref, vt_ref, o_ref, s0_ref, s1_ref, max_ref, *, n_chunks, heads, key_lanes):
    tq = q_ref.shape[1]
    lane = lax.broadcasted_iota(jnp.int32, (tq, LANES), 1)

    def tile_chains(ref):
        chains = []
        for q_slab, k_slab, v_head, shared_key_slab in heads:
            q = ref[0, :, q_slab * LANES:(q_slab + 1) * LANES].astype(F32)
            if shared_key_slab:
                q = jnp.where((lane >> HEAD_SHIFT) == pl.program_id(1), q, 0.0)
            chains.append((q.T[:key_lanes].astype(BF16), k_slab, v_head))
        return chains

    accs = _pipelined_attention(k_ref, vt_ref, tile_chains(q_ref), tile_chains(q_next_ref), jnp.exp2, n_chunks,
                                DENSE_ROW_BLOCK, s0_ref, s1_ref, max_ref, pl.program_id(2) == 0)
    for n, acc in enumerate(accs):
        o_ref[0, n * HEAD_DIM:(n + 1) * HEAD_DIM, :] = _normalised(acc).astype(BF16)


def _dense_attention(q, k, vt, *, n_heads, kv_group, heads_per_step, key_lanes):
    bsz, seq, _ = q.shape
    nch = vt.shape[2]
    tq = min(Q_TILE, seq)
    hp = heads_per_step
    if kv_group:
        assert hp == kv_group and n_heads // kv_group == LANES // HEAD_DIM
        heads = tuple((s, 0, 0, True) for s in range(kv_group))
        q_block, q_slab_of = (1, tq, kv_group * LANES), lambda g: 0
        k_spec = pl.BlockSpec((1, seq, LANES), lambda b, g, i: (b, 0, 0))
        vt_spec = pl.BlockSpec((1, 1, nch, V_ROWS, KV_CHUNK), lambda b, g, i: (b, g, 0, 0, 0))
    else:
        heads = tuple((n, n, n, False) for n in range(hp))
        q_block, q_slab_of = (1, tq, hp * LANES), lambda g: g
        k_spec = pl.BlockSpec((1, seq, hp * LANES), lambda b, g, i: (b, 0, g))
        vt_spec = pl.BlockSpec((1, hp, nch, V_ROWS, KV_CHUNK), lambda b, g, i: (b, g, 0, 0, 0))
    n_tiles = seq // tq
    q_spec = pl.BlockSpec(q_block, lambda b, g, i: (b, i, q_slab_of(g)))
    q_next_spec = pl.BlockSpec(q_block, lambda b, g, i: (b, jnp.minimum(i + 1, n_tiles - 1), q_slab_of(g)))
    assert nch % 2 == 0
    cps = math.gcd(CHUNKS_PER_STEP, nch // 2)
    s_buffer = pltpu.VMEM((hp, cps * KV_CHUNK, tq), F32)
    return pl.pallas_call(
        functools.partial(_dense_kernel, n_chunks=nch, heads=heads, key_lanes=key_lanes),
        grid=(bsz, n_heads // hp, n_tiles),
        in_specs=[q_spec, q_next_spec, k_spec, vt_spec],
        out_specs=pl.BlockSpec((1, hp * HEAD_DIM, tq), lambda b, g, i: (b, g, i)),
        out_shape=jax.ShapeDtypeStruct((bsz, n_heads * HEAD_DIM, seq), BF16),
        scratch_shapes=[s_buffer, s_buffer, pltpu.VMEM((hp, F32_SUBLANES, tq), F32)],
        compiler_params=_cparams(("parallel", "parallel", "arbitrary"), ATTENTION_FLAGS),
    )(q, q, k, vt)


_AUG = 2 * DIFF_QK
DIFF_KEY_LANES = -(-(_AUG + 4 * len(LOG2E_PARTS)) // BF16_SUBLANES) * BF16_SUBLANES


def _diff_kernel(lam_ref, g_ref, q_ref, q_next_ref, k_ref, vt_ref, o_ref, s0_ref, s1_ref, max_ref, bias_ref, *,
                 n_chunks, hp, lam_init, slope_exp0):
    i = pl.program_id(2)
    tq = q_ref.shape[1]
    lane = lax.broadcasted_iota(jnp.int32, (tq, LANES), 1)
    digit = (lane - _AUG) & 3
    part = (lane - _AUG) >> 2
    is_bias_lane = (lane >= _AUG) & (lane < _AUG + 4 * len(LOG2E_PARTS))
    log2e_part = functools.reduce(lambda acc, x: jnp.where(part == x[0], x[1], acc),
                                  enumerate(LOG2E_PARTS), jnp.zeros((tq, LANES), F32))
    row = lax.broadcasted_iota(jnp.int32, (tq, LANES), 0)
    own_dist = jnp.abs(lax.broadcasted_iota(jnp.int32, (ROW_BLOCK, tq), 1)
                       - lax.broadcasted_iota(jnp.int32, (ROW_BLOCK, tq), 0)).astype(F32)
    key_lane = lax.broadcasted_iota(jnp.int32, (1, DIFF_KEY_LANES), 1)
    key_bias_lane = (key_lane >= _AUG) & (key_lane < _AUG + 4 * len(LOG2E_PARTS))

    slopes = []
    for n in range(hp):
        head = pl.program_id(1) * hp + n
        slope_bits = jnp.full((1, LANES), 127 - slope_exp0, jnp.int32) - head
        slope = lax.bitcast_convert_type(slope_bits << 23, F32)
        slopes.append(slope)
        bias = (-LOG2E) * slope[:, :1] * own_dist
        bias_ref[n, 0] = jnp.zeros((ROW_BLOCK, 2 * tq), F32)
        bias_ref[n, 1] = jnp.concatenate([bias, bias], axis=1)

    def tile(ref, t):
        qpos = t * tq + row
        q_hi = ((qpos >> 7) << 7).astype(F32)
        q_lo = (qpos & 127).astype(F32)
        chains = []
        for n, slope in enumerate(slopes):
            qf = ref[0, :, n * LANES:(n + 1) * LANES].astype(F32)
            aug = jnp.where(digit == 0, -slope * q_hi, jnp.where(digit == 1, -slope * q_lo, slope * log2e_part))
            aug = jnp.where(is_bias_lane, aug, 0.0)
            q1 = jnp.where(lane < DIFF_QK, qf, 0.0)
            q2 = jnp.where((lane >= DIFF_QK) & (lane < 2 * DIFF_QK), qf, 0.0)
            qt = jnp.concatenate([q1 + aug, q2 + aug], axis=0).T
            chains.append((qt[:DIFF_KEY_LANES].astype(BF16), n, n))

        def edit_keys(kc, block):
            sign = jnp.where(block < t, 1.0, jnp.where(block > t, -1.0, 0.0))
            return kc * jnp.where(key_bias_lane, sign, 1.0).astype(BF16)

        def add_bias(s, n, block):
            return s + bias_ref[n, jnp.where(block == t, 1, 0)]

        return chains, (edit_keys, add_bias)

    chains, hooks = tile(q_ref, i)
    next_chains, next_hooks = tile(q_next_ref, jnp.minimum(i + 1, pl.num_programs(2) - 1))
    accs = _pipelined_attention(k_ref, vt_ref, chains, next_chains, jnp.exp2, n_chunks, ROW_BLOCK, s0_ref, s1_ref,
                                max_ref, i == 0, hooks, next_hooks)

    lp = lam_ref[...]
    lam = (jnp.exp(jnp.sum(lp[0:1] * lp[1:2], axis=-1, keepdims=True))
           - jnp.exp(jnp.sum(lp[2:3] * lp[3:4], axis=-1, keepdims=True)) + lam_init)
    for n, acc in enumerate(accs):
        o = _normalised(acc[:, :tq]) - lam * _normalised(acc[:, tq:])
        o = o * lax.rsqrt(jnp.mean(o * o, axis=0, keepdims=True) + NORM_EPS)
        o_ref[0, n * DIFF_V:(n + 1) * DIFF_V, :] = (o * g_ref[...] * (1.0 - lam_init)).astype(BF16)


def _diff_attention(q, k, vt, lam_params, subln_col, *, layer_idx, slope_exp0, heads_per_step):
    bsz, seq, _ = q.shape
    nch = vt.shape[2]
    tq = ROW_BLOCK
    hp = heads_per_step
    assert seq % tq == 0 and seq <= 256 * LANES and DIFF_HEADS % hp == 0 and nch % 2 == 0
    lam_init = 0.8 - 0.6 * math.exp(-0.3 * layer_idx)
    cps = math.gcd(CHUNKS_PER_STEP, nch // 2)
    s_buffer = pltpu.VMEM((hp, cps * KV_CHUNK, 2 * tq), F32)
    n_tiles = seq // tq
    return pl.pallas_call(
        functools.partial(_diff_kernel, n_chunks=nch, hp=hp, lam_init=lam_init, slope_exp0=slope_exp0),
        grid=(bsz, DIFF_HEADS // hp, n_tiles),
        in_specs=[
            _resident(lam_params.shape),
            _resident(subln_col.shape),
            pl.BlockSpec((1, tq, hp * LANES), lambda b, g, i: (b, i, g)),
            pl.BlockSpec((1, tq, hp * LANES), lambda b, g, i: (b, jnp.minimum(i + 1, n_tiles - 1), g)),
            pl.BlockSpec((1, seq, hp * LANES), lambda b, g, i: (b, 0, g)),
            pl.BlockSpec((1, hp, nch, V_ROWS, KV_CHUNK), lambda b, g, i: (b, g, 0, 0, 0)),
        ],
        out_specs=pl.BlockSpec((1, hp * DIFF_V, tq), lambda b, g, i: (b, g, i)),
        out_shape=jax.ShapeDtypeStruct((bsz, DIFF_HEADS * DIFF_V, seq), BF16),
        scratch_shapes=[s_buffer, s_buffer, pltpu.VMEM((hp, F32_SUBLANES, 2 * tq), F32),
                        pltpu.VMEM((hp, 2, ROW_BLOCK, 2 * tq), F32)],
        compiler_params=_cparams(("parallel", "parallel", "arbitrary"), ATTENTION_FLAGS),
    )(lam_params, subln_col, q, q, k, vt)


_WIN_OFFSETS = (-1, 0, 1)
_WIN_NO_BLOCK = len(_WIN_OFFSETS)


def _window_bias_tiles():
    key = np.arange(WIN_TILE)[:, None]
    query = np.arange(WIN_TILE)[None, :]
    tiles = np.full((WIN_HEADS, _WIN_NO_BLOCK + 1, WIN_TILE, WIN_TILE), NEG_INF, np.float32)
    for head in range(WIN_HEADS):
        for sel, offset in enumerate(_WIN_OFFSETS):
            dist = np.abs(query - key - WIN_TILE * offset)
            tiles[head, sel] = np.where(dist <= WINDOW, -(2.0 ** -(head + 1)) * LOG2E * dist, NEG_INF)
    return tiles


def _window_kernel(sink_ref, bias_ref, q_ref, k_ref, vt_ref, o_ref, *, n_blocks):
    i = pl.program_id(1)
    lane = lax.broadcasted_iota(jnp.int32, (WIN_TILE, LANES), 1)
    group = WIN_HEADS // WIN_KV_HEADS
    blocks = []
    for offset in _WIN_OFFSETS:
        block = i + offset
        exists = (block >= 0) & (block < n_blocks)
        block = jnp.clip(block, 0, n_blocks - 1)
        keys = k_ref[0, pl.ds(pl.multiple_of(block * WIN_TILE, WIN_TILE), WIN_TILE), :]
        blocks.append((jnp.where(exists, offset + 1, _WIN_NO_BLOCK), block, keys))
    scores = []
    for head in range(WIN_HEADS):
        kv_head, slab = head // group, head % group
        qf = q_ref[0, :, slab * LANES:(slab + 1) * LANES].astype(F32)
        qt = jnp.where((lane >> HEAD_SHIFT) == kv_head, qf, 0.0).T.astype(BF16)
        scores.append([jnp.dot(keys, qt, preferred_element_type=F32) + bias_ref[head, sel]
                       for sel, _, keys in blocks])
    probs = []
    for head in range(WIN_HEADS):
        sink = jnp.full((1, WIN_TILE), sink_ref[head] * LOG2E, F32)
        m = functools.reduce(jnp.maximum, [jnp.max(s, axis=0, keepdims=True) for s in scores[head]], sink)
        probs.append(([jnp.exp2(s - m).astype(BF16) for s in scores[head]], jnp.exp2(sink - m)))
    for head in range(WIN_HEADS):
        acc = None
        for p, (_, block, _) in zip(probs[head][0], blocks):
            part = jnp.dot(vt_ref[0, head // group, block], p, preferred_element_type=F32)
            acc = part if acc is None else acc + part
        denominator = acc[HEAD_DIM:HEAD_DIM + 1] + probs[head][1]
        o_ref[0, head * HEAD_DIM:(head + 1) * HEAD_DIM, :] = (acc[:HEAD_DIM] / denominator).astype(BF16)


def _window_attention(q, k, vt, sink):
    bsz, seq, _ = q.shape
    n_blocks = vt.shape[2]
    bias = jnp.asarray(_window_bias_tiles())
    return pl.pallas_call(
        functools.partial(_window_kernel, n_blocks=n_blocks),
        grid=(bsz, n_blocks),
        in_specs=[
            pl.BlockSpec(memory_space=pltpu.SMEM),
            _resident(bias.shape),
            pl.BlockSpec((1, WIN_TILE, 2 * LANES), lambda b, i: (b, i, 0)),
            _single_buffered((1, seq, LANES), lambda b, i: (b, 0, 0)),
            _single_buffered((1, WIN_KV_HEADS, n_blocks, V_ROWS, WIN_TILE), lambda b, i: (b, 0, 0, 0, 0)),
        ],
        out_specs=pl.BlockSpec((1, WIN_HEADS * HEAD_DIM, WIN_TILE), lambda b, i: (b, 0, i)),
        out_shape=jax.ShapeDtypeStruct((bsz, WIN_HEADS * HEAD_DIM, seq), BF16),
        compiler_params=_cparams(("parallel", "arbitrary")),
    )(sink, bias, q, k, vt)


def _outproj_kernel(x_ref, oa_ref, ob_ref, oc_ref, od_ref, wt_ref, g_ref, b_ref, o_ref, *, alpha):
    mix_t = None
    for m, ref in enumerate((oa_ref, ob_ref, oc_ref, od_ref)):
        part = jnp.dot(wt_ref[:, m * 256:(m + 1) * 256], ref[0], preferred_element_type=F32)
        mix_t = part if mix_t is None else mix_t + part
    y = alpha * x_ref[0] + mix_t.T
    o_ref[0] = _layer_norm_rows(y, g_ref[...], b_ref[...])


def _outproj(x, o_a, o_b, o_c, o_d, w_out_t, g, b, alpha):
    bsz, seq, d = x.shape
    tm = min(TOK_TILE, seq)
    ot = pl.BlockSpec((1, 256, tm), lambda bb, i: (bb, 0, i))
    tok = pl.BlockSpec((1, tm, d), lambda bb, i: (bb, i, 0))
    return pl.pallas_call(
        functools.partial(_outproj_kernel, alpha=alpha),
        grid=(bsz, seq // tm),
        in_specs=[tok, ot, ot, ot, ot, _resident(w_out_t.shape), _resident(g.shape), _resident(b.shape)],
        out_specs=tok,
        out_shape=jax.ShapeDtypeStruct((bsz, seq, d), F32),
        compiler_params=_cparams(("parallel", "parallel")),
    )(x, o_a, o_b, o_c, o_d, w_out_t, g, b)


def _rot_half_cols(start, width):
    half = width // 2
    src = np.concatenate([np.arange(start + half, start + width), np.arange(start, start + half)])
    sgn = np.concatenate([-np.ones(half), np.ones(half)])
    return src, sgn


def _gqa_slab_order(n_heads, n_kv):
    group = n_heads // n_kv
    return [kv * group + s for s in range(group) for kv in range(n_kv)]


def _layer_params(w_in, mla_q_norm, mla_w_uq, mla_kv_norm, mla_w_ukv, ax_q_norm, ax_k_norm):
    d = w_in.shape[0]
    zeros = lambda n: jnp.zeros((d, n), w_in.dtype)
    a0, b0, c0, d0 = 0, A_COLS, A_COLS + B_COLS, A_COLS + B_COLS + C_COLS
    head_cols = lambda base, h: np.arange(base + h * HEAD_DIM, base + (h + 1) * HEAD_DIM)

    def axial_rot(cols):
        src, sgn = [], []
        for blk in range(0, HEAD_DIM, HEAD_DIM // 2):
            s_, g_ = _rot_half_cols(blk, HEAD_DIM // 2)
            src.append(cols[s_])
            sgn.append(g_)
        return np.concatenate(src), np.concatenate(sgn)

    order = _gqa_slab_order(AX_HEADS, AX_KV_HEADS)
    aq = np.concatenate([head_cols(a0, h) for h in order])
    ak = np.arange(a0 + 256, a0 + 384)
    cq = np.concatenate([head_cols(c0, h) for h in order])
    ck = np.arange(c0 + 256, c0 + 384)
    cq_rot = [axial_rot(head_cols(c0, h)) for h in order]
    ck_rot = [axial_rot(head_cols(c0 + 256, h)) for h in range(AX_KV_HEADS)]

    def gather(src, sgn=None):
        w = w_in[:, np.asarray(src)]
        return w if sgn is None else w * jnp.asarray(sgn, w.dtype)[None, :]

    def widen(cols):
        return jnp.concatenate([gather(cols), zeros(LANES - len(cols))], axis=1)

    kr = np.arange(b0 + MLA_Q_RANK + MLA_KV_RANK, b0 + B_COLS)
    kr_src, kr_sgn = _rot_half_cols(kr[0], MLA_ROPE)
    place_rope = lambda w: jnp.concatenate([zeros(MLA_NOPE), w, zeros(LANES - MLA_NOPE - MLA_ROPE)], axis=1)

    w_tok = jnp.concatenate(
        [gather(aq), gather(ak), gather(cq), gather(ck)]
        + [gather(s, g) for s, g in cq_rot] + [gather(s, g) for s, g in ck_rot]
        + [widen(head_cols(d0, h)) for h in range(DIFF_HEADS)]
        + [widen(head_cols(d0 + 256, h)) for h in range(DIFF_HEADS)]
        + [gather(np.arange(b0, b0 + MLA_Q_RANK + MLA_KV_RANK)),
           place_rope(gather(kr)), place_rope(gather(kr_src, kr_sgn))], axis=1)
    assert w_tok.shape[1] == _C_END
    v_cols = np.concatenate([np.arange(a0 + 384, a0 + 512), np.arange(c0 + 384, c0 + 512),
                             np.arange(d0 + 512, d0 + 768)])
    w_vt = w_in[:, v_cols].T

    qd = MLA_NOPE + MLA_ROPE
    zq = lambda n: jnp.zeros((MLA_Q_RANK, n), mla_w_uq.dtype)
    uq, uq_rot = [], []
    for h in range(MLA_HEADS):
        blk = mla_w_uq[:, h * qd:(h + 1) * qd]
        src, sgn = _rot_half_cols(MLA_NOPE, MLA_ROPE)
        rot = blk[:, src] * jnp.asarray(sgn, blk.dtype)[None, :]
        uq += [blk, zq(LANES - qd)]
        uq_rot += [zq(MLA_NOPE), rot, zq(LANES - qd)]
    ukv = mla_w_ukv.reshape(MLA_KV_RANK, MLA_HEADS, MLA_NOPE + MLA_V)
    ukv_k = jnp.concatenate([ukv[:, :, :MLA_NOPE], jnp.zeros((MLA_KV_RANK, MLA_HEADS, LANES - MLA_NOPE), ukv.dtype)],
                            axis=2).reshape(MLA_KV_RANK, MLA_HEADS * LANES)
    ukv_vt = ukv[:, :, MLA_NOPE:].reshape(MLA_KV_RANK, MLA_HEADS * MLA_V).T

    def ax_gain(g):
        src, sgn = axial_rot(np.arange(HEAD_DIM))
        return jnp.tile(g, 2)[None, :].astype(F32), jnp.tile(g[src], 2)[None, :].astype(F32)

    ax_q, ax_q_rot = ax_gain(ax_q_norm)
    ax_k, ax_k_rot = ax_gain(ax_k_norm)
    lane = np.arange(LANES)
    g_avg = (lane[:, None] // HEAD_DIM == lane[None, :] // HEAD_DIM).astype(np.float32) / HEAD_DIM
    return {
        "w_tok": w_tok.astype(BF16), "w_vt": w_vt.astype(BF16),
        "w_uq": jnp.concatenate(uq, axis=1).astype(BF16), "w_uq_rot": jnp.concatenate(uq_rot, axis=1).astype(BF16),
        "w_ukv_k": ukv_k.astype(BF16), "w_ukv_vt": ukv_vt.astype(BF16),
        "g_avg": jnp.asarray(g_avg, BF16),
        "g_q": mla_q_norm[None, :].astype(F32), "g_kv": mla_kv_norm[None, :].astype(F32),
        "ax_q": ax_q, "ax_q_rot": ax_q_rot, "ax_k": ax_k, "ax_k_rot": ax_k_rot,
    }


def _position_tables(seq):
    pos = jnp.arange(seq, dtype=jnp.int32)
    half = HEAD_DIM // 2

    def angles(p, dim):
        inv = ROPE_BASE ** (-jnp.arange(0, dim, 2, dtype=F32) / dim)
        ang = p.astype(F32)[:, None] * inv[None, :]
        return jnp.cos(ang), jnp.sin(ang)

    cb, sb = angles(pos, MLA_ROPE)
    pad = LANES - MLA_NOPE - MLA_ROPE
    cos_b = jnp.concatenate([jnp.ones((seq, MLA_NOPE), F32), cb, cb, jnp.zeros((seq, pad), F32)], axis=1)
    sin_b = jnp.concatenate([jnp.zeros((seq, MLA_NOPE), F32), sb, sb, jnp.zeros((seq, pad), F32)], axis=1)
    cr, sr = angles(pos // GRID_W, half)
    cc, sc = angles(pos % GRID_W, half)
    cos_c = jnp.tile(jnp.concatenate([cr, cr, cc, cc], axis=1), (1, 2))
    sin_c = jnp.tile(jnp.concatenate([sr, sr, sc, sc], axis=1), (1, 2))
    rel = np.arange(LANES) - _AUG
    in_aug = (rel >= 0) & (rel < 4 * len(LOG2E_PARTS))
    part_row = np.where(in_aug, np.asarray(LOG2E_PARTS + (0.0,), np.float32)[np.clip(rel >> 2, 0, len(LOG2E_PARTS))], 0.0)
    digit = jnp.asarray((rel & 3)[None, :])
    pos_hi = ((pos >> 7) << 7).astype(F32)[:, None]
    pos_lo = (pos & 127).astype(F32)[:, None]
    aug = jnp.where(digit < 2, jnp.asarray(part_row, F32)[None, :], jnp.where(digit == 2, pos_hi, pos_lo))
    aug = jnp.where(jnp.asarray(in_aug[None, :]), aug, 0.0)
    return {"cos_b": cos_b, "sin_b": sin_b, "cos_c": cos_c, "sin_c": sin_c, "aug_k": aug}


def _ffn_weights(w_gu, w_down):
    *lead, d, two_ff = w_gu.shape
    n_chunks = two_ff // 2 // FF_CHUNK
    nl = len(lead)
    wgu = w_gu.astype(BF16).reshape(*lead, d, 2, n_chunks, FF_CHUNK)
    wgu = wgu.transpose(*range(nl), nl + 2, nl, nl + 1, nl + 3).reshape(*lead, n_chunks, d, 2 * FF_CHUNK)
    wd = w_down.astype(BF16).reshape(*lead, n_chunks, FF_CHUNK, d)
    return wgu, wd


def kernel(x, w_in, win_sink, mla_q_norm, mla_w_uq, mla_kv_norm, mla_w_ukv, ax_q_norm, ax_k_norm,
           diff_lambda, diff_subln, w_out, ffn_w_gu, ffn_w_down, ln_g, ln_b):
    depth = w_in.shape[0]
    bsz, seq, d = x.shape
    alpha = (2 * depth) ** 0.25
    tabs = _position_tables(seq)
    wgu_all, wd_all = _ffn_weights(ffn_w_gu, ffn_w_down)
    w_out_t = jnp.swapaxes(w_out, 1, 2).astype(BF16)
    row = lambda v: v[None, :].astype(F32)
    for l in range(depth):
        x = _ffn(x.reshape(bsz * seq, d), wgu_all[l, 0], wd_all[l, 0], row(ln_g[l, 0]), row(ln_b[l, 0]), alpha).reshape(bsz, seq, d)

        p = _layer_params(w_in[l], mla_q_norm[l], mla_w_uq[l], mla_kv_norm[l], mla_w_ukv[l],
                          ax_q_norm[l], ax_k_norm[l])
        qa, ka, vta, qb, kb, vtb, qc, kc, vtc, qd, kd, vtd = _prep(x, p, tabs)
        o_a = _window_attention(qa, ka, vta, win_sink[l].astype(F32))
        o_b = _dense_attention(qb, kb, vtb, n_heads=MLA_HEADS, kv_group=0, heads_per_step=HEADS_PER_STEP,
                               key_lanes=MLA_NOPE + MLA_ROPE)
        o_c = _dense_attention(qc, kc, vtc, n_heads=AX_HEADS, kv_group=AX_HEADS // AX_KV_HEADS,
                               heads_per_step=AX_HEADS // AX_KV_HEADS, key_lanes=LANES)
        o_d = _diff_attention(qd, kd, vtd, diff_lambda[l].astype(F32), diff_subln[l][:, None].astype(F32),
                              layer_idx=l, slope_exp0=WIN_HEADS + 1, heads_per_step=HEADS_PER_STEP)
        x = _outproj(x, o_a, o_b, o_c, o_d, w_out_t[l], row(ln_g[l, 1]), row(ln_b[l, 1]), alpha)

        x = _ffn(x.reshape(bsz * seq, d), wgu_all[l, 1], wd_all[l, 1], row(ln_g[l, 2]), row(ln_b[l, 2]), alpha).reshape(bsz, seq, d)
    return x
```

```python
import functools
import math

import numpy as np
import jax
import jax.numpy as jnp
from jax import lax
from jax.experimental import pallas as pl
from jax.experimental.pallas import tpu as pltpu

F32 = jnp.float32
BF16 = jnp.bfloat16

HEAD_DIM = 64
HEAD_SHIFT = 6
GRID_W = 64
WIN_HEADS, WIN_KV_HEADS, WINDOW = 4, 2, 128
MLA_HEADS, MLA_Q_RANK, MLA_KV_RANK, MLA_NOPE, MLA_ROPE, MLA_V = 4, 256, 128, 64, 32, 64
AX_HEADS, AX_KV_HEADS = 4, 2
DIFF_HEADS, DIFF_QK, DIFF_V = 4, 32, 64
A_COLS = (WIN_HEADS + 2 * WIN_KV_HEADS) * HEAD_DIM
B_COLS = MLA_Q_RANK + MLA_KV_RANK + MLA_ROPE
C_COLS = (AX_HEADS + 2 * AX_KV_HEADS) * HEAD_DIM
D_COLS = DIFF_HEADS * (4 * DIFF_QK + DIFF_V)
ROPE_BASE = 10000.0
NORM_EPS = 1e-5
NEG_INF = -1e30

LANES = 128
F32_SUBLANES = 8
BF16_SUBLANES = 16
V_ROWS = HEAD_DIM + BF16_SUBLANES
VMEM_LIMIT = 56 * 1024 * 1024

TOK_TILE = 512
FFN_TOK_TILE = 1024
KV_CHUNK = 512
Q_TILE = 512
FF_CHUNK = 256
HEADS_PER_STEP = 2
CHUNKS_PER_STEP = 4
ROW_BLOCK = 256
DENSE_ROW_BLOCK = 256
WIN_TILE = 256
assert WIN_TILE >= WINDOW and KV_CHUNK % WIN_TILE == 0


def _cparams(sem, flags=None):
    return pltpu.CompilerParams(dimension_semantics=sem, vmem_limit_bytes=VMEM_LIMIT, flags=flags)


ATTENTION_FLAGS = None


def _resident(shape):
    nd = len(shape)
    return pl.BlockSpec(shape, lambda *_: (0,) * nd, pipeline_mode=pl.Buffered(1))


def _layer_norm_rows(y, g, b):
    mu = jnp.mean(y, axis=-1, keepdims=True)
    yc = y - mu
    var = jnp.mean(yc * yc, axis=-1, keepdims=True)
    return yc * lax.rsqrt(var + NORM_EPS) * g + b


def _ffn_kernel(x_ref, wgu_ref, wd_ref, g_ref, b_ref, o_ref, acc_ref, *, alpha, n_chunks):
    x = x_ref[...]
    xb = x.astype(BF16)

    def chunk(c):
        gu = jnp.dot(xb, wgu_ref[c], preferred_element_type=F32)
        gate = gu[:, :FF_CHUNK]
        up = gu[:, FF_CHUNK:]
        h = (gate * jax.nn.sigmoid(gate) * up).astype(BF16)
        return jnp.dot(h, wd_ref[c], preferred_element_type=F32)

    def body(c, carry):
        acc_ref[...] += chunk(c)
        return carry

    acc_ref[...] = chunk(0)
    lax.fori_loop(1, n_chunks, body, 0)
    y = alpha * x + 0.5 * acc_ref[...]
    o_ref[...] = _layer_norm_rows(y, g_ref[...], b_ref[...])


def _ffn(x2d, wgu, wd, g, b, alpha):
    n_tok, d = x2d.shape
    n_chunks = wgu.shape[0]
    tm = min(FFN_TOK_TILE, n_tok)
    return pl.pallas_call(
        functools.partial(_ffn_kernel, alpha=alpha, n_chunks=n_chunks),
        grid=(n_tok // tm,),
        in_specs=[
            pl.BlockSpec((tm, d), lambda i: (i, 0)),
            _resident(wgu.shape),
            _resident(wd.shape),
            _resident(g.shape),
            _resident(b.shape),
        ],
        out_specs=pl.BlockSpec((tm, d), lambda i: (i, 0)),
        out_shape=jax.ShapeDtypeStruct((n_tok, d), F32),
        scratch_shapes=[pltpu.VMEM((tm, d), F32)],
        compiler_params=_cparams(("parallel",)),
    )(x2d, wgu, wd, g, b)


_C_AQ, _C_AK, _C_CQ, _C_CK, _C_CQR, _C_CKR = 0, 256, 384, 640, 768, 1024
_C_DQ, _C_DK, _C_BQ, _C_BKV, _C_BKR, _C_BKRR, _C_END = 1152, 1664, 2176, 2432, 2560, 2688, 2816
_R_AV, _R_CV, _R_DV, _R_END = 0, 128, 256, 512


def _split_hi_lo(v):
    hi = v.astype(BF16)
    lo = (v - hi.astype(F32)).astype(BF16)
    return hi, lo


def _prep_kernel(x_ref, wtok_ref, wvt_ref, wuq_ref, wuqr_ref, wukvk_ref, wukvvt_ref, gavg_ref,
                 gq_ref, gkv_ref, axq_ref, axqr_ref, axk_ref, axkr_ref,
                 cosb_ref, sinb_ref, cosc_ref, sinc_ref, augk_ref,
                 qa_ref, ka_ref, vta_ref, qb_ref, kb_ref, vtb_ref,
                 qc_ref, kc_ref, vtc_ref, qd_ref, kd_ref, vtd_ref):
    xb = x_ref[0].astype(BF16)
    tm = xb.shape[0]
    pm = jnp.dot(xb, wtok_ref[...], preferred_element_type=F32)
    nt = (((1,), (1,)), ((), ()))
    pvt = lax.dot_general(wvt_ref[...], xb, nt, preferred_element_type=F32)

    def ones_row_tile(width):
        row = lax.broadcasted_iota(jnp.int32, (BF16_SUBLANES, width), 0)
        return jnp.where(row == 0, 1.0, 0.0).astype(BF16)

    ones_tile, ones_tile_narrow = ones_row_tile(tm), ones_row_tile(WIN_TILE)

    def put_vt(ref, head, vt):
        width = ref.shape[-1]
        ones = ones_tile if width == tm else ones_tile_narrow
        for piece in range(tm // width):
            ref[0, head, piece, 0:HEAD_DIM, :] = vt[:, piece * width:(piece + 1) * width].astype(BF16)
            ref[0, head, piece, HEAD_DIM:V_ROWS, :] = ones

    qa_ref[0] = (pm[:, _C_AQ:_C_AQ + 256] * (HEAD_DIM ** -0.5 * LOG2E)).astype(BF16)
    ka_ref[0] = pm[:, _C_AK:_C_AK + 128].astype(BF16)
    for h in range(WIN_KV_HEADS):
        put_vt(vta_ref, h, pvt[_R_AV + 64 * h:_R_AV + 64 * (h + 1)])

    cosc = cosc_ref[...]
    sinc = sinc_ref[...]
    gavg = gavg_ref[...]

    def head_rms_scale(v):
        hi, lo = _split_hi_lo(v * v)
        ms = (jnp.dot(hi, gavg, preferred_element_type=F32)
              + jnp.dot(lo, gavg, preferred_element_type=F32))
        return lax.rsqrt(ms + NORM_EPS)

    def axial(off, off_rot, g_ref_, gr_ref_, scale):
        v = pm[:, off:off + LANES]
        r = head_rms_scale(v)
        vn = v * r * g_ref_[...]
        vrn = pm[:, off_rot:off_rot + LANES] * r * gr_ref_[...]
        return ((vn * cosc + vrn * sinc) * scale).astype(BF16)

    for grp in range(2):
        qc_ref[0, :, grp * LANES:(grp + 1) * LANES] = axial(
            _C_CQ + grp * LANES, _C_CQR + grp * LANES, axq_ref, axqr_ref, HEAD_DIM ** -0.5 * LOG2E)
    kc_ref[0] = axial(_C_CK, _C_CKR, axk_ref, axkr_ref, 1.0)
    for h in range(AX_KV_HEADS):
        put_vt(vtc_ref, h, pvt[_R_CV + 64 * h:_R_CV + 64 * (h + 1)])

    qd_ref[0] = (pm[:, _C_DQ:_C_DQ + 512] * (DIFF_QK ** -0.5 * LOG2E)).astype(BF16)
    augk = augk_ref[...]
    for h in range(DIFF_HEADS):
        kd_ref[0, :, h * LANES:(h + 1) * LANES] = (
            pm[:, _C_DK + h * LANES:_C_DK + (h + 1) * LANES] + augk).astype(BF16)
        put_vt(vtd_ref, h, pvt[_R_DV + 64 * h:_R_DV + 64 * (h + 1)])

    cosb = cosb_ref[...]
    sinb = sinb_ref[...]
    cq = pm[:, _C_BQ:_C_BQ + MLA_Q_RANK]
    cqn = (cq * lax.rsqrt(jnp.mean(cq * cq, axis=-1, keepdims=True) + NORM_EPS) * gq_ref[...]).astype(BF16)
    qw = jnp.dot(cqn, wuq_ref[...], preferred_element_type=F32)
    qwr = jnp.dot(cqn, wuqr_ref[...], preferred_element_type=F32)
    ckv = pm[:, _C_BKV:_C_BKV + MLA_KV_RANK]
    ckvn = (ckv * lax.rsqrt(jnp.mean(ckv * ckv, axis=-1, keepdims=True) + NORM_EPS) * gkv_ref[...]).astype(BF16)
    kw = jnp.dot(ckvn, wukvk_ref[...], preferred_element_type=F32)
    vbt = lax.dot_general(wukvvt_ref[...], ckvn, nt, preferred_element_type=F32)
    k_rope = pm[:, _C_BKR:_C_BKR + LANES] * cosb + pm[:, _C_BKRR:_C_BKRR + LANES] * sinb
    q_scale = (MLA_NOPE + MLA_ROPE) ** -0.5 * LOG2E
    for h in range(MLA_HEADS):
        sl = slice(h * LANES, (h + 1) * LANES)
        qb_ref[0, :, sl] = ((qw[:, sl] * cosb + qwr[:, sl] * sinb) * q_scale).astype(BF16)
        kb_ref[0, :, sl] = (kw[:, sl] + k_rope).astype(BF16)
        put_vt(vtb_ref, h, vbt[64 * h:64 * (h + 1)])


def _prep(x, p, tabs):
    bsz, seq, d = x.shape
    tm = KV_CHUNK
    nch = seq // tm
    tok = lambda w: pl.BlockSpec((1, tm, w), lambda b, i: (b, i, 0))
    tab = pl.BlockSpec((tm, LANES), lambda b, i: (i, 0))
    vt = lambda nh, width=tm: pl.BlockSpec((1, nh, tm // width, V_ROWS, width), lambda b, i: (b, 0, i, 0, 0))
    vt_shape = lambda nh, width=tm: jax.ShapeDtypeStruct((bsz, nh, seq // width, V_ROWS, width), BF16)
    tok_shape = lambda w: jax.ShapeDtypeStruct((bsz, seq, w), BF16)
    weights = [p["w_tok"], p["w_vt"], p["w_uq"], p["w_uq_rot"], p["w_ukv_k"], p["w_ukv_vt"], p["g_avg"],
               p["g_q"], p["g_kv"], p["ax_q"], p["ax_q_rot"], p["ax_k"], p["ax_k_rot"]]
    return pl.pallas_call(
        _prep_kernel,
        grid=(bsz, nch),
        in_specs=[tok(d)] + [_resident(w.shape) for w in weights] + [tab] * 5,
        out_specs=[tok(256), tok(128), vt(2, WIN_TILE), tok(512), tok(512), vt(4),
                   tok(256), tok(128), vt(2), tok(512), tok(512), vt(4)],
        out_shape=[tok_shape(256), tok_shape(128), vt_shape(2, WIN_TILE), tok_shape(512), tok_shape(512), vt_shape(4),
                   tok_shape(256), tok_shape(128), vt_shape(2), tok_shape(512), tok_shape(512), vt_shape(4)],
        compiler_params=_cparams(("parallel", "parallel")),
    )(x, *weights, tabs["cos_b"], tabs["sin_b"], tabs["cos_c"], tabs["sin_c"], tabs["aug_k"])


LOG2E = math.log2(math.e)


def _bf16_parts(x, n):
    parts = []
    for _ in range(n):
        part = float(np.float32(x).astype(jnp.bfloat16).astype(np.float32))
        parts.append(part)
        x -= part
    return tuple(parts)


LOG2E_PARTS = _bf16_parts(LOG2E, 3)


def _score_block(k_ref, q, k_slab, rb, row0, s_write, n, r, hooks):
    start = row0 + r * rb
    if not isinstance(start, int):
        start = pl.multiple_of(start, rb)
    kc = k_ref[0, pl.ds(start, rb), k_slab * LANES:k_slab * LANES + q.shape[0]]
    if hooks is not None:
        kc = hooks[0](kc, start // rb)
    s = jnp.dot(kc, q, preferred_element_type=F32)
    if hooks is not None:
        s = hooks[1](s, n, start // rb)
    s_write[n, r * rb:(r + 1) * rb, :] = s
    return jnp.max(s.reshape(rb // F32_SUBLANES, F32_SUBLANES, s.shape[-1]), axis=0)


def _first_scores(k_ref, chains, cps, rb, s_write, hooks):
    out = []
    for n, (q, k_slab, _) in enumerate(chains):
        parts = [_score_block(k_ref, q, k_slab, rb, 0, s_write, n, r, hooks) for r in range(cps * KV_CHUNK // rb)]
        out.append(functools.reduce(jnp.maximum, parts))
    return out


def _pipelined_step(k_ref, vt_ref, chains, exp_fn, cps, rb, hooks, j, carry, s_read, s_write, ahead=None):
    step_rows = cps * KV_CHUNK
    blocks_per_chunk = KV_CHUNK // rb
    ahead_chains, ahead_hooks, next_row0 = (chains, hooks, (j + 1) * step_rows) if ahead is None else (*ahead, 0)
    n_chains = len(chains)
    stats = []
    for m, _, smax in carry:
        m_new = jnp.maximum(m, jnp.max(smax, axis=0, keepdims=True))
        stats.append((m_new, exp_fn(m - m_new)))
    next_max, pv = [None] * n_chains, [None] * n_chains
    for r in range(step_rows // rb):
        for n in range(n_chains):
            q, k_slab, _ = ahead_chains[n]
            part = _score_block(k_ref, q, k_slab, rb, next_row0, s_write, n, r, ahead_hooks)
            next_max[n] = part if r == 0 else jnp.maximum(next_max[n], part)
            p = exp_fn(s_read[n, r * rb:(r + 1) * rb, :] - stats[n][0]).astype(BF16)
            c, b = divmod(r, blocks_per_chunk)
            part = jnp.dot(vt_ref[0, chains[n][2], j * cps + c, :, b * rb:(b + 1) * rb], p,
                           preferred_element_type=F32)
            pv[n] = part if pv[n] is None else pv[n] + part
    return tuple((stats[n][0], carry[n][1] * stats[n][1] + pv[n], next_max[n]) for n in range(n_chains))


def _pipelined_attention(k_ref, vt_ref, chains, next_chains, exp_fn, n_chunks, rb, s0_ref, s1_ref, max_ref,
                         is_first_tile, hooks=None, next_hooks=None):
    nq = chains[0][0].shape[1]
    cps = s0_ref.shape[1] // KV_CHUNK
    n_steps = n_chunks // cps
    step = functools.partial(_pipelined_step, k_ref, vt_ref, chains, exp_fn, cps, rb, hooks)

    @pl.when(is_first_tile)
    def _():
        for n, first_max in enumerate(_first_scores(k_ref, chains, cps, rb, s0_ref, hooks)):
            max_ref[n] = first_max

    init = tuple((jnp.full((1, nq), NEG_INF, F32), jnp.zeros((V_ROWS, nq), F32), max_ref[n])
                 for n in range(len(chains)))

    def pair(t, carry):
        carry = step(2 * t, carry, s0_ref, s1_ref)
        return step(2 * t + 1, carry, s1_ref, s0_ref)

    carry = lax.fori_loop(0, n_steps // 2 - 1, pair, init)
    carry = step(n_steps - 2, carry, s0_ref, s1_ref)
    final = step(n_steps - 1, carry, s1_ref, s0_ref, ahead=(next_chains, next_hooks))
    for n, (_, _, next_max) in enumerate(final):
        max_ref[n] = next_max
    return [acc for _, acc, _ in final]


def _normalised(acc):
    return acc[:HEAD_DIM] / acc[HEAD_DIM:HEAD_DIM + 1]


def _single_buffered(block, index_map):
    return pl.BlockSpec(block, index_map, pipeline_mode=pl.Buffered(1))


def _dense_kernel(q_ref, q_next_ref, k_ref, vt_ref, o_ref, s0_ref, s1_ref, max_ref, *, n_chunks, heads, key_lanes):
    tq = q_ref.shape[1]
    lane = lax.broadcasted_iota(jnp.int32, (tq, LANES), 1)

    def tile_chains(ref):
        chains = []
        for q_slab, k_slab, v_head, shared_key_slab in heads:
            q = ref[0, :, q_slab * LANES:(q_slab + 1) * LANES].astype(F32)
            if shared_key_slab:
                q = jnp.where((lane >> HEAD_SHIFT) == pl.program_id(1), q, 0.0)
            chains.append((q.T[:key_lanes].astype(BF16), k_slab, v_head))
        return chains

    accs = _pipelined_attention(k_ref, vt_ref, tile_chains(q_ref), tile_chains(q_next_ref), jnp.exp2, n_chunks,
                                DENSE_ROW_BLOCK, s0_ref, s1_ref, max_ref, pl.program_id(2) == 0)
    for n, acc in enumerate(accs):
        o_ref[0, n * HEAD_DIM:(n + 1) * HEAD_DIM, :] = _normalised(acc).astype(BF16)


def _dense_attention(q, k, vt, *, n_heads, kv_group, heads_per_step, key_lanes):
    bsz, seq, _ = q.shape
    nch = vt.shape[2]
    tq = min(Q_TILE, seq)
    hp = heads_per_step
    if kv_group:
        assert hp == kv_group and n_heads // kv_group == LANES // HEAD_DIM
        heads = tuple((s, 0, 0, True) for s in range(kv_group))
        q_block, q_slab_of = (1, tq, kv_group * LANES), lambda g: 0
        k_spec = pl.BlockSpec((1, seq, LANES), lambda b, g, i: (b, 0, 0))
        vt_spec = pl.BlockSpec((1, 1, nch, V_ROWS, KV_CHUNK), lambda b, g, i: (b, g, 0, 0, 0))
    else:
        heads = tuple((n, n, n, False) for n in range(hp))
        q_block, q_slab_of = (1, tq, hp * LANES), lambda g: g
        k_spec = pl.BlockSpec((1, seq, hp * LANES), lambda b, g, i: (b, 0, g))
        vt_spec = pl.BlockSpec((1, hp, nch, V_ROWS, KV_CHUNK), lambda b, g, i: (b, g, 0, 0, 0))
    n_tiles = seq // tq
    q_spec = pl.BlockSpec(q_block, lambda b, g, i: (b, i, q_slab_of(g)))
    q_next_spec = pl.BlockSpec(q_block, lambda b, g, i: (b, jnp.minimum(i + 1, n_tiles - 1), q_slab_of(g)))
    assert nch % 2 == 0
    cps = math.gcd(CHUNKS_PER_STEP, nch // 2)
    s_buffer = pltpu.VMEM((hp, cps * KV_CHUNK, tq), F32)
    return pl.pallas_call(
        functools.partial(_dense_kernel, n_chunks=nch, heads=heads, key_lanes=key_lanes),
        grid=(bsz, n_heads // hp, n_tiles),
        in_specs=[q_spec, q_next_spec, k_spec, vt_spec],
        out_specs=pl.BlockSpec((1, hp * HEAD_DIM, tq), lambda b, g, i: (b, g, i)),
        out_shape=jax.ShapeDtypeStruct((bsz, n_heads * HEAD_DIM, seq), BF16),
        scratch_shapes=[s_buffer, s_buffer, pltpu.VMEM((hp, F32_SUBLANES, tq), F32)],
        compiler_params=_cparams(("parallel", "parallel", "arbitrary"), ATTENTION_FLAGS),
    )(q, q, k, vt)


_AUG = 2 * DIFF_QK
DIFF_KEY_LANES = -(-(_AUG + 4 * len(LOG2E_PARTS)) // BF16_SUBLANES) * BF16_SUBLANES


def _diff_kernel(lam_ref, g_ref, q_ref, q_next_ref, k_ref, vt_ref, o_ref, s0_ref, s1_ref, max_ref, bias_ref, *,
                 n_chunks, hp, lam_init, slope_exp0):
    i = pl.program_id(2)
    tq = q_ref.shape[1]
    lane = lax.broadcasted_iota(jnp.int32, (tq, LANES), 1)
    digit = (lane - _AUG) & 3
    part = (lane - _AUG) >> 2
    is_bias_lane = (lane >= _AUG) & (lane < _AUG + 4 * len(LOG2E_PARTS))
    log2e_part = functools.reduce(lambda acc, x: jnp.where(part == x[0], x[1], acc),
                                  enumerate(LOG2E_PARTS), jnp.zeros((tq, LANES), F32))
    row = lax.broadcasted_iota(jnp.int32, (tq, LANES), 0)
    own_dist = jnp.abs(lax.broadcasted_iota(jnp.int32, (ROW_BLOCK, tq), 1)
                       - lax.broadcasted_iota(jnp.int32, (ROW_BLOCK, tq), 0)).astype(F32)
    key_lane = lax.broadcasted_iota(jnp.int32, (1, DIFF_KEY_LANES), 1)
    key_bias_lane = (key_lane >= _AUG) & (key_lane < _AUG + 4 * len(LOG2E_PARTS))

    slopes = []
    for n in range(hp):
        head = pl.program_id(1) * hp + n
        slope_bits = jnp.full((1, LANES), 127 - slope_exp0, jnp.int32) - head
        slope = lax.bitcast_convert_type(slope_bits << 23, F32)
        slopes.append(slope)
        bias = (-LOG2E) * slope[:, :1] * own_dist
        bias_ref[n, 0] = jnp.zeros((ROW_BLOCK, 2 * tq), F32)
        bias_ref[n, 1] = jnp.concatenate([bias, bias], axis=1)

    def tile(ref, t):
        qpos = t * tq + row
        q_hi = ((qpos >> 7) << 7).astype(F32)
        q_lo = (qpos & 127).astype(F32)
        chains = []
        for n, slope in enumerate(slopes):
            qf = ref[0, :, n * LANES:(n + 1) * LANES].astype(F32)
            aug = jnp.where(digit == 0, -slope * q_hi, jnp.where(digit == 1, -slope * q_lo, slope * log2e_part))
            aug = jnp.where(is_bias_lane, aug, 0.0)
            q1 = jnp.where(lane < DIFF_QK, qf, 0.0)
            q2 = jnp.where((lane >= DIFF_QK) & (lane < 2 * DIFF_QK), qf, 0.0)
            qt = jnp.concatenate([q1 + aug, q2 + aug], axis=0).T
            chains.append((qt[:DIFF_KEY_LANES].astype(BF16), n, n))

        def edit_keys(kc, block):
            sign = jnp.where(block < t, 1.0, jnp.where(block > t, -1.0, 0.0))
            return kc * jnp.where(key_bias_lane, sign, 1.0).astype(BF16)

        def add_bias(s, n, block):
            return s + bias_ref[n, jnp.where(block == t, 1, 0)]

        return chains, (edit_keys, add_bias)

    chains, hooks = tile(q_ref, i)
    next_chains, next_hooks = tile(q_next_ref, jnp.minimum(i + 1, pl.num_programs(2) - 1))
    accs = _pipelined_attention(k_ref, vt_ref, chains, next_chains, jnp.exp2, n_chunks, ROW_BLOCK, s0_ref, s1_ref,
                                max_ref, i == 0, hooks, next_hooks)

    lp = lam_ref[...]
    lam = (jnp.exp(jnp.sum(lp[0:1] * lp[1:2], axis=-1, keepdims=True))
           - jnp.exp(jnp.sum(lp[2:3] * lp[3:4], axis=-1, keepdims=True)) + lam_init)
    for n, acc in enumerate(accs):
        o = _normalised(acc[:, :tq]) - lam * _normalised(acc[:, tq:])
        o = o * lax.rsqrt(jnp.mean(o * o, axis=0, keepdims=True) + NORM_EPS)
        o_ref[0, n * DIFF_V:(n + 1) * DIFF_V, :] = (o * g_ref[...] * (1.0 - lam_init)).astype(BF16)


def _diff_attention(q, k, vt, lam_params, subln_col, *, layer_idx, slope_exp0, heads_per_step):
    bsz, seq, _ = q.shape
    nch = vt.shape[2]
    tq = ROW_BLOCK
    hp = heads_per_step
    assert seq % tq == 0 and seq <= 256 * LANES and DIFF_HEADS % hp == 0 and nch % 2 == 0
    lam_init = 0.8 - 0.6 * math.exp(-0.3 * layer_idx)
    cps = math.gcd(CHUNKS_PER_STEP, nch // 2)
    s_buffer = pltpu.VMEM((hp, cps * KV_CHUNK, 2 * tq), F32)
    n_tiles = seq // tq
    return pl.pallas_call(
        functools.partial(_diff_kernel, n_chunks=nch, hp=hp, lam_init=lam_init, slope_exp0=slope_exp0),
        grid=(bsz, DIFF_HEADS // hp, n_tiles),
        in_specs=[
            _resident(lam_params.shape),
            _resident(subln_col.shape),
            pl.BlockSpec((1, tq, hp * LANES), lambda b, g, i: (b, i, g)),
            pl.BlockSpec((1, tq, hp * LANES), lambda b, g, i: (b, jnp.minimum(i + 1, n_tiles - 1), g)),
            pl.BlockSpec((1, seq, hp * LANES), lambda b, g, i: (b, 0, g)),
            pl.BlockSpec((1, hp, nch, V_ROWS, KV_CHUNK), lambda b, g, i: (b, g, 0, 0, 0)),
        ],
        out_specs=pl.BlockSpec((1, hp * DIFF_V, tq), lambda b, g, i: (b, g, i)),
        out_shape=jax.ShapeDtypeStruct((bsz, DIFF_HEADS * DIFF_V, seq), BF16),
        scratch_shapes=[s_buffer, s_buffer, pltpu.VMEM((hp, F32_SUBLANES, 2 * tq), F32),
                        pltpu.VMEM((hp, 2, ROW_BLOCK, 2 * tq), F32)],
        compiler_params=_cparams(("parallel", "parallel", "arbitrary"), ATTENTION_FLAGS),
    )(lam_params, subln_col, q, q, k, vt)


_WIN_OFFSETS = (-1, 0, 1)
_WIN_NO_BLOCK = len(_WIN_OFFSETS)


def _window_bias_tiles():
    key = np.arange(WIN_TILE)[:, None]
    query = np.arange(WIN_TILE)[None, :]
    tiles = np.full((WIN_HEADS, _WIN_NO_BLOCK + 1, WIN_TILE, WIN_TILE), NEG_INF, np.float32)
    for head in range(WIN_HEADS):
        for sel, offset in enumerate(_WIN_OFFSETS):
            dist = np.abs(query - key - WIN_TILE * offset)
            tiles[head, sel] = np.where(dist <= WINDOW, -(2.0 ** -(head + 1)) * LOG2E * dist, NEG_INF)
    return tiles


def _window_kernel(sink_ref, bias_ref, q_ref, k_ref, vt_ref, o_ref, *, n_blocks):
    i = pl.program_id(1)
    lane = lax.broadcasted_iota(jnp.int32, (WIN_TILE, LANES), 1)
    group = WIN_HEADS // WIN_KV_HEADS
    blocks = []
    for offset in _WIN_OFFSETS:
        block = i + offset
        exists = (block >= 0) & (block < n_blocks)
        block = jnp.clip(block, 0, n_blocks - 1)
        keys = k_ref[0, pl.ds(pl.multiple_of(block * WIN_TILE, WIN_TILE), WIN_TILE), :]
        blocks.append((jnp.where(exists, offset + 1, _WIN_NO_BLOCK), block, keys))
    scores = []
    for head in range(WIN_HEADS):
        kv_head, slab = head // group, head % group
        qf = q_ref[0, :, slab * LANES:(slab + 1) * LANES].astype(F32)
        qt = jnp.where((lane >> HEAD_SHIFT) == kv_head, qf, 0.0).T.astype(BF16)
        scores.append([jnp.dot(keys, qt, preferred_element_type=F32) + bias_ref[head, sel]
                       for sel, _, keys in blocks])
    probs = []
    for head in range(WIN_HEADS):
        sink = jnp.full((1, WIN_TILE), sink_ref[head] * LOG2E, F32)
        m = functools.reduce(jnp.maximum, [jnp.max(s, axis=0, keepdims=True) for s in scores[head]], sink)
        probs.append(([jnp.exp2(s - m).astype(BF16) for s in scores[head]], jnp.exp2(sink - m)))
    for head in range(WIN_HEADS):
        acc = None
        for p, (_, block, _) in zip(probs[head][0], blocks):
            part = jnp.dot(vt_ref[0, head // group, block], p, preferred_element_type=F32)
            acc = part if acc is None else acc + part
        denominator = acc[HEAD_DIM:HEAD_DIM + 1] + probs[head][1]
        o_ref[0, head * HEAD_DIM:(head + 1) * HEAD_DIM, :] = (acc[:HEAD_DIM] / denominator).astype(BF16)


def _window_attention(q, k, vt, sink):
    bsz, seq, _ = q.shape
    n_blocks = vt.shape[2]
    bias = jnp.asarray(_window_bias_tiles())
    return pl.pallas_call(
        functools.partial(_window_kernel, n_blocks=n_blocks),
        grid=(bsz, n_blocks),
        in_specs=[
            pl.BlockSpec(memory_space=pltpu.SMEM),
            _resident(bias.shape),
            pl.BlockSpec((1, WIN_TILE, 2 * LANES), lambda b, i: (b, i, 0)),
            _single_buffered((1, seq, LANES), lambda b, i: (b, 0, 0)),
            _single_buffered((1, WIN_KV_HEADS, n_blocks, V_ROWS, WIN_TILE), lambda b, i: (b, 0, 0, 0, 0)),
        ],
        out_specs=pl.BlockSpec((1, WIN_HEADS * HEAD_DIM, WIN_TILE), lambda b, i: (b, 0, i)),
        out_shape=jax.ShapeDtypeStruct((bsz, WIN_HEADS * HEAD_DIM, seq), BF16),
        compiler_params=_cparams(("parallel", "arbitrary")),
    )(sink, bias, q, k, vt)


def _outproj_kernel(x_ref, oa_ref, ob_ref, oc_ref, od_ref, wt_ref, g_ref, b_ref, o_ref, *, alpha):
    mix_t = None
    for m, ref in enumerate((oa_ref, ob_ref, oc_ref, od_ref)):
        part = jnp.dot(wt_ref[:, m * 256:(m + 1) * 256], ref[0], preferred_element_type=F32)
        mix_t = part if mix_t is None else mix_t + part
    y = alpha * x_ref[0] + mix_t.T
    o_ref[0] = _layer_norm_rows(y, g_ref[...], b_ref[...])


def _outproj(x, o_a, o_b, o_c, o_d, w_out_t, g, b, alpha):
    bsz, seq, d = x.shape
    tm = min(TOK_TILE, seq)
    ot = pl.BlockSpec((1, 256, tm), lambda bb, i: (bb, 0, i))
    tok = pl.BlockSpec((1, tm, d), lambda bb, i: (bb, i, 0))
    return pl.pallas_call(
        functools.partial(_outproj_kernel, alpha=alpha),
        grid=(bsz, seq // tm),
        in_specs=[tok, ot, ot, ot, ot, _resident(w_out_t.shape), _resident(g.shape), _resident(b.shape)],
        out_specs=tok,
        out_shape=jax.ShapeDtypeStruct((bsz, seq, d), F32),
        compiler_params=_cparams(("parallel", "parallel")),
    )(x, o_a, o_b, o_c, o_d, w_out_t, g, b)


def _rot_half_cols(start, width):
    half = width // 2
    src = np.concatenate([np.arange(start + half, start + width), np.arange(start, start + half)])
    sgn = np.concatenate([-np.ones(half), np.ones(half)])
    return src, sgn


def _gqa_slab_order(n_heads, n_kv):
    group = n_heads // n_kv
    return [kv * group + s for s in range(group) for kv in range(n_kv)]


def _layer_params(w_in, mla_q_norm, mla_w_uq, mla_kv_norm, mla_w_ukv, ax_q_norm, ax_k_norm):
    d = w_in.shape[0]
    zeros = lambda n: jnp.zeros((d, n), w_in.dtype)
    a0, b0, c0, d0 = 0, A_COLS, A_COLS + B_COLS, A_COLS + B_COLS + C_COLS
    head_cols = lambda base, h: np.arange(base + h * HEAD_DIM, base + (h + 1) * HEAD_DIM)

    def axial_rot(cols):
        src, sgn = [], []
        for blk in range(0, HEAD_DIM, HEAD_DIM // 2):
            s_, g_ = _rot_half_cols(blk, HEAD_DIM // 2)
            src.append(cols[s_])
            sgn.append(g_)
        return np.concatenate(src), np.concatenate(sgn)

    order = _gqa_slab_order(AX_HEADS, AX_KV_HEADS)
    aq = np.concatenate([head_cols(a0, h) for h in order])
    ak = np.arange(a0 + 256, a0 + 384)
    cq = np.concatenate([head_cols(c0, h) for h in order])
    ck = np.arange(c0 + 256, c0 + 384)
    cq_rot = [axial_rot(head_cols(c0, h)) for h in order]
    ck_rot = [axial_rot(head_cols(c0 + 256, h)) for h in range(AX_KV_HEADS)]

    def gather(src, sgn=None):
        w = w_in[:, np.asarray(src)]
        return w if sgn is None else w * jnp.asarray(sgn, w.dtype)[None, :]

    def widen(cols):
        return jnp.concatenate([gather(cols), zeros(LANES - len(cols))], axis=1)

    kr = np.arange(b0 + MLA_Q_RANK + MLA_KV_RANK, b0 + B_COLS)
    kr_src, kr_sgn = _rot_half_cols(kr[0], MLA_ROPE)
    place_rope = lambda w: jnp.concatenate([zeros(MLA_NOPE), w, zeros(LANES - MLA_NOPE - MLA_ROPE)], axis=1)

    w_tok = jnp.concatenate(
        [gather(aq), gather(ak), gather(cq), gather(ck)]
        + [gather(s, g) for s, g in cq_rot] + [gather(s, g) for s, g in ck_rot]
        + [widen(head_cols(d0, h)) for h in range(DIFF_HEADS)]
        + [widen(head_cols(d0 + 256, h)) for h in range(DIFF_HEADS)]
        + [gather(np.arange(b0, b0 + MLA_Q_RANK + MLA_KV_RANK)),
           place_rope(gather(kr)), place_rope(gather(kr_src, kr_sgn))], axis=1)
    assert w_tok.shape[1] == _C_END
    v_cols = np.concatenate([np.arange(a0 + 384, a0 + 512), np.arange(c0 + 384, c0 + 512),
                             np.arange(d0 + 512, d0 + 768)])
    w_vt = w_in[:, v_cols].T

    qd = MLA_NOPE + MLA_ROPE
    zq = lambda n: jnp.zeros((MLA_Q_RANK, n), mla_w_uq.dtype)
    uq, uq_rot = [], []
    for h in range(MLA_HEADS):
        blk = mla_w_uq[:, h * qd:(h + 1) * qd]
        src, sgn = _rot_half_cols(MLA_NOPE, MLA_ROPE)
        rot = blk[:, src] * jnp.asarray(sgn, blk.dtype)[None, :]
        uq += [blk, zq(LANES - qd)]
        uq_rot += [zq(MLA_NOPE), rot, zq(LANES - qd)]
    ukv = mla_w_ukv.reshape(MLA_KV_RANK, MLA_HEADS, MLA_NOPE + MLA_V)
    ukv_k = jnp.concatenate([ukv[:, :, :MLA_NOPE], jnp.zeros((MLA_KV_RANK, MLA_HEADS, LANES - MLA_NOPE), ukv.dtype)],
                            axis=2).reshape(MLA_KV_RANK, MLA_HEADS * LANES)
    ukv_vt = ukv[:, :, MLA_NOPE:].reshape(MLA_KV_RANK, MLA_HEADS * MLA_V).T

    def ax_gain(g):
        src, sgn = axial_rot(np.arange(HEAD_DIM))
        return jnp.tile(g, 2)[None, :].astype(F32), jnp.tile(g[src], 2)[None, :].astype(F32)

    ax_q, ax_q_rot = ax_gain(ax_q_norm)
    ax_k, ax_k_rot = ax_gain(ax_k_norm)
    lane = np.arange(LANES)
    g_avg = (lane[:, None] // HEAD_DIM == lane[None, :] // HEAD_DIM).astype(np.float32) / HEAD_DIM
    return {
        "w_tok": w_tok.astype(BF16), "w_vt": w_vt.astype(BF16),
        "w_uq": jnp.concatenate(uq, axis=1).astype(BF16), "w_uq_rot": jnp.concatenate(uq_rot, axis=1).astype(BF16),
        "w_ukv_k": ukv_k.astype(BF16), "w_ukv_vt": ukv_vt.astype(BF16),
        "g_avg": jnp.asarray(g_avg, BF16),
        "g_q": mla_q_norm[None, :].astype(F32), "g_kv": mla_kv_norm[None, :].astype(F32),
        "ax_q": ax_q, "ax_q_rot": ax_q_rot, "ax_k": ax_k, "ax_k_rot": ax_k_rot,
    }


def _position_tables(seq):
    pos = jnp.arange(seq, dtype=jnp.int32)
    half = HEAD_DIM // 2

    def angles(p, dim):
        inv = ROPE_BASE ** (-jnp.arange(0, dim, 2, dtype=F32) / dim)
        ang = p.astype(F32)[:, None] * inv[None, :]
        return jnp.cos(ang), jnp.sin(ang)

    cb, sb = angles(pos, MLA_ROPE)
    pad = LANES - MLA_NOPE - MLA_ROPE
    cos_b = jnp.concatenate([jnp.ones((seq, MLA_NOPE), F32), cb, cb, jnp.zeros((seq, pad), F32)], axis=1)
    sin_b = jnp.concatenate([jnp.zeros((seq, MLA_NOPE), F32), sb, sb, jnp.zeros((seq, pad), F32)], axis=1)
    cr, sr = angles(pos // GRID_W, half)
    cc, sc = angles(pos % GRID_W, half)
    cos_c = jnp.tile(jnp.concatenate([cr, cr, cc, cc], axis=1), (1, 2))
    sin_c = jnp.tile(jnp.concatenate([sr, sr, sc, sc], axis=1), (1, 2))
    rel = np.arange(LANES) - _AUG
    in_aug = (rel >= 0) & (rel < 4 * len(LOG2E_PARTS))
    part_row = np.where(in_aug, np.asarray(LOG2E_PARTS + (0.0,), np.float32)[np.clip(rel >> 2, 0, len(LOG2E_PARTS))], 0.0)
    digit = jnp.asarray((rel & 3)[None, :])
    pos_hi = ((pos >> 7) << 7).astype(F32)[:, None]
    pos_lo = (pos & 127).astype(F32)[:, None]
    aug = jnp.where(digit < 2, jnp.asarray(part_row, F32)[None, :], jnp.where(digit == 2, pos_hi, pos_lo))
    aug = jnp.where(jnp.asarray(in_aug[None, :]), aug, 0.0)
    return {"cos_b": cos_b, "sin_b": sin_b, "cos_c": cos_c, "sin_c": sin_c, "aug_k": aug}


def _ffn_weights(w_gu, w_down):
    *lead, d, two_ff = w_gu.shape
    n_chunks = two_ff // 2 // FF_CHUNK
    nl = len(lead)
    wgu = w_gu.astype(BF16).reshape(*lead, d, 2, n_chunks, FF_CHUNK)
    wgu = wgu.transpose(*range(nl), nl + 2, nl, nl + 1, nl + 3).reshape(*lead, n_chunks, d, 2 * FF_CHUNK)
    wd = w_down.astype(BF16).reshape(*lead, n_chunks, FF_CHUNK, d)
    return wgu, wd


def kernel(x, w_in, win_sink, mla_q_norm, mla_w_uq, mla_kv_norm, mla_w_ukv, ax_q_norm, ax_k_norm,
           diff_lambda, diff_subln, w_out, ffn_w_gu, ffn_w_down, ln_g, ln_b):
    depth = w_in.shape[0]
    bsz, seq, d = x.shape
    alpha = (2 * depth) ** 0.25
    tabs = _position_tables(seq)
    wgu_all, wd_all = _ffn_weights(ffn_w_gu, ffn_w_down)
    w_out_t = jnp.swapaxes(w_out, 1, 2).astype(BF16)
    row = lambda v: v[None, :].astype(F32)
    for l in range(depth):
        x = _ffn(x.reshape(bsz * seq, d), wgu_all[l, 0], wd_all[l, 0], row(ln_g[l, 0]), row(ln_b[l, 0]), alpha).reshape(bsz, seq, d)

        p = _layer_params(w_in[l], mla_q_norm[l], mla_w_uq[l], mla_kv_norm[l], mla_w_ukv[l],
                          ax_q_norm[l], ax_k_norm[l])
        qa, ka, vta, qb, kb, vtb, qc, kc, vtc, qd, kd, vtd = _prep(x, p, tabs)
        o_a = _window_attention(qa, ka, vta, win_sink[l].astype(F32))
        o_b = _dense_attention(qb, kb, vtb, n_heads=MLA_HEADS, kv_group=0, heads_per_step=HEADS_PER_STEP,
                               key_lanes=MLA_NOPE + MLA_ROPE)
        o_c = _dense_attention(qc, kc, vtc, n_heads=AX_HEADS, kv_group=AX_HEADS // AX_KV_HEADS,
                               heads_per_step=AX_HEADS // AX_KV_HEADS, key_lanes=LANES)
        o_d = _diff_attention(qd, kd, vtd, diff_lambda[l].astype(F32), diff_subln[l][:, None].astype(F32),
                              layer_idx=l, slope_exp0=WIN_HEADS + 1, heads_per_step=HEADS_PER_STEP)
        x = _outproj(x, o_a, o_b, o_c, o_d, w_out_t[l], row(ln_g[l, 1]), row(ln_b[l, 1]), alpha)

        x = _ffn(x.reshape(bsz * seq, d), wgu_all[l, 1], wd_all[l, 1], row(ln_g[l, 2]), row(ln_b[l, 2]), alpha).reshape(bsz, seq, d)
    return x
```
